```python
import jax
import jax.numpy as jnp
from jax import lax
import numpy as np

D_MODEL = 1024
BATCH = 8
SEQ = 4096
DEPTH = 1

LRU_WIDTH = D_MODEL
LRU_BLOCK_W = 256
LRU_BLOCKS = LRU_WIDTH // LRU_BLOCK_W
CONV_W = 4
LRU_C = 8.0
N_HEADS = 8
HEAD_DIM = D_MODEL // N_HEADS
ATTN_WIDTH = N_HEADS * HEAD_DIM
Q_BLOCK = 128
N_GROUPS = 4
EXPERTS_PER_GROUP = 8
N_EXPERTS = N_GROUPS * EXPERTS_PER_GROUP
TOP_K = 2
D_EXPERT = D_MODEL // 2
ROUTE_BLOCK = 128
EPS = 1e-6
IN_WIDTHS = (LRU_WIDTH, LRU_WIDTH, ATTN_WIDTH, ATTN_WIDTH, ATTN_WIDTH, N_HEADS, D_MODEL, D_MODEL)
IN_WIDTH = sum(IN_WIDTHS)

kernel_name = 'hybrid_rglru_fox_hmoe'


def rms_norm(x, g):
    x32 = x.astype(jnp.float32)
    y = x32 * lax.rsqrt(jnp.mean(x32 * x32, axis=-1, keepdims=True) + EPS)
    return (y * g.astype(jnp.float32)).astype(x.dtype)


def causal_depthwise_conv(x, w, b):
    y = lax.conv_general_dilated(
        x, w[:, None, :].astype(x.dtype), window_strides=(1,),
        padding=[(CONV_W - 1, 0)], dimension_numbers=('NWC', 'WIO', 'NWC'),
        feature_group_count=x.shape[-1])
    return y + b.astype(x.dtype)


def _linear_recurrence_combine(e1, e2):
    a1, b1 = e1
    a2, b2 = e2
    return a1 * a2, a2 * b1 + b2


def rg_lru(xc, w_a, b_a, w_x, b_x, lam):
    B, S, C = xc.shape
    xb = xc.reshape(B, S, LRU_BLOCKS, LRU_BLOCK_W)
    r = jax.nn.sigmoid(jnp.einsum('bsnc,ncd->bsnd', xb, w_a).reshape(B, S, C).astype(jnp.float32) + b_a.astype(jnp.float32))
    i = jax.nn.sigmoid(jnp.einsum('bsnc,ncd->bsnd', xb, w_x).reshape(B, S, C).astype(jnp.float32) + b_x.astype(jnp.float32))
    log_a = -LRU_C * r * jax.nn.softplus(-lam.astype(jnp.float32))
    a = jnp.exp(log_a)
    mult = jnp.sqrt(-jnp.expm1(2.0 * log_a))
    b = mult * i * xc.astype(jnp.float32)
    _, h = lax.associative_scan(_linear_recurrence_combine, (a, b), axis=1)
    return h.astype(xc.dtype)


def forgetting_attention(q, k, v, log_f):
    B, S, H, Dh = q.shape
    scale = Dh ** -0.5
    c = jnp.cumsum(log_f, axis=1).transpose(0, 2, 1)
    outs = []
    for qb in range(S // Q_BLOCK):
        lo = qb * Q_BLOCK
        hi = lo + Q_BLOCK
        s = jnp.einsum('bqhd,bkhd->bhqk', q[:, lo:hi], k[:, :hi], preferred_element_type=jnp.float32) * scale
        s = s + c[:, :, lo:hi, None] - c[:, :, None, :hi]
        causal = jnp.arange(hi)[None, :] <= (lo + jnp.arange(Q_BLOCK))[:, None]
        s = jnp.where(causal, s, -jnp.inf)
        p = jax.nn.softmax(s, axis=-1)
        outs.append(jnp.einsum('bhqk,bkhd->bqhd', p.astype(v.dtype), v[:, :hi]))
    return jnp.concatenate(outs, axis=1).reshape(B, S, H * Dh)


def parallel_mixer(u, w_in, conv_w, conv_b, w_rg_a, b_rg_a, w_rg_x, b_rg_x, lam, b_forget,
                   w_lru_out, w_attn_out, w_out):
    B, S, _ = u.shape
    z = u @ w_in
    split_points = np.cumsum(IN_WIDTHS)[:-1].tolist()
    xr, gr, q, k, v, fl, m_lru, m_attn = jnp.split(z, split_points, axis=-1)
    h = rg_lru(causal_depthwise_conv(xr, conv_w, conv_b), w_rg_a, b_rg_a, w_rg_x, b_rg_x, lam)
    y_lru = (h * jax.nn.gelu(gr)) @ w_lru_out
    log_f = jax.nn.log_sigmoid(fl.astype(jnp.float32) + b_forget.astype(jnp.float32))
    o = forgetting_attention(q.reshape(B, S, N_HEADS, HEAD_DIM), k.reshape(B, S, N_HEADS, HEAD_DIM),
                             v.reshape(B, S, N_HEADS, HEAD_DIM), log_f)
    y_attn = o @ w_attn_out
    merged = jax.nn.sigmoid(m_lru) * y_lru + jax.nn.sigmoid(m_attn) * y_attn
    return merged @ w_out


def hierarchical_moe(xn, w_rg, b_rg, w_re, b_re, w1, w3, w2):
    B, S, D = xn.shape
    T = B * S
    xt = xn.reshape(T, D)
    g_logits = (xt @ w_rg).astype(jnp.float32) + b_rg.astype(jnp.float32)
    g_prob = jax.nn.softmax(g_logits, axis=-1)
    _, g_sel = lax.top_k(g_logits, 1)
    p_group = jnp.take_along_axis(g_prob, g_sel, axis=1)
    e_logits = ((xt @ w_re).astype(jnp.float32) + b_re.astype(jnp.float32)).reshape(T, N_GROUPS, EXPERTS_PER_GROUP)
    e_in_group = jnp.take_along_axis(e_logits, g_sel[:, :, None], axis=1)[:, 0]
    top_v, top_i = lax.top_k(e_in_group, TOP_K)
    comb_w = p_group * jax.nn.softmax(top_v, axis=-1)
    expert = g_sel * EXPERTS_PER_GROUP + top_i
    N = T * TOP_K
    flat_e = expert.reshape(N)
    flat_w = comb_w.reshape(N)
    flat_t = jnp.arange(N) // TOP_K
    order = jnp.argsort(flat_e, stable=True)
    sorted_e = flat_e[order]
    counts = jnp.bincount(flat_e, length=N_EXPERTS)
    padded = ((counts + ROUTE_BLOCK - 1) // ROUTE_BLOCK) * ROUTE_BLOCK
    pad_end = jnp.cumsum(padded)
    pad_start = pad_end - padded
    seg_start = jnp.cumsum(counts) - counts
    dest = pad_start[sorted_e] + jnp.arange(N) - seg_start[sorted_e]
    n_blocks = (N + N_EXPERTS * (ROUTE_BLOCK - 1) + ROUTE_BLOCK - 1) // ROUTE_BLOCK
    P = n_blocks * ROUTE_BLOCK
    slot_tok = jnp.full((P,), T, dtype=jnp.int32).at[dest].set(flat_t[order].astype(jnp.int32))
    slot_w = jnp.zeros((P,), xn.dtype).at[dest].set(flat_w[order].astype(xn.dtype))
    blk_expert = jnp.minimum(jnp.searchsorted(pad_end, jnp.arange(n_blocks) * ROUTE_BLOCK, side='right'), N_EXPERTS - 1)
    x_pad = jnp.concatenate([xt, jnp.zeros((1, D), xt.dtype)], axis=0)

    def expert_block(args):
        tok, e = args
        xb = x_pad[tok]
        hb = jax.nn.silu(xb @ w1[e]) * (xb @ w3[e])
        return hb @ w2[e]

    y = lax.map(expert_block, (slot_tok.reshape(n_blocks, ROUTE_BLOCK), blk_expert)).reshape(P, D)
    out = jnp.zeros((T + 1, D), xn.dtype).at[slot_tok].add(y * slot_w[:, None])[:T]
    return out.reshape(B, S, D)


def setup_inputs(seed: int = 0) -> dict:
    key = jax.random.key(seed)
    ks = jax.random.split(key, 23)
    f32 = jnp.float32

    def nrm(k, shape, fan_in):
        return jax.random.normal(k, shape, f32) * (fan_in ** -0.5)

    u = jax.random.uniform(ks[9], (DEPTH, LRU_WIDTH), f32, minval=0.9, maxval=0.999)
    a0 = u ** (1.0 / LRU_C)
    return {
        'x': jax.random.normal(ks[0], (BATCH, SEQ, D_MODEL), f32),
        'g_mix': 1.0 + 0.05 * jax.random.normal(ks[1], (DEPTH, D_MODEL), f32),
        'w_in': nrm(ks[2], (DEPTH, D_MODEL, IN_WIDTH), D_MODEL),
        'conv_w': nrm(ks[3], (DEPTH, CONV_W, LRU_WIDTH), CONV_W),
        'conv_b': 0.02 * jax.random.normal(ks[4], (DEPTH, LRU_WIDTH), f32),
        'w_rg_a': nrm(ks[5], (DEPTH, LRU_BLOCKS, LRU_BLOCK_W, LRU_BLOCK_W), LRU_BLOCK_W),
        'b_rg_a': 0.02 * jax.random.normal(ks[6], (DEPTH, LRU_WIDTH), f32),
        'w_rg_x': nrm(ks[7], (DEPTH, LRU_BLOCKS, LRU_BLOCK_W, LRU_BLOCK_W), LRU_BLOCK_W),
        'b_rg_x': 0.02 * jax.random.normal(ks[8], (DEPTH, LRU_WIDTH), f32),
        'lru_lambda': jnp.log(a0) - jnp.log1p(-a0),
        'b_forget': 2.0 + 0.1 * jax.random.normal(ks[10], (DEPTH, N_HEADS), f32),
        'w_lru_out': nrm(ks[11], (DEPTH, LRU_WIDTH, D_MODEL), LRU_WIDTH),
        'w_attn_out': nrm(ks[12], (DEPTH, ATTN_WIDTH, D_MODEL), ATTN_WIDTH),
        'w_out': nrm(ks[13], (DEPTH, D_MODEL, D_MODEL), D_MODEL),
        'g_ffn': 1.0 + 0.05 * jax.random.normal(ks[14], (DEPTH, D_MODEL), f32),
        'w_route_group': nrm(ks[15], (DEPTH, D_MODEL, N_GROUPS), D_MODEL),
        'b_route_group': 0.01 * jax.random.normal(ks[16], (DEPTH, N_GROUPS), f32),
        'w_route_expert': nrm(ks[17], (DEPTH, D_MODEL, N_EXPERTS), D_MODEL),
        'b_route_expert': 0.01 * jax.random.normal(ks[18], (DEPTH, N_EXPERTS), f32),
        'w_exp_gate': nrm(ks[19], (DEPTH, N_EXPERTS, D_MODEL, D_EXPERT), D_MODEL),
        'w_exp_up': nrm(ks[20], (DEPTH, N_EXPERTS, D_MODEL, D_EXPERT), D_MODEL),
        'w_exp_down': nrm(ks[21], (DEPTH, N_EXPERTS, D_EXPERT, D_MODEL), D_EXPERT),
        'g_final': 1.0 + 0.05 * jax.random.normal(ks[22], (D_MODEL,), f32),
    }


def reference(x, g_mix, w_in, conv_w, conv_b, w_rg_a, b_rg_a, w_rg_x, b_rg_x, lru_lambda, b_forget,
              w_lru_out, w_attn_out, w_out, g_ffn, w_route_group, b_route_group, w_route_expert,
              b_route_expert, w_exp_gate, w_exp_up, w_exp_down, g_final):
    for l in range(DEPTH):
        u = rms_norm(x, g_mix[l])
        x = x + parallel_mixer(u, w_in[l], conv_w[l], conv_b[l], w_rg_a[l], b_rg_a[l], w_rg_x[l], b_rg_x[l],
                               lru_lambda[l], b_forget[l], w_lru_out[l], w_attn_out[l], w_out[l])
        v = rms_norm(x, g_ffn[l])
        x = x + hierarchical_moe(v, w_route_group[l], b_route_group[l], w_route_expert[l], b_route_expert[l],
                                 w_exp_gate[l], w_exp_up[l], w_exp_down[l])
    return rms_norm(x, g_final)
```

```python
import functools

import jax
import jax.numpy as jnp
from jax import lax
from jax.experimental import pallas as pl
from jax.experimental.pallas import tpu as pltpu

F32 = jnp.float32
BF16 = jnp.bfloat16

D_MODEL = 1024
LRU_BLOCK_W = 256
LRU_BLOCKS = D_MODEL // LRU_BLOCK_W
CONV_W = 4
LRU_C = 8.0
N_HEADS = 8
HEAD_DIM = D_MODEL // N_HEADS
N_GROUPS = 4
EXPERTS_PER_GROUP = 8
N_EXPERTS = N_GROUPS * EXPERTS_PER_GROUP
TOP_K = 2
D_EXPERT = D_MODEL // 2
EPS = 1e-6

LANES = 128
SUBLANES = 8
ROW_CHUNKS = D_MODEL // LANES
VMEM_LIMIT = 48 * 1024 * 1024

ZC_XR, ZC_GR, ZC_Q, ZC_K, ZC_V, ZC_ML, ZC_MA = 0, 8, 16, 24, 32, 40, 48
Z_WIDTH = 7 * D_MODEL

R_EXP0 = N_GROUPS

EXPERT_ROWS = 128


def _cparams(sem):
    return pltpu.CompilerParams(dimension_semantics=sem, vmem_limit_bytes=VMEM_LIMIT)


def _inproj_body(x_ref, g_ref, w_ref, wfl_ref, z_ref, fl_ref, u_ref):
    @pl.when(pl.program_id(1) == 0)
    def _():
        x = x_ref[...]
        ms = jnp.mean(x * x, axis=-1, keepdims=True)
        u = (x * lax.rsqrt(ms + EPS) * g_ref[...]).astype(BF16)
        u_ref[...] = u
        fl_ref[...] = jnp.dot(u, wfl_ref[...], preferred_element_type=F32)

    z_ref[...] = jnp.dot(u_ref[...], w_ref[...], preferred_element_type=F32).astype(BF16)


def _inproj(x2, g, w_main, w_fl, *, tm=1024, tn=1024):
    T = x2.shape[0]
    return pl.pallas_call(
        _inproj_body,
        grid=(T // tm, Z_WIDTH // tn),
        in_specs=[
            pl.BlockSpec((tm, D_MODEL), lambda i, j: (i, 0)),
            pl.BlockSpec((1, D_MODEL), lambda i, j: (0, 0)),
            pl.BlockSpec((D_MODEL, tn), lambda i, j: (0, j)),
            pl.BlockSpec((D_MODEL, LANES), lambda i, j: (0, 0)),
        ],
        out_specs=[
            pl.BlockSpec((tm, tn), lambda i, j: (i, j)),
            pl.BlockSpec((tm, LANES), lambda i, j: (i, 0)),
        ],
        out_shape=[
            jax.ShapeDtypeStruct((T, Z_WIDTH), BF16),
            jax.ShapeDtypeStruct((T, LANES), F32),
        ],
        scratch_shapes=[pltpu.VMEM((tm, D_MODEL), BF16)],
        compiler_params=_cparams(("arbitrary", "arbitrary")),
        name="inproj",
    )(x2, g, w_main, w_fl)


def _log_sigmoid(z):
    return jnp.minimum(z, 0.0) - jnp.log1p(jnp.exp(-jnp.abs(z)))


def _forget_body(fl_ref, b_ref, c_ref, *, seq):
    lf = _log_sigmoid(fl_ref[...] + b_ref[...])
    row = lax.broadcasted_iota(jnp.int32, lf.shape, 0)
    c = lf
    k = 1
    while k < seq:
        c = c + jnp.where(row >= k, pltpu.roll(c, k, axis=0), 0.0)
        k *= 2
    c_ref[0] = c.T[:N_HEADS, :]


def _forget(fl, b_pad, *, batch, seq):
    return pl.pallas_call(
        functools.partial(_forget_body, seq=seq),
        grid=(batch,),
        in_specs=[
            pl.BlockSpec((seq, LANES), lambda b: (b, 0)),
            pl.BlockSpec((1, LANES), lambda b: (0, 0)),
        ],
        out_specs=pl.BlockSpec((1, N_HEADS, seq), lambda b: (b, 0, 0)),
        out_shape=jax.ShapeDtypeStruct((batch, N_HEADS, seq), F32),
        compiler_params=_cparams(("arbitrary",)),
        name="forget",
    )(fl, b_pad)


def _attn_body(q_ref, k_ref, v_ref, c_ref, o_ref, *, tq, scale):
    i = pl.program_id(2)
    q = q_ref[...]

    def step(j, carry, masked):
        m, l, acc = carry
        off = pl.multiple_of(j * tq, tq)
        kj = k_ref[pl.ds(off, tq), :]
        vj = v_ref[pl.ds(off, tq), :]
        cj = c_ref[0, :, pl.ds(off, tq)]
        s = lax.dot_general(q, kj, (((1,), (1,)), ((), ())), preferred_element_type=F32) * scale - cj
        if masked:
            r = lax.broadcasted_iota(jnp.int32, s.shape, 0)
            cidx = lax.broadcasted_iota(jnp.int32, s.shape, 1)
            s = jnp.where(cidx <= r, s, -jnp.inf)
        m_new = jnp.maximum(m, jnp.max(s, axis=-1, keepdims=True))
        p = jnp.exp(s - m_new)
        alpha = jnp.exp(m - m_new)
        l = alpha * l + jnp.sum(p, axis=-1, keepdims=True)
        acc = alpha * acc + jnp.dot(p.astype(BF16), vj, preferred_element_type=F32)
        return m_new, l, acc

    init = (jnp.full((tq, 1), -jnp.inf, F32), jnp.zeros((tq, 1), F32), jnp.zeros((tq, HEAD_DIM), F32))
    carry = lax.fori_loop(0, i, lambda j, c: step(j, c, False), init)
    _, l, acc = step(i, carry, True)
    o_ref[...] = (acc / l).astype(BF16)


def _attention(z, c3, *, batch, seq, tq=512):
    nq = seq // tq
    T = batch * seq
    return pl.pallas_call(
        functools.partial(_attn_body, tq=tq, scale=HEAD_DIM ** -0.5),
        grid=(batch, N_HEADS, nq),
        in_specs=[
            pl.BlockSpec((tq, HEAD_DIM), lambda b, h, i: (b * nq + i, ZC_Q + h)),
            pl.BlockSpec((seq, HEAD_DIM), lambda b, h, i: (b, ZC_K + h)),
            pl.BlockSpec((seq, HEAD_DIM), lambda b, h, i: (b, ZC_V + h)),
            pl.BlockSpec((1, 1, seq), lambda b, h, i: (b * N_HEADS + h, 0, 0)),
        ],
        out_specs=pl.BlockSpec((tq, HEAD_DIM), lambda b, h, i: (b * nq + i, h)),
        out_shape=jax.ShapeDtypeStruct((T, D_MODEL), BF16),
        compiler_params=_cparams(("arbitrary", "arbitrary", "arbitrary")),
        name="attn",
    )(z, z, z, c3)


def _lru_body(xr_ref, gr_ref, cw_ref, cb_ref, wa_ref, wx_ref, ba_ref, bx_ref, lam_ref,
              hg_ref, xbuf, hcar, *, ts):
    i = pl.program_id(1)

    @pl.when(i == 0)
    def _():
        xbuf[0:SUBLANES, :] = jnp.zeros((SUBLANES, D_MODEL), F32)
        hcar[...] = jnp.zeros_like(hcar)

    @pl.when(i > 0)
    def _():
        xbuf[0:SUBLANES, :] = xbuf[ts:ts + SUBLANES, :]

    xbuf[SUBLANES:ts + SUBLANES, :] = xr_ref[...].astype(F32)

    base = SUBLANES - (CONV_W - 1)
    xc = cw_ref[0:1, :] * xbuf[base:base + ts, :]
    for k in range(1, CONV_W):
        xc = xc + cw_ref[k:k + 1, :] * xbuf[base + k:base + k + ts, :]
    xc = xc + cb_ref[...]

    xcb = xc.astype(BF16)
    ra = jnp.concatenate(
        [jnp.dot(xcb[:, n * LRU_BLOCK_W:(n + 1) * LRU_BLOCK_W], wa_ref[n], preferred_element_type=F32)
         for n in range(LRU_BLOCKS)], axis=-1)
    rx = jnp.concatenate(
        [jnp.dot(xcb[:, n * LRU_BLOCK_W:(n + 1) * LRU_BLOCK_W], wx_ref[n], preferred_element_type=F32)
         for n in range(LRU_BLOCKS)], axis=-1)
    r = jax.nn.sigmoid(ra + ba_ref[...])
    ig = jax.nn.sigmoid(rx + bx_ref[...])
    nlam = -lam_ref[...]
    softplus = jnp.maximum(nlam, 0.0) + jnp.log1p(jnp.exp(-jnp.abs(nlam)))
    log_a = (-LRU_C * r) * softplus
    a = jnp.exp(log_a)
    th = jnp.tanh(log_a)
    mult = jnp.sqrt(-2.0 * th / (1.0 - th))
    b = mult * ig * xc

    row = lax.broadcasted_iota(jnp.int32, (ts, D_MODEL), 0)
    k = 1
    while k < ts:
        keep = row >= k
        a_sh = jnp.where(keep, pltpu.roll(a, k, axis=0), 1.0)
        b_sh = jnp.where(keep, pltpu.roll(b, k, axis=0), 0.0)
        b = a * b_sh + b
        a = a * a_sh
        k *= 2
    h = b + a * hcar[...]
    hcar[...] = h[ts - 1:ts, :]

    hg_ref[...] = (h * jax.nn.gelu(gr_ref[...].astype(F32))).astype(BF16)


def _lru(z, conv_w, conv_b, wa, wx, ba, bx, lam, *, batch, seq, ts=256):
    ns = seq // ts
    T = batch * seq
    full = lambda shape: pl.BlockSpec(shape, lambda b, i: (0,) * len(shape))
    return pl.pallas_call(
        functools.partial(_lru_body, ts=ts),
        grid=(batch, ns),
        in_specs=[
            pl.BlockSpec((ts, D_MODEL), lambda b, i: (b * ns + i, ZC_XR // SUBLANES)),
            pl.BlockSpec((ts, D_MODEL), lambda b, i: (b * ns + i, ZC_GR // SUBLANES)),
            full((CONV_W, D_MODEL)),
            full((1, D_MODEL)),
            full((LRU_BLOCKS, LRU_BLOCK_W, LRU_BLOCK_W)),
            full((LRU_BLOCKS, LRU_BLOCK_W, LRU_BLOCK_W)),
            full((1, D_MODEL)),
            full((1, D_MODEL)),
            full((1, D_MODEL)),
        ],
        out_specs=pl.BlockSpec((ts, D_MODEL), lambda b, i: (b * ns + i, 0)),
        out_shape=jax.ShapeDtypeStruct((T, D_MODEL), BF16),
        scratch_shapes=[pltpu.VMEM((ts + SUBLANES, D_MODEL), F32), pltpu.VMEM((1, D_MODEL), F32)],
        compiler_params=_cparams(("arbitrary", "arbitrary")),
        name="lru",
    )(z, z, conv_w, conv_b, wa, wx, ba, bx, lam)


def _merge_body(hg_ref, o_ref, ml_ref, ma_ref, x_ref, wl_ref, wat_ref, wo_ref, g_ref, wr_ref, br_ref,
                x1_ref, v_ref, ri_ref, cnt_ref, carry_ref, *, tm):
    @pl.when(pl.program_id(0) == 0)
    def _():
        carry_ref[...] = jnp.zeros_like(carry_ref)

    yl = jnp.dot(hg_ref[...], wl_ref[...], preferred_element_type=F32)
    ya = jnp.dot(o_ref[...], wat_ref[...], preferred_element_type=F32)
    merged = jax.nn.sigmoid(ml_ref[...].astype(F32)) * yl + jax.nn.sigmoid(ma_ref[...].astype(F32)) * ya
    x1 = x_ref[...] + jnp.dot(merged.astype(BF16), wo_ref[...], preferred_element_type=F32)
    x1_ref[...] = x1
    ms = jnp.mean(x1 * x1, axis=-1, keepdims=True)
    v = x1 * lax.rsqrt(ms + EPS) * g_ref[...]
    v_ref[...] = v

    logits = jnp.dot(v.astype(BF16), wr_ref[...], preferred_element_type=F32) + br_ref[...]
    lane = lax.broadcasted_iota(jnp.int32, logits.shape, 1)
    ninf = -jnp.inf
    big = jnp.int32(1 << 20)

    gl = jnp.where(lane < N_GROUPS, logits, ninf)
    gmax = jnp.max(gl, axis=-1, keepdims=True)
    gsel = jnp.min(jnp.where(gl == gmax, lane, big), axis=-1, keepdims=True)
    pg = 1.0 / jnp.sum(jnp.exp(gl - gmax), axis=-1, keepdims=True)

    lo = R_EXP0 + gsel * EXPERTS_PER_GROUP
    el = jnp.where(lane >= lo, jnp.where(lane < lo + EXPERTS_PER_GROUP, logits, ninf), ninf)
    v1 = jnp.max(el, axis=-1, keepdims=True)
    i1 = jnp.min(jnp.where(el == v1, lane, big), axis=-1, keepdims=True)
    el2 = jnp.where(lane == i1, ninf, el)
    v2 = jnp.max(el2, axis=-1, keepdims=True)
    i2 = jnp.min(jnp.where(el2 == v2, lane, big), axis=-1, keepdims=True)
    e21 = jnp.exp(v2 - v1)
    p1 = 1.0 / (1.0 + e21)
    w0 = pg * p1
    w1 = pg * (e21 * p1)

    hit0 = lane == i1
    hit1 = lane == i2
    onehot = jnp.where(hit0, 1.0, jnp.where(hit1, 1.0, 0.0))
    rr = lax.broadcasted_iota(jnp.int32, (tm, tm), 0)
    cc = lax.broadcasted_iota(jnp.int32, (tm, tm), 1)
    tri = jnp.where(cc < rr, 1.0, 0.0).astype(BF16)
    before = jnp.dot(tri, onehot.astype(BF16), preferred_element_type=F32) + carry_ref[...]
    rank0 = jnp.sum(jnp.where(hit0, before, 0.0), axis=-1, keepdims=True)
    rank1 = jnp.sum(jnp.where(hit1, before, 0.0), axis=-1, keepdims=True)
    carry = carry_ref[...] + jnp.sum(onehot, axis=0, keepdims=True)
    carry_ref[...] = carry
    cnt_ref[...] = carry

    e0 = (i1 - R_EXP0).astype(F32)
    e1 = (i2 - R_EXP0).astype(F32)
    ri = jnp.where(lane == 0, e0,
         jnp.where(lane == 1, e1,
         jnp.where(lane == 2, w0,
         jnp.where(lane == 3, w1,
         jnp.where(lane == 4, rank0,
         jnp.where(lane == 5, rank1, 0.0))))))
    ri_ref[...] = ri


def _merge(hg, o, z, x2, wl, wat, wo, g, wr, br, *, tm=512):
    T = x2.shape[0]
    full = lambda shape: pl.BlockSpec(shape, lambda i: (0,) * len(shape))
    return pl.pallas_call(
        functools.partial(_merge_body, tm=tm),
        grid=(T // tm,),
        in_specs=[
            pl.BlockSpec((tm, D_MODEL), lambda i: (i, 0)),
            pl.BlockSpec((tm, D_MODEL), lambda i: (i, 0)),
            pl.BlockSpec((tm, D_MODEL), lambda i: (i, ZC_ML // SUBLANES)),
            pl.BlockSpec((tm, D_MODEL), lambda i: (i, ZC_MA // SUBLANES)),
            pl.BlockSpec((tm, D_MODEL), lambda i: (i, 0)),
            full((D_MODEL, D_MODEL)),
            full((D_MODEL, D_MODEL)),
            full((D_MODEL, D_MODEL)),
            full((1, D_MODEL)),
            full((D_MODEL, LANES)),
            full((1, LANES)),
        ],
        out_specs=[
            pl.BlockSpec((tm, D_MODEL), lambda i: (i, 0)),
            pl.BlockSpec((tm, D_MODEL), lambda i: (i, 0)),
            pl.BlockSpec((tm, LANES), lambda i: (i, 0)),
            pl.BlockSpec((1, LANES), lambda i: (0, 0)),
        ],
        out_shape=[
            jax.ShapeDtypeStruct((T, D_MODEL), F32),
            jax.ShapeDtypeStruct((T, D_MODEL), F32),
            jax.ShapeDtypeStruct((T, LANES), F32),
            jax.ShapeDtypeStruct((1, LANES), F32),
        ],
        scratch_shapes=[pltpu.VMEM((1, LANES), F32)],
        compiler_params=_cparams(("arbitrary",)),
        name="merge",
    )(hg, o, z, z, x2, wl, wat, wo, g, wr, br)


def _row_slice(r):
    return pl.ds(pl.multiple_of(r * ROW_CHUNKS, ROW_CHUNKS), ROW_CHUNKS)


def _dispatch_body(dest_ref, v_ref, xs_in_ref, xs_ref, rows, sem, *, td):
    del xs_in_ref
    for s in range(ROW_CHUNKS):
        rows[pl.ds(s, td, stride=ROW_CHUNKS), :] = v_ref[:, s * LANES:(s + 1) * LANES]
    base = pl.program_id(0) * td

    def copy(r, k):
        d = dest_ref[TOP_K * (base + r) + k]
        return pltpu.make_async_copy(rows.at[_row_slice(r)], xs_ref.at[_row_slice(d)], sem)

    def issue(r, carry):
        for k in range(TOP_K):
            copy(r, k).start()
        return carry

    def drain(r, carry):
        for k in range(TOP_K):
            copy(r, k).wait()
        return carry

    lax.fori_loop(0, td, issue, 0)
    lax.fori_loop(0, td, drain, 0)


def _dispatch(dest, v, xs_init, *, td=256):
    T = v.shape[0]
    grid_spec = pltpu.PrefetchScalarGridSpec(
        num_scalar_prefetch=1,
        grid=(T // td,),
        in_specs=[
            pl.BlockSpec((td, D_MODEL), lambda i, dest: (i, 0)),
            pl.BlockSpec(memory_space=pl.ANY),
        ],
        out_specs=pl.BlockSpec(memory_space=pl.ANY),
        scratch_shapes=[pltpu.VMEM((td * ROW_CHUNKS, LANES), F32), pltpu.SemaphoreType.DMA(())],
    )
    return pl.pallas_call(
        functools.partial(_dispatch_body, td=td),
        grid_spec=grid_spec,
        out_shape=jax.ShapeDtypeStruct(xs_init.shape, F32),
        input_output_aliases={2: 0},
        compiler_params=_cparams(("arbitrary",)),
        name="dispatch",
    )(dest, v, xs_init)


def _expert_body(be_ref, nu_ref, x_ref, w13_ref, w2_ref, y_ref, *, rows):
    del be_ref
    i = pl.program_id(0)

    @pl.when(i < nu_ref[0])
    def _():
        xb = jnp.concatenate(
            [x_ref[pl.ds(s, rows, stride=ROW_CHUNKS), :].astype(BF16) for s in range(ROW_CHUNKS)], axis=-1)
        gu = jnp.dot(xb, w13_ref[0], preferred_element_type=F32)
        hb = (jax.nn.silu(gu[:, :D_EXPERT]) * gu[:, D_EXPERT:]).astype(BF16)
        y = jnp.dot(hb, w2_ref[0], preferred_element_type=F32)
        for s in range(ROW_CHUNKS):
            y_ref[pl.ds(s, rows, stride=ROW_CHUNKS), :] = y[:, s * LANES:(s + 1) * LANES]

    @pl.when(i >= nu_ref[0])
    def _():
        y_ref[...] = jnp.zeros_like(y_ref)


def _experts(blk_expert, n_used, xs, w13, w2, *, rows=EXPERT_ROWS):
    n_blocks = xs.shape[0] // (rows * ROW_CHUNKS)
    grid_spec = pltpu.PrefetchScalarGridSpec(
        num_scalar_prefetch=2,
        grid=(n_blocks,),
        in_specs=[
            pl.BlockSpec((rows * ROW_CHUNKS, LANES), lambda i, be, nu: (i, 0)),
            pl.BlockSpec((1, D_MODEL, 2 * D_EXPERT), lambda i, be, nu: (be[i], 0, 0)),
            pl.BlockSpec((1, D_EXPERT, D_MODEL), lambda i, be, nu: (be[i], 0, 0)),
        ],
        out_specs=pl.BlockSpec((rows * ROW_CHUNKS, LANES), lambda i, be, nu: (i, 0)),
    )
    return pl.pallas_call(
        functools.partial(_expert_body, rows=rows),
        grid_spec=grid_spec,
        out_shape=jax.ShapeDtypeStruct(xs.shape, F32),
        compiler_params=_cparams(("arbitrary",)),
        name="experts",
    )(blk_expert, n_used, xs, w13, w2)


def _combine_body(dest_ref, y_ref, x1_ref, ri_ref, g_ref, out_ref, buf0, buf1, sem, *, tc, final_norm):
    base = pl.program_id(0) * tc
    bufs = (buf0, buf1)

    def copy(r, k):
        d = dest_ref[TOP_K * (base + r) + k]
        return pltpu.make_async_copy(y_ref.at[_row_slice(d)], bufs[k].at[_row_slice(r)], sem)

    def issue(r, carry):
        for k in range(TOP_K):
            copy(r, k).start()
        return carry

    def drain(r, carry):
        for k in range(TOP_K):
            copy(r, k).wait()
        return carry

    lax.fori_loop(0, tc, issue, 0)
    lax.fori_loop(0, tc, drain, 0)

    def rows_of(buf):
        return jnp.concatenate([buf[pl.ds(s, tc, stride=ROW_CHUNKS), :] for s in range(ROW_CHUNKS)], axis=-1)

    ri = ri_ref[...]
    w0 = ri[:, 2:3]
    w1 = ri[:, 3:4]
    x2 = x1_ref[...] + (w0 * rows_of(buf0) + w1 * rows_of(buf1))
    if final_norm:
        ms = jnp.mean(x2 * x2, axis=-1, keepdims=True)
        x2 = x2 * lax.rsqrt(ms + EPS) * g_ref[...]
    out_ref[...] = x2


def _combine(dest, y, x1, ri, g, *, final_norm, tc=256):
    T = x1.shape[0]
    grid_spec = pltpu.PrefetchScalarGridSpec(
        num_scalar_prefetch=1,
        grid=(T // tc,),
        in_specs=[
            pl.BlockSpec(memory_space=pl.ANY),
            pl.BlockSpec((tc, D_MODEL), lambda i, dest: (i, 0)),
            pl.BlockSpec((tc, LANES), lambda i, dest: (i, 0)),
            pl.BlockSpec((1, D_MODEL), lambda i, dest: (0, 0)),
        ],
        out_specs=pl.BlockSpec((tc, D_MODEL), lambda i, dest: (i, 0)),
        scratch_shapes=[pltpu.VMEM((tc * ROW_CHUNKS, LANES), F32),
                        pltpu.VMEM((tc * ROW_CHUNKS, LANES), F32),
                        pltpu.SemaphoreType.DMA(())],
    )
    return pl.pallas_call(
        functools.partial(_combine_body, tc=tc, final_norm=final_norm),
        grid_spec=grid_spec,
        out_shape=jax.ShapeDtypeStruct((T, D_MODEL), F32),
        compiler_params=_cparams(("arbitrary",)),
        name="combine",
    )(dest, y, x1, ri, g)


def _pad_lanes(a, width=LANES):
    return jnp.pad(a, ((0, 0), (0, width - a.shape[-1])))


def _layer(x2, batch, seq, g_mix, w_in, conv_w, conv_b, w_rg_a, b_rg_a, w_rg_x, b_rg_x, lam, b_forget,
           w_lru_out, w_attn_out, w_out, g_ffn, w_rgrp, b_rgrp, w_rexp, b_rexp, w1, w3, w2):
    T = batch * seq
    row = lambda a: a.reshape(1, -1).astype(F32)

    fl0 = 5 * D_MODEL
    w_main = jnp.concatenate([w_in[:, :fl0], w_in[:, fl0 + N_HEADS:]], axis=1).astype(BF16)
    w_fl = _pad_lanes(w_in[:, fl0:fl0 + N_HEADS]).astype(BF16)
    z, fl = _inproj(x2, row(g_mix), w_main, w_fl)

    c = _forget(fl, _pad_lanes(row(b_forget)), batch=batch, seq=seq)
    o = _attention(z, c.reshape(batch * N_HEADS, 1, seq), batch=batch, seq=seq)

    hg = _lru(z, conv_w.astype(F32), row(conv_b), w_rg_a.astype(BF16), w_rg_x.astype(BF16),
              row(b_rg_a), row(b_rg_x), row(lam), batch=batch, seq=seq)

    wr = _pad_lanes(jnp.concatenate([w_rgrp, w_rexp], axis=1)).astype(BF16)
    br = _pad_lanes(jnp.concatenate([row(b_rgrp), row(b_rexp)], axis=1))
    x1, v, ri, cnt = _merge(hg, o, z, x2, w_lru_out.astype(BF16), w_attn_out.astype(BF16),
                            w_out.astype(BF16), row(g_ffn), wr, br)

    e = ri[:, 0:TOP_K].astype(jnp.int32)
    rank = ri[:, 4:4 + TOP_K].astype(jnp.int32)
    counts = cnt[0, R_EXP0:R_EXP0 + N_EXPERTS].astype(jnp.int32)
    padded = (counts + EXPERT_ROWS - 1) // EXPERT_ROWS * EXPERT_ROWS
    pad_end = jnp.cumsum(padded)
    pad_start = pad_end - padded
    dest = (pad_start[e] + rank).reshape(T * TOP_K).astype(jnp.int32)
    n_blocks = (T * TOP_K + N_EXPERTS * (EXPERT_ROWS - 1) + EXPERT_ROWS - 1) // EXPERT_ROWS
    blk_expert = jnp.minimum(
        jnp.searchsorted(pad_end, jnp.arange(n_blocks, dtype=jnp.int32) * EXPERT_ROWS, side='right'),
        N_EXPERTS - 1).astype(jnp.int32)
    n_used = (pad_end[-1:] // EXPERT_ROWS).astype(jnp.int32)

    xs = _dispatch(dest, v, jnp.zeros((n_blocks * EXPERT_ROWS * ROW_CHUNKS, LANES), F32))
    w13 = jnp.concatenate([w1, w3], axis=-1).astype(BF16)
    y = _experts(blk_expert, n_used, xs, w13, w2.astype(BF16))
    return x1, y, dest, ri


def kernel(x, g_mix, w_in, conv_w, conv_b, w_rg_a, b_rg_a, w_rg_x, b_rg_x, lru_lambda, b_forget, w_lru_out, w_attn_out, w_out, g_ffn, w_route_group, b_route_group, w_route_expert, b_route_expert, w_exp_gate, w_exp_up, w_exp_down, g_final):
    batch, seq, _ = x.shape
    depth = g_mix.shape[0]
    x2 = x.reshape(batch * seq, D_MODEL)
    for l in range(depth):
        x1, y, dest, ri = _layer(
            x2, batch, seq, g_mix[l], w_in[l], conv_w[l], conv_b[l], w_rg_a[l], b_rg_a[l], w_rg_x[l],
            b_rg_x[l], lru_lambda[l], b_forget[l], w_lru_out[l], w_attn_out[l], w_out[l], g_ffn[l],
            w_route_group[l], b_route_group[l], w_route_expert[l], b_route_expert[l],
            w_exp_gate[l], w_exp_up[l], w_exp_down[l])
        x2 = _combine(dest, y, x1, ri, g_final.reshape(1, -1).astype(F32), final_norm=l == depth - 1)
    return x2.reshape(batch, seq, D_MODEL)
```

```python
import functools

import jax
import jax.numpy as jnp
from jax import lax
from jax.experimental import pallas as pl
from jax.experimental.pallas import tpu as pltpu

F32 = jnp.float32
BF16 = jnp.bfloat16

D_MODEL = 1024
LRU_BLOCK_W = 256
LRU_BLOCKS = D_MODEL // LRU_BLOCK_W
CONV_W = 4
LRU_C = 8.0
N_HEADS = 8
HEAD_DIM = D_MODEL // N_HEADS
N_GROUPS = 4
EXPERTS_PER_GROUP = 8
N_EXPERTS = N_GROUPS * EXPERTS_PER_GROUP
TOP_K = 2
D_EXPERT = D_MODEL // 2
EPS = 1e-6

LANES = 128
SUBLANES = 8
ROW_CHUNKS = D_MODEL // LANES
VMEM_LIMIT = 48 * 1024 * 1024

ZC_XR, ZC_GR, ZC_Q, ZC_K, ZC_V, ZC_ML, ZC_MA = 0, 8, 16, 24, 32, 40, 48
Z_WIDTH = 7 * D_MODEL

R_EXP0 = N_GROUPS

EXPERT_ROWS = 256


def _cparams(sem):
    return pltpu.CompilerParams(dimension_semantics=sem, vmem_limit_bytes=VMEM_LIMIT)


LOG2E = 1.4426950408889634
Q_PRESCALE = HEAD_DIM ** -0.5 * LOG2E


def _inproj_body(x_ref, g_ref, w_ref, wfl_ref, z_ref, fl_ref, u_ref, *, q_block):
    j = pl.program_id(1)

    @pl.when(j == 0)
    def _():
        x = x_ref[...]
        ms = jnp.mean(x * x, axis=-1, keepdims=True)
        u = (x * lax.rsqrt(ms + EPS) * g_ref[...]).astype(BF16)
        u_ref[...] = u
        fl_ref[...] = jnp.dot(u, wfl_ref[...], preferred_element_type=F32)

    acc = jnp.dot(u_ref[...], w_ref[...], preferred_element_type=F32)

    @pl.when(j == q_block)
    def _():
        z_ref[...] = (acc * Q_PRESCALE).astype(BF16)

    @pl.when(j != q_block)
    def _():
        z_ref[...] = acc.astype(BF16)


def _inproj(x2, g, w_main, w_fl, *, tm=1024, tn=D_MODEL):
    T = x2.shape[0]
    return pl.pallas_call(
        functools.partial(_inproj_body, q_block=ZC_Q * LANES // tn),
        grid=(T // tm, Z_WIDTH // tn),
        in_specs=[
            pl.BlockSpec((tm, D_MODEL), lambda i, j: (i, 0)),
            pl.BlockSpec((1, D_MODEL), lambda i, j: (0, 0)),
            pl.BlockSpec((D_MODEL, tn), lambda i, j: (0, j)),
            pl.BlockSpec((D_MODEL, LANES), lambda i, j: (0, 0)),
        ],
        out_specs=[
            pl.BlockSpec((tm, tn), lambda i, j: (i, j)),
            pl.BlockSpec((tm, LANES), lambda i, j: (i, 0)),
        ],
        out_shape=[
            jax.ShapeDtypeStruct((T, Z_WIDTH), BF16),
            jax.ShapeDtypeStruct((T, LANES), F32),
        ],
        scratch_shapes=[pltpu.VMEM((tm, D_MODEL), BF16)],
        compiler_params=_cparams(("arbitrary", "arbitrary")),
        name="inproj",
    )(x2, g, w_main, w_fl)


C_TERMS = 3


def _log_sigmoid(z):
    return jnp.minimum(z, 0.0) - jnp.log1p(jnp.exp(-jnp.abs(z)))


def _forget_body(fl_ref, b_ref, c_ref, *, seq):
    lf = _log_sigmoid(fl_ref[...] + b_ref[...])
    row = lax.broadcasted_iota(jnp.int32, lf.shape, 0)
    lane = lax.broadcasted_iota(jnp.int32, lf.shape, 1)
    c = lf
    k = 1
    while k < seq:
        c = c + jnp.where(row >= k, pltpu.roll(c, k, axis=0), 0.0)
        k *= 2
    rem = c * LOG2E
    terms = []
    for _ in range(C_TERMS):
        t = rem.astype(BF16).astype(F32)
        terms.append(t)
        rem = rem - t
    for h in range(N_HEADS):
        out = jnp.zeros(lf.shape, F32)
        for n, t in enumerate(terms):
            out = jnp.where(lane == n, t[:, h:h + 1], out)
        c_ref[0, h] = out.astype(BF16)


def _forget(fl, b_pad, *, batch, seq):
    return pl.pallas_call(
        functools.partial(_forget_body, seq=seq),
        grid=(batch,),
        in_specs=[
            pl.BlockSpec((seq, LANES), lambda b: (b, 0)),
            pl.BlockSpec((1, LANES), lambda b: (0, 0)),
        ],
        out_specs=pl.BlockSpec((1, N_HEADS, seq, LANES), lambda b: (b, 0, 0, 0)),
        out_shape=jax.ShapeDtypeStruct((batch, N_HEADS, seq, LANES), BF16),
        compiler_params=_cparams(("arbitrary",)),
        name="forget",
    )(fl, b_pad)


ATTN_HEADS_PER_STEP = 2


def _attn_body(q_ref, k_ref, v_ref, kc_ref, o_ref, *, tq):
    i = pl.program_id(2)
    lane = lax.broadcasted_iota(jnp.int32, (tq, LANES), 1)
    qc = jnp.where(lane < C_TERMS, -1.0, 0.0).astype(BF16)
    heads = range(ATTN_HEADS_PER_STEP)
    cols = [slice(g * HEAD_DIM, (g + 1) * HEAD_DIM) for g in heads]
    qs = [jnp.concatenate([q_ref[:, cols[g]], qc], axis=1) for g in heads]

    def step(j, carry, masked):
        off = pl.multiple_of(j * tq, tq)
        out = []
        for g in heads:
            m, l, acc = carry[g]
            kj = jnp.concatenate([k_ref[pl.ds(off, tq), cols[g]], kc_ref[0, g, pl.ds(off, tq), :]], axis=1)
            s = lax.dot_general(qs[g], kj, (((1,), (1,)), ((), ())), preferred_element_type=F32)
            if masked:
                r = lax.broadcasted_iota(jnp.int32, s.shape, 0)
                cidx = lax.broadcasted_iota(jnp.int32, s.shape, 1)
                s = jnp.where(cidx <= r, s, -jnp.inf)
            m_new = jnp.maximum(m, jnp.max(s, axis=-1, keepdims=True))
            p = jnp.exp2(s - m_new)
            alpha = jnp.exp2(m - m_new)
            l = alpha * l + jnp.sum(p, axis=-1, keepdims=True)
            acc = alpha * acc + jnp.dot(p.astype(BF16), v_ref[pl.ds(off, tq), cols[g]],
                                        preferred_element_type=F32)
            out.append((m_new, l, acc))
        return tuple(out)

    init = tuple((jnp.full((tq, 1), -jnp.inf, F32), jnp.zeros((tq, 1), F32), jnp.zeros((tq, HEAD_DIM), F32))
                 for _ in heads)
    carry = lax.fori_loop(0, i, lambda j, c: step(j, c, False), init)
    carry = step(i, carry, True)
    for g in heads:
        _, l, acc = carry[g]
        o_ref[:, cols[g]] = (acc / l).astype(BF16)


def _attention(z, caug, *, batch, seq, tq=512):
    nq = seq // tq
    T = batch * seq
    G = ATTN_HEADS_PER_STEP
    W = G * HEAD_DIM
    return pl.pallas_call(
        functools.partial(_attn_body, tq=tq),
        grid=(batch, N_HEADS // G, nq),
        in_specs=[
            pl.BlockSpec((tq, W), lambda b, h, i: (b * nq + i, ZC_Q // G + h)),
            pl.BlockSpec((seq, W), lambda b, h, i: (b, ZC_K // G + h)),
            pl.BlockSpec((seq, W), lambda b, h, i: (b, ZC_V // G + h)),
            pl.BlockSpec((1, G, seq, LANES), lambda b, h, i: (b, h, 0, 0)),
        ],
        out_specs=pl.BlockSpec((tq, W), lambda b, h, i: (b * nq + i, h)),
        out_shape=jax.ShapeDtypeStruct((T, D_MODEL), BF16),
        compiler_params=_cparams(("arbitrary", "arbitrary", "arbitrary")),
        name="attn",
    )(z, z, z, caug)


def _lru_body(xr_ref, gr_ref, cw_ref, cb_ref, wa_ref, wx_ref, ba_ref, bx_ref, lam_ref,
              hg_ref, xbuf, hcar, *, ts):
    i = pl.program_id(1)

    @pl.when(i == 0)
    def _():
        xbuf[0:SUBLANES, :] = jnp.zeros((SUBLANES, D_MODEL), F32)
        hcar[...] = jnp.zeros_like(hcar)

    @pl.when(i > 0)
    def _():
        xbuf[0:SUBLANES, :] = xbuf[ts:ts + SUBLANES, :]

    xbuf[SUBLANES:ts + SUBLANES, :] = xr_ref[...].astype(F32)

    base = SUBLANES - (CONV_W - 1)
    xc = cw_ref[0:1, :] * xbuf[base:base + ts, :]
    for k in range(1, CONV_W):
        xc = xc + cw_ref[k:k + 1, :] * xbuf[base + k:base + k + ts, :]
    xc = xc + cb_ref[...]

    xcb = xc.astype(BF16)
    ra = jnp.concatenate(
        [jnp.dot(xcb[:, n * LRU_BLOCK_W:(n + 1) * LRU_BLOCK_W], wa_ref[n], preferred_element_type=F32)
         for n in range(LRU_BLOCKS)], axis=-1)
    rx = jnp.concatenate(
        [jnp.dot(xcb[:, n * LRU_BLOCK_W:(n + 1) * LRU_BLOCK_W], wx_ref[n], preferred_element_type=F32)
         for n in range(LRU_BLOCKS)], axis=-1)
    r = jax.nn.sigmoid(ra + ba_ref[...])
    ig = jax.nn.sigmoid(rx + bx_ref[...])
    nlam = -lam_ref[...]
    softplus = jnp.maximum(nlam, 0.0) + jnp.log1p(jnp.exp(-jnp.abs(nlam)))
    log_a = (-LRU_C * r) * softplus
    a = jnp.exp(log_a)
    th = jnp.tanh(log_a)
    mult = jnp.sqrt(-2.0 * th / (1.0 - th))
    b = mult * ig * xc

    row = lax.broadcasted_iota(jnp.int32, (ts, D_MODEL), 0)
    k = 1
    while k < ts:
        keep = row >= k
        a_sh = jnp.where(keep, pltpu.roll(a, k, axis=0), 1.0)
        b_sh = jnp.where(keep, pltpu.roll(b, k, axis=0), 0.0)
        b = a * b_sh + b
        a = a * a_sh
        k *= 2
    h = b + a * hcar[...]
    hcar[...] = h[ts - 1:ts, :]

    hg_ref[...] = (h * jax.nn.gelu(gr_ref[...].astype(F32))).astype(BF16)


def _lru(z, conv_w, conv_b, wa, wx, ba, bx, lam, *, batch, seq, ts=256):
    ns = seq // ts
    T = batch * seq
    full = lambda shape: pl.BlockSpec(shape, lambda b, i: (0,) * len(shape))
    return pl.pallas_call(
        functools.partial(_lru_body, ts=ts),
        grid=(batch, ns),
        in_specs=[
            pl.BlockSpec((ts, D_MODEL), lambda b, i: (b * ns + i, ZC_XR // SUBLANES)),
            pl.BlockSpec((ts, D_MODEL), lambda b, i: (b * ns + i, ZC_GR // SUBLANES)),
            full((CONV_W, D_MODEL)),
            full((1, D_MODEL)),
            full((LRU_BLOCKS, LRU_BLOCK_W, LRU_BLOCK_W)),
            full((LRU_BLOCKS, LRU_BLOCK_W, LRU_BLOCK_W)),
            full((1, D_MODEL)),
            full((1, D_MODEL)),
            full((1, D_MODEL)),
        ],
        out_specs=pl.BlockSpec((ts, D_MODEL), lambda b, i: (b * ns + i, 0)),
        out_shape=jax.ShapeDtypeStruct((T, D_MODEL), BF16),
        scratch_shapes=[pltpu.VMEM((ts + SUBLANES, D_MODEL), F32), pltpu.VMEM((1, D_MODEL), F32)],
        compiler_params=_cparams(("arbitrary", "arbitrary")),
        name="lru",
    )(z, z, conv_w, conv_b, wa, wx, ba, bx, lam)


def _merge_body(hg_ref, o_ref, ml_ref, ma_ref, x_ref, wl_ref, wat_ref, wo_ref, g_ref, wr_ref, br_ref,
                x1_ref, v_ref, ri_ref, cnt_ref, carry_ref, *, tm):
    @pl.when(pl.program_id(0) == 0)
    def _():
        carry_ref[...] = jnp.zeros_like(carry_ref)

    yl = jnp.dot(hg_ref[...], wl_ref[...], preferred_element_type=F32)
    ya = jnp.dot(o_ref[...], wat_ref[...], preferred_element_type=F32)
    merged = jax.nn.sigmoid(ml_ref[...].astype(F32)) * yl + jax.nn.sigmoid(ma_ref[...].astype(F32)) * ya
    x1 = x_ref[...] + jnp.dot(merged.astype(BF16), wo_ref[...], preferred_element_type=F32)
    x1_ref[...] = x1
    ms = jnp.mean(x1 * x1, axis=-1, keepdims=True)
    v = x1 * lax.rsqrt(ms + EPS) * g_ref[...]
    v_ref[...] = v

    logits = jnp.dot(v.astype(BF16), wr_ref[...], preferred_element_type=F32) + br_ref[...]
    lane = lax.broadcasted_iota(jnp.int32, logits.shape, 1)
    ninf = -jnp.inf
    big = jnp.int32(1 << 20)

    gl = jnp.where(lane < N_GROUPS, logits, ninf)
    gmax = jnp.max(gl, axis=-1, keepdims=True)
    gsel = jnp.min(jnp.where(gl == gmax, lane, big), axis=-1, keepdims=True)
    pg = 1.0 / jnp.sum(jnp.exp(gl - gmax), axis=-1, keepdims=True)

    lo = R_EXP0 + gsel * EXPERTS_PER_GROUP
    el = jnp.where(lane >= lo, jnp.where(lane < lo + EXPERTS_PER_GROUP, logits, ninf), ninf)
    v1 = jnp.max(el, axis=-1, keepdims=True)
    i1 = jnp.min(jnp.where(el == v1, lane, big), axis=-1, keepdims=True)
    el2 = jnp.where(lane == i1, ninf, el)
    v2 = jnp.max(el2, axis=-1, keepdims=True)
    i2 = jnp.min(jnp.where(el2 == v2, lane, big), axis=-1, keepdims=True)
    e21 = jnp.exp(v2 - v1)
    p1 = 1.0 / (1.0 + e21)
    w0 = pg * p1
    w1 = pg * (e21 * p1)

    hit0 = lane == i1
    hit1 = lane == i2
    onehot = jnp.where(hit0, 1.0, jnp.where(hit1, 1.0, 0.0))
    rr = lax.broadcasted_iota(jnp.int32, (tm, tm), 0)
    cc = lax.broadcasted_iota(jnp.int32, (tm, tm), 1)
    tri = jnp.where(cc < rr, 1.0, 0.0).astype(BF16)
    before = jnp.dot(tri, onehot.astype(BF16), preferred_element_type=F32) + carry_ref[...]
    rank0 = jnp.sum(jnp.where(hit0, before, 0.0), axis=-1, keepdims=True)
    rank1 = jnp.sum(jnp.where(hit1, before, 0.0), axis=-1, keepdims=True)
    carry = carry_ref[...] + jnp.sum(onehot, axis=0, keepdims=True)
    carry_ref[...] = carry
    cnt_ref[...] = carry

    e0 = (i1 - R_EXP0).astype(F32)
    e1 = (i2 - R_EXP0).astype(F32)
    ri = jnp.where(lane == 0, e0,
         jnp.where(lane == 1, e1,
         jnp.where(lane == 2, w0,
         jnp.where(lane == 3, w1,
         jnp.where(lane == 4, rank0,
         jnp.where(lane == 5, rank1, 0.0))))))
    ri_ref[...] = ri


def _merge(hg, o, z, x2, wl, wat, wo, g, wr, br, *, tm=512):
    T = x2.shape[0]
    full = lambda shape: pl.BlockSpec(shape, lambda i: (0,) * len(shape))
    return pl.pallas_call(
        functools.partial(_merge_body, tm=tm),
        grid=(T // tm,),
        in_specs=[
            pl.BlockSpec((tm, D_MODEL), lambda i: (i, 0)),
            pl.BlockSpec((tm, D_MODEL), lambda i: (i, 0)),
            pl.BlockSpec((tm, D_MODEL), lambda i: (i, ZC_ML // SUBLANES)),
            pl.BlockSpec((tm, D_MODEL), lambda i: (i, ZC_MA // SUBLANES)),
            pl.BlockSpec((tm, D_MODEL), lambda i: (i, 0)),
            full((D_MODEL, D_MODEL)),
            full((D_MODEL, D_MODEL)),
            full((D_MODEL, D_MODEL)),
            full((1, D_MODEL)),
            full((D_MODEL, LANES)),
            full((1, LANES)),
        ],
        out_specs=[
            pl.BlockSpec((tm, D_MODEL), lambda i: (i, 0)),
            pl.BlockSpec((tm, D_MODEL), lambda i: (i, 0)),
            pl.BlockSpec((tm, LANES), lambda i: (i, 0)),
            pl.BlockSpec((1, LANES), lambda i: (0, 0)),
        ],
        out_shape=[
            jax.ShapeDtypeStruct((T, D_MODEL), F32),
            jax.ShapeDtypeStruct((T, D_MODEL), F32),
            jax.ShapeDtypeStruct((T, LANES), F32),
            jax.ShapeDtypeStruct((1, LANES), F32),
        ],
        scratch_shapes=[pltpu.VMEM((1, LANES), F32)],
        compiler_params=_cparams(("arbitrary",)),
        name="merge",
    )(hg, o, z, z, x2, wl, wat, wo, g, wr, br)


def _row_slice(r):
    return pl.ds(pl.multiple_of(r * ROW_CHUNKS, ROW_CHUNKS), ROW_CHUNKS)


def _dispatch_body(dest_ref, v_ref, xs_in_ref, xs_ref, rows, sem, *, td):
    del xs_in_ref
    for s in range(ROW_CHUNKS):
        rows[pl.ds(s, td, stride=ROW_CHUNKS), :] = v_ref[:, s * LANES:(s + 1) * LANES]
    base = pl.program_id(0) * td

    def copy(r, k):
        d = dest_ref[TOP_K * (base + r) + k]
        return pltpu.make_async_copy(rows.at[_row_slice(r)], xs_ref.at[_row_slice(d)], sem)

    def issue(r, carry):
        for k in range(TOP_K):
            copy(r, k).start()
        return carry

    def drain(r, carry):
        for k in range(TOP_K):
            copy(r, k).wait()
        return carry

    lax.fori_loop(0, td, issue, 0)
    lax.fori_loop(0, td, drain, 0)


def _dispatch(dest, v, xs_init, *, td=256):
    T = v.shape[0]
    grid_spec = pltpu.PrefetchScalarGridSpec(
        num_scalar_prefetch=1,
        grid=(T // td,),
        in_specs=[
            pl.BlockSpec((td, D_MODEL), lambda i, dest: (i, 0)),
            pl.BlockSpec(memory_space=pl.ANY),
        ],
        out_specs=pl.BlockSpec(memory_space=pl.ANY),
        scratch_shapes=[pltpu.VMEM((td * ROW_CHUNKS, LANES), F32), pltpu.SemaphoreType.DMA(())],
    )
    return pl.pallas_call(
        functools.partial(_dispatch_body, td=td),
        grid_spec=grid_spec,
        out_shape=jax.ShapeDtypeStruct(xs_init.shape, F32),
        input_output_aliases={2: 0},
        compiler_params=_cparams(("arbitrary",)),
        name="dispatch",
    )(dest, v, xs_init)


def _expert_body(be_ref, nu_ref, x_ref, w13_ref, w2_ref, y_ref, *, rows):
    del be_ref
    i = pl.program_id(0)

    @pl.when(i < nu_ref[0])
    def _():
        xb = jnp.concatenate(
            [x_ref[pl.ds(s, rows, stride=ROW_CHUNKS), :].astype(BF16) for s in range(ROW_CHUNKS)], axis=-1)
        gu = jnp.dot(xb, w13_ref[0], preferred_element_type=F32)
        hb = (jax.nn.silu(gu[:, :D_EXPERT]) * gu[:, D_EXPERT:]).astype(BF16)
        y = jnp.dot(hb, w2_ref[0], preferred_element_type=F32)
        for s in range(ROW_CHUNKS):
            y_ref[pl.ds(s, rows, stride=ROW_CHUNKS), :] = y[:, s * LANES:(s + 1) * LANES]

    @pl.when(i >= nu_ref[0])
    def _():
        y_ref[...] = jnp.zeros_like(y_ref)


def _experts(blk_expert, n_used, xs, w13, w2, *, rows=EXPERT_ROWS):
    n_blocks = xs.shape[0] // (rows * ROW_CHUNKS)
    grid_spec = pltpu.PrefetchScalarGridSpec(
        num_scalar_prefetch=2,
        grid=(n_blocks,),
        in_specs=[
            pl.BlockSpec((rows * ROW_CHUNKS, LANES), lambda i, be, nu: (i, 0)),
            pl.BlockSpec((1, D_MODEL, 2 * D_EXPERT), lambda i, be, nu: (be[i], 0, 0)),
            pl.BlockSpec((1, D_EXPERT, D_MODEL), lambda i, be, nu: (be[i], 0, 0)),
        ],
        out_specs=pl.BlockSpec((rows * ROW_CHUNKS, LANES), lambda i, be, nu: (i, 0)),
    )
    return pl.pallas_call(
        functools.partial(_expert_body, rows=rows),
        grid_spec=grid_spec,
        out_shape=jax.ShapeDtypeStruct(xs.shape, F32),
        compiler_params=_cparams(("arbitrary",)),
        name="experts",
    )(blk_expert, n_used, xs, w13, w2)


def _combine_body(dest_ref, y_ref, x1_ref, ri_ref, g_ref, out_ref, buf0, buf1, sem, *, tc, final_norm):
    base = pl.program_id(0) * tc
    bufs = (buf0, buf1)

    def copy(r, k):
        d = dest_ref[TOP_K * (base + r) + k]
        return pltpu.make_async_copy(y_ref.at[_row_slice(d)], bufs[k].at[_row_slice(r)], sem)

    def issue(r, carry):
        for k in range(TOP_K):
            copy(r, k).start()
        return carry

    def drain(r, carry):
        for k in range(TOP_K):
            copy(r, k).wait()
        return carry

    lax.fori_loop(0, tc, issue, 0)
    lax.fori_loop(0, tc, drain, 0)

    def rows_of(buf):
        return jnp.concatenate([buf[pl.ds(s, tc, stride=ROW_CHUNKS), :] for s in range(ROW_CHUNKS)], axis=-1)

    ri = ri_ref[...]
    w0 = ri[:, 2:3]
    w1 = ri[:, 3:4]
    x2 = x1_ref[...] + (w0 * rows_of(buf0) + w1 * rows_of(buf1))
    if final_norm:
        ms = jnp.mean(x2 * x2, axis=-1, keepdims=True)
        x2 = x2 * lax.rsqrt(ms + EPS) * g_ref[...]
    out_ref[...] = x2


def _combine(dest, y, x1, ri, g, *, final_norm, tc=256):
    T = x1.shape[0]
    grid_spec = pltpu.PrefetchScalarGridSpec(
        num_scalar_prefetch=1,
        grid=(T // tc,),
        in_specs=[
            pl.BlockSpec(memory_space=pl.ANY),
            pl.BlockSpec((tc, D_MODEL), lambda i, dest: (i, 0)),
            pl.BlockSpec((tc, LANES), lambda i, dest: (i, 0)),
            pl.BlockSpec((1, D_MODEL), lambda i, dest: (0, 0)),
        ],
        out_specs=pl.BlockSpec((tc, D_MODEL), lambda i, dest: (i, 0)),
        scratch_shapes=[pltpu.VMEM((tc * ROW_CHUNKS, LANES), F32),
                        pltpu.VMEM((tc * ROW_CHUNKS, LANES), F32),
                        pltpu.SemaphoreType.DMA(())],
    )
    return pl.pallas_call(
        functools.partial(_combine_body, tc=tc, final_norm=final_norm),
        grid_spec=grid_spec,
        out_shape=jax.ShapeDtypeStruct((T, D_MODEL), F32),
        compiler_params=_cparams(("arbitrary",)),
        name="combine",
    )(dest, y, x1, ri, g)


def _pad_lanes(a, width=LANES):
    return jnp.pad(a, ((0, 0), (0, width - a.shape[-1])))


def _layer(x2, batch, seq, g_mix, w_in, conv_w, conv_b, w_rg_a, b_rg_a, w_rg_x, b_rg_x, lam, b_forget,
           w_lru_out, w_attn_out, w_out, g_ffn, w_rgrp, b_rgrp, w_rexp, b_rexp, w1, w3, w2):
    T = batch * seq
    row = lambda a: a.reshape(1, -1).astype(F32)

    fl0 = 5 * D_MODEL
    w_main = jnp.concatenate([w_in[:, :fl0], w_in[:, fl0 + N_HEADS:]], axis=1).astype(BF16)
    w_fl = _pad_lanes(w_in[:, fl0:fl0 + N_HEADS]).astype(BF16)
    z, fl = _inproj(x2, row(g_mix), w_main, w_fl)

    caug = _forget(fl, _pad_lanes(row(b_forget)), batch=batch, seq=seq)
    o = _attention(z, caug, batch=batch, seq=seq)

    hg = _lru(z, conv_w.astype(F32), row(conv_b), w_rg_a.astype(BF16), w_rg_x.astype(BF16),
              row(b_rg_a), row(b_rg_x), row(lam), batch=batch, seq=seq)

    wr = _pad_lanes(jnp.concatenate([w_rgrp, w_rexp], axis=1)).astype(BF16)
    br = _pad_lanes(jnp.concatenate([row(b_rgrp), row(b_rexp)], axis=1))
    x1, v, ri, cnt = _merge(hg, o, z, x2, w_lru_out.astype(BF16), w_attn_out.astype(BF16),
                            w_out.astype(BF16), row(g_ffn), wr, br)

    e = ri[:, 0:TOP_K].astype(jnp.int32)
    rank = ri[:, 4:4 + TOP_K].astype(jnp.int32)
    counts = cnt[0, R_EXP0:R_EXP0 + N_EXPERTS].astype(jnp.int32)
    padded = (counts + EXPERT_ROWS - 1) // EXPERT_ROWS * EXPERT_ROWS
    pad_end = jnp.cumsum(padded)
    pad_start = pad_end - padded
    dest = (pad_start[e] + rank).reshape(T * TOP_K).astype(jnp.int32)
    n_blocks = (T * TOP_K + N_EXPERTS * (EXPERT_ROWS - 1) + EXPERT_ROWS - 1) // EXPERT_ROWS
    blk_start = jnp.arange(n_blocks, dtype=jnp.int32) * EXPERT_ROWS
    blk_expert = jnp.minimum(
        jnp.sum((pad_end[None, :] <= blk_start[:, None]).astype(jnp.int32), axis=1), N_EXPERTS - 1)
    n_used = (pad_end[-1:] // EXPERT_ROWS).astype(jnp.int32)

    xs = _dispatch(dest, v, jnp.zeros((n_blocks * EXPERT_ROWS * ROW_CHUNKS, LANES), F32))
    w13 = jnp.concatenate([w1, w3], axis=-1).astype(BF16)
    y = _experts(blk_expert, n_used, xs, w13, w2.astype(BF16))
    return x1, y, dest, ri


def kernel(x, g_mix, w_in, conv_w, conv_b, w_rg_a, b_rg_a, w_rg_x, b_rg_x, lru_lambda, b_forget, w_lru_out, w_attn_out, w_out, g_ffn, w_route_group, b_route_group, w_route_expert, b_route_expert, w_exp_gate, w_exp_up, w_exp_down, g_final):
    batch, seq, _ = x.shape
    depth = g_mix.shape[0]
    x2 = x.reshape(batch * seq, D_MODEL)
    for l in range(depth):
        x1, y, dest, ri = _layer(
            x2, batch, seq, g_mix[l], w_in[l], conv_w[l], conv_b[l], w_rg_a[l], b_rg_a[l], w_rg_x[l],
            b_rg_x[l], lru_lambda[l], b_forget[l], w_lru_out[l], w_attn_out[l], w_out[l], g_ffn[l],
            w_route_group[l], b_route_group[l], w_route_expert[l], b_route_expert[l],
            w_exp_gate[l], w_exp_up[l], w_exp_down[l])
        x2 = _combine(dest, y, x1, ri, g_final.reshape(1, -1).astype(F32), final_norm=l == depth - 1)
    return x2.reshape(batch, seq, D_MODEL)
```

```python
import functools

import jax
import jax.numpy as jnp
from jax import lax
from jax.experimental import pallas as pl
from jax.experimental.pallas import tpu as pltpu

F32 = jnp.float32
BF16 = jnp.bfloat16

D_MODEL = 1024
LRU_BLOCK_W = 256
LRU_BLOCKS = D_MODEL // LRU_BLOCK_W
CONV_W = 4
LRU_C = 8.0
N_HEADS = 8
HEAD_DIM = D_MODEL // N_HEADS
N_GROUPS = 4
EXPERTS_PER_GROUP = 8
N_EXPERTS = N_GROUPS * EXPERTS_PER_GROUP
TOP_K = 2
D_EXPERT = D_MODEL // 2
EPS = 1e-6

LANES = 128
SUBLANES = 8
ROW_CHUNKS = D_MODEL // LANES
VMEM_LIMIT = 48 * 1024 * 1024

ZC_XR, ZC_GR, ZC_Q, ZC_K, ZC_V, ZC_ML, ZC_MA = 0, 8, 16, 24, 32, 40, 48
Z_WIDTH = 7 * D_MODEL

R_EXP0 = N_GROUPS

EXPERT_ROWS = 256


def _cparams(sem):
    return pltpu.CompilerParams(dimension_semantics=sem, vmem_limit_bytes=VMEM_LIMIT)


LOG2E = 1.4426950408889634
Q_PRESCALE = HEAD_DIM ** -0.5 * LOG2E


def _inproj_body(x_ref, g_ref, w_ref, wfl_ref, z_ref, fl_ref, u_ref, *, q_block):
    j = pl.program_id(1)

    @pl.when(j == 0)
    def _():
        x = x_ref[...]
        ms = jnp.mean(x * x, axis=-1, keepdims=True)
        u = (x * lax.rsqrt(ms + EPS) * g_ref[...]).astype(BF16)
        u_ref[...] = u
        fl_ref[...] = jnp.dot(u, wfl_ref[...], preferred_element_type=F32)

    acc = jnp.dot(u_ref[...], w_ref[...], preferred_element_type=F32)

    @pl.when(j == q_block)
    def _():
        z_ref[...] = (acc * Q_PRESCALE).astype(BF16)

    @pl.when(j != q_block)
    def _():
        z_ref[...] = acc.astype(BF16)


def _inproj(x2, g, w_main, w_fl, *, tm=1024, tn=D_MODEL):
    T = x2.shape[0]
    return pl.pallas_call(
        functools.partial(_inproj_body, q_block=ZC_Q * LANES // tn),
        grid=(T // tm, Z_WIDTH // tn),
        in_specs=[
            pl.BlockSpec((tm, D_MODEL), lambda i, j: (i, 0)),
            pl.BlockSpec((1, D_MODEL), lambda i, j: (0, 0)),
            pl.BlockSpec((D_MODEL, tn), lambda i, j: (0, j)),
            pl.BlockSpec((D_MODEL, LANES), lambda i, j: (0, 0)),
        ],
        out_specs=[
            pl.BlockSpec((tm, tn), lambda i, j: (i, j)),
            pl.BlockSpec((tm, LANES), lambda i, j: (i, 0)),
        ],
        out_shape=[
            jax.ShapeDtypeStruct((T, Z_WIDTH), BF16),
            jax.ShapeDtypeStruct((T, LANES), F32),
        ],
        scratch_shapes=[pltpu.VMEM((tm, D_MODEL), BF16)],
        compiler_params=_cparams(("arbitrary", "arbitrary")),
        name="inproj",
    )(x2, g, w_main, w_fl)


C_TERMS = 3


def _log_sigmoid(z):
    return jnp.minimum(z, 0.0) - jnp.log1p(jnp.exp(-jnp.abs(z)))


def _forget_body(fl_ref, b_ref, c_ref, *, seq):
    lf = _log_sigmoid(fl_ref[...] + b_ref[...])
    row = lax.broadcasted_iota(jnp.int32, lf.shape, 0)
    lane = lax.broadcasted_iota(jnp.int32, lf.shape, 1)
    c = lf
    k = 1
    while k < seq:
        c = c + jnp.where(row >= k, pltpu.roll(c, k, axis=0), 0.0)
        k *= 2
    rem = c * LOG2E
    terms = []
    for _ in range(C_TERMS):
        t = rem.astype(BF16).astype(F32)
        terms.append(t)
        rem = rem - t
    for h in range(N_HEADS):
        out = jnp.zeros(lf.shape, F32)
        for n, t in enumerate(terms):
            out = jnp.where(lane == n, t[:, h:h + 1], out)
        c_ref[0, h] = out.astype(BF16)


def _forget(fl, b_pad, *, batch, seq):
    return pl.pallas_call(
        functools.partial(_forget_body, seq=seq),
        grid=(batch,),
        in_specs=[
            pl.BlockSpec((seq, LANES), lambda b: (b, 0)),
            pl.BlockSpec((1, LANES), lambda b: (0, 0)),
        ],
        out_specs=pl.BlockSpec((1, N_HEADS, seq, LANES), lambda b: (b, 0, 0, 0)),
        out_shape=jax.ShapeDtypeStruct((batch, N_HEADS, seq, LANES), BF16),
        compiler_params=_cparams(("arbitrary",)),
        name="forget",
    )(fl, b_pad)


ATTN_HEADS_PER_STEP = 2


def _attn_body(q_ref, k_ref, v_ref, kc_ref, o_ref, *, tq):
    i = pl.program_id(2)
    lane = lax.broadcasted_iota(jnp.int32, (tq, LANES), 1)
    qc = jnp.where(lane < C_TERMS, -1.0, 0.0).astype(BF16)
    heads = range(ATTN_HEADS_PER_STEP)
    cols = [slice(g * HEAD_DIM, (g + 1) * HEAD_DIM) for g in heads]
    qs = [jnp.concatenate([q_ref[:, cols[g]], qc], axis=1) for g in heads]

    def step(j, carry, masked):
        off = pl.multiple_of(j * tq, tq)
        out = []
        for g in heads:
            m, l, acc = carry[g]
            kj = jnp.concatenate([k_ref[pl.ds(off, tq), cols[g]], kc_ref[0, g, pl.ds(off, tq), :]], axis=1)
            s = lax.dot_general(qs[g], kj, (((1,), (1,)), ((), ())), preferred_element_type=F32)
            if masked:
                r = lax.broadcasted_iota(jnp.int32, s.shape, 0)
                cidx = lax.broadcasted_iota(jnp.int32, s.shape, 1)
                s = jnp.where(cidx <= r, s, -jnp.inf)
            m_new = jnp.maximum(m, jnp.max(s, axis=-1, keepdims=True))
            p = jnp.exp2(s - m_new)
            alpha = jnp.exp2(m - m_new)
            l = alpha * l + jnp.sum(p, axis=-1, keepdims=True)
            acc = alpha * acc + jnp.dot(p.astype(BF16), v_ref[pl.ds(off, tq), cols[g]],
                                        preferred_element_type=F32)
            out.append((m_new, l, acc))
        return tuple(out)

    init = tuple((jnp.full((tq, 1), -jnp.inf, F32), jnp.zeros((tq, 1), F32), jnp.zeros((tq, HEAD_DIM), F32))
                 for _ in heads)
    carry = lax.fori_loop(0, i, lambda j, c: step(j, c, False), init)
    carry = step(i, carry, True)
    for g in heads:
        _, l, acc = carry[g]
        o_ref[:, cols[g]] = (acc / l).astype(BF16)


def _attention(z, caug, *, batch, seq, tq=512):
    nq = seq // tq
    T = batch * seq
    G = ATTN_HEADS_PER_STEP
    W = G * HEAD_DIM
    return pl.pallas_call(
        functools.partial(_attn_body, tq=tq),
        grid=(batch, N_HEADS // G, nq),
        in_specs=[
            pl.BlockSpec((tq, W), lambda b, h, i: (b * nq + i, ZC_Q // G + h)),
            pl.BlockSpec((seq, W), lambda b, h, i: (b, ZC_K // G + h)),
            pl.BlockSpec((seq, W), lambda b, h, i: (b, ZC_V // G + h)),
            pl.BlockSpec((1, G, seq, LANES), lambda b, h, i: (b, h, 0, 0)),
        ],
        out_specs=pl.BlockSpec((tq, W), lambda b, h, i: (b * nq + i, h)),
        out_shape=jax.ShapeDtypeStruct((T, D_MODEL), BF16),
        compiler_params=_cparams(("arbitrary", "arbitrary", "arbitrary")),
        name="attn",
    )(z, z, z, caug)


def _lru_body(xr_ref, gr_ref, cw_ref, cb_ref, wa_ref, wx_ref, ba_ref, bx_ref, lam_ref,
              hg_ref, xbuf, hcar, *, ts):
    i = pl.program_id(1)

    @pl.when(i == 0)
    def _():
        xbuf[0:SUBLANES, :] = jnp.zeros((SUBLANES, D_MODEL), F32)
        hcar[...] = jnp.zeros_like(hcar)

    @pl.when(i > 0)
    def _():
        xbuf[0:SUBLANES, :] = xbuf[ts:ts + SUBLANES, :]

    xbuf[SUBLANES:ts + SUBLANES, :] = xr_ref[...].astype(F32)

    base = SUBLANES - (CONV_W - 1)
    xc = cw_ref[0:1, :] * xbuf[base:base + ts, :]
    for k in range(1, CONV_W):
        xc = xc + cw_ref[k:k + 1, :] * xbuf[base + k:base + k + ts, :]
    xc = xc + cb_ref[...]

    xcb = xc.astype(BF16)
    ra = jnp.concatenate(
        [jnp.dot(xcb[:, n * LRU_BLOCK_W:(n + 1) * LRU_BLOCK_W], wa_ref[n], preferred_element_type=F32)
         for n in range(LRU_BLOCKS)], axis=-1)
    rx = jnp.concatenate(
        [jnp.dot(xcb[:, n * LRU_BLOCK_W:(n + 1) * LRU_BLOCK_W], wx_ref[n], preferred_element_type=F32)
         for n in range(LRU_BLOCKS)], axis=-1)
    r = jax.nn.sigmoid(ra + ba_ref[...])
    ig = jax.nn.sigmoid(rx + bx_ref[...])
    nlam = -lam_ref[...]
    softplus = jnp.maximum(nlam, 0.0) + jnp.log1p(jnp.exp(-jnp.abs(nlam)))
    log_a = (-LRU_C * r) * softplus
    a = jnp.exp(log_a)
    th = jnp.tanh(log_a)
    mult = jnp.sqrt(-2.0 * th / (1.0 - th))
    b = mult * ig * xc

    row = lax.broadcasted_iota(jnp.int32, (SUBLANES, D_MODEL), 0)
    keeps = [(k, row >= k) for k in (1, 2, 4)]
    hprev = jnp.broadcast_to(hcar[...], (SUBLANES, D_MODEL))
    pieces = []
    for j in range(ts // SUBLANES):
        aj = a[j * SUBLANES:(j + 1) * SUBLANES, :]
        bj = b[j * SUBLANES:(j + 1) * SUBLANES, :]
        for k, keep in keeps:
            a_sh = jnp.where(keep, pltpu.roll(aj, k, axis=0), 1.0)
            b_sh = jnp.where(keep, pltpu.roll(bj, k, axis=0), 0.0)
            bj = aj * b_sh + bj
            aj = aj * a_sh
        hj = bj + aj * hprev
        hprev = jnp.broadcast_to(hj[SUBLANES - 1:SUBLANES, :], (SUBLANES, D_MODEL))
        pieces.append(hj)
    h = jnp.concatenate(pieces, axis=0)
    hcar[...] = h[ts - 1:ts, :]

    hg_ref[...] = (h * jax.nn.gelu(gr_ref[...].astype(F32))).astype(BF16)


def _lru(z, conv_w, conv_b, wa, wx, ba, bx, lam, *, batch, seq, ts=256):
    ns = seq // ts
    T = batch * seq
    full = lambda shape: pl.BlockSpec(shape, lambda b, i: (0,) * len(shape))
    return pl.pallas_call(
        functools.partial(_lru_body, ts=ts),
        grid=(batch, ns),
        in_specs=[
            pl.BlockSpec((ts, D_MODEL), lambda b, i: (b * ns + i, ZC_XR // SUBLANES)),
            pl.BlockSpec((ts, D_MODEL), lambda b, i: (b * ns + i, ZC_GR // SUBLANES)),
            full((CONV_W, D_MODEL)),
            full((1, D_MODEL)),
            full((LRU_BLOCKS, LRU_BLOCK_W, LRU_BLOCK_W)),
            full((LRU_BLOCKS, LRU_BLOCK_W, LRU_BLOCK_W)),
            full((1, D_MODEL)),
            full((1, D_MODEL)),
            full((1, D_MODEL)),
        ],
        out_specs=pl.BlockSpec((ts, D_MODEL), lambda b, i: (b * ns + i, 0)),
        out_shape=jax.ShapeDtypeStruct((T, D_MODEL), BF16),
        scratch_shapes=[pltpu.VMEM((ts + SUBLANES, D_MODEL), F32), pltpu.VMEM((1, D_MODEL), F32)],
        compiler_params=_cparams(("arbitrary", "arbitrary")),
        name="lru",
    )(z, z, conv_w, conv_b, wa, wx, ba, bx, lam)


def _merge_body(hg_ref, o_ref, ml_ref, ma_ref, x_ref, wl_ref, wat_ref, wo_ref, g_ref, wr_ref, br_ref,
                x1_ref, v_ref, ri_ref, cnt_ref, carry_ref, *, tm):
    @pl.when(pl.program_id(0) == 0)
    def _():
        carry_ref[...] = jnp.zeros_like(carry_ref)

    yl = jnp.dot(hg_ref[...], wl_ref[...], preferred_element_type=F32)
    ya = jnp.dot(o_ref[...], wat_ref[...], preferred_element_type=F32)
    merged = jax.nn.sigmoid(ml_ref[...].astype(F32)) * yl + jax.nn.sigmoid(ma_ref[...].astype(F32)) * ya
    x1 = x_ref[...] + jnp.dot(merged.astype(BF16), wo_ref[...], preferred_element_type=F32)
    x1_ref[...] = x1
    ms = jnp.mean(x1 * x1, axis=-1, keepdims=True)
    v = x1 * lax.rsqrt(ms + EPS) * g_ref[...]
    v_ref[...] = v

    logits = jnp.dot(v.astype(BF16), wr_ref[...], preferred_element_type=F32) + br_ref[...]
    lane = lax.broadcasted_iota(jnp.int32, logits.shape, 1)
    ninf = -jnp.inf
    big = jnp.int32(1 << 20)

    gl = jnp.where(lane < N_GROUPS, logits, ninf)
    gmax = jnp.max(gl, axis=-1, keepdims=True)
    gsel = jnp.min(jnp.where(gl == gmax, lane, big), axis=-1, keepdims=True)
    pg = 1.0 / jnp.sum(jnp.exp(gl - gmax), axis=-1, keepdims=True)

    lo = R_EXP0 + gsel * EXPERTS_PER_GROUP
    el = jnp.where(lane >= lo, jnp.where(lane < lo + EXPERTS_PER_GROUP, logits, ninf), ninf)
    v1 = jnp.max(el, axis=-1, keepdims=True)
    i1 = jnp.min(jnp.where(el == v1, lane, big), axis=-1, keepdims=True)
    el2 = jnp.where(lane == i1, ninf, el)
    v2 = jnp.max(el2, axis=-1, keepdims=True)
    i2 = jnp.min(jnp.where(el2 == v2, lane, big), axis=-1, keepdims=True)
    e21 = jnp.exp(v2 - v1)
    p1 = 1.0 / (1.0 + e21)
    w0 = pg * p1
    w1 = pg * (e21 * p1)

    hit0 = lane == i1
    hit1 = lane == i2
    onehot = jnp.where(hit0, 1.0, jnp.where(hit1, 1.0, 0.0))
    rr = lax.broadcasted_iota(jnp.int32, (tm, tm), 0)
    cc = lax.broadcasted_iota(jnp.int32, (tm, tm), 1)
    tri = jnp.where(cc < rr, 1.0, 0.0).astype(BF16)
    before = jnp.dot(tri, onehot.astype(BF16), preferred_element_type=F32) + carry_ref[...]
    rank0 = jnp.sum(jnp.where(hit0, before, 0.0), axis=-1, keepdims=True)
    rank1 = jnp.sum(jnp.where(hit1, before, 0.0), axis=-1, keepdims=True)
    carry = carry_ref[...] + jnp.sum(onehot, axis=0, keepdims=True)
    carry_ref[...] = carry
    cnt_ref[...] = carry

    e0 = (i1 - R_EXP0).astype(F32)
    e1 = (i2 - R_EXP0).astype(F32)
    ri = jnp.where(lane == 0, e0,
         jnp.where(lane == 1, e1,
         jnp.where(lane == 2, w0,
         jnp.where(lane == 3, w1,
         jnp.where(lane == 4, rank0,
         jnp.where(lane == 5, rank1, 0.0))))))
    ri_ref[...] = ri


def _merge(hg, o, z, x2, wl, wat, wo, g, wr, br, *, tm=512):
    T = x2.shape[0]
    full = lambda shape: pl.BlockSpec(shape, lambda i: (0,) * len(shape))
    return pl.pallas_call(
        functools.partial(_merge_body, tm=tm),
        grid=(T // tm,),
        in_specs=[
            pl.BlockSpec((tm, D_MODEL), lambda i: (i, 0)),
            pl.BlockSpec((tm, D_MODEL), lambda i: (i, 0)),
            pl.BlockSpec((tm, D_MODEL), lambda i: (i, ZC_ML // SUBLANES)),
            pl.BlockSpec((tm, D_MODEL), lambda i: (i, ZC_MA // SUBLANES)),
            pl.BlockSpec((tm, D_MODEL), lambda i: (i, 0)),
            full((D_MODEL, D_MODEL)),
            full((D_MODEL, D_MODEL)),
            full((D_MODEL, D_MODEL)),
            full((1, D_MODEL)),
            full((D_MODEL, LANES)),
            full((1, LANES)),
        ],
        out_specs=[
            pl.BlockSpec((tm, D_MODEL), lambda i: (i, 0)),
            pl.BlockSpec((tm, D_MODEL), lambda i: (i, 0)),
            pl.BlockSpec((tm, LANES), lambda i: (i, 0)),
            pl.BlockSpec((1, LANES), lambda i: (0, 0)),
        ],
        out_shape=[
            jax.ShapeDtypeStruct((T, D_MODEL), F32),
            jax.ShapeDtypeStruct((T, D_MODEL), F32),
            jax.ShapeDtypeStruct((T, LANES), F32),
            jax.ShapeDtypeStruct((1, LANES), F32),
        ],
        scratch_shapes=[pltpu.VMEM((1, LANES), F32)],
        compiler_params=_cparams(("arbitrary",)),
        name="merge",
    )(hg, o, z, z, x2, wl, wat, wo, g, wr, br)


def _row_slice(r):
    return pl.ds(pl.multiple_of(r * ROW_CHUNKS, ROW_CHUNKS), ROW_CHUNKS)


def _dispatch_body(dest_ref, fill_ref, v_ref, xs_ref, rows, zrow, zblock, sems, zsem, *, td, nsteps, n_blocks):
    i = pl.program_id(0)
    slot = i % 2

    def copy(step, sl, r, k):
        d = dest_ref[TOP_K * (step * td + r) + k]
        return pltpu.make_async_copy(rows.at[sl, _row_slice(r)], xs_ref.at[_row_slice(d)], sems.at[sl])

    def issue(step, sl):
        def f(r, carry):
            for k in range(TOP_K):
                copy(step, sl, r, k).start()
            return carry
        lax.fori_loop(0, td, f, 0)

    def drain(step, sl):
        def f(r, carry):
            for k in range(TOP_K):
                copy(step, sl, r, k).wait()
            return carry
        lax.fori_loop(0, td, f, 0)

    def zero_copy(e, r):
        return pltpu.make_async_copy(zrow, xs_ref.at[_row_slice(fill_ref[e] + r)], zsem)

    def for_each_padding_row(fn):
        def per_expert(e, carry):
            lax.fori_loop(0, fill_ref[N_EXPERTS + e], lambda r, c: (fn(e, r), c)[1], 0)
            return carry
        lax.fori_loop(0, N_EXPERTS, per_expert, 0)

    def zero_block_copy(blk):
        start = pl.multiple_of(blk * (EXPERT_ROWS * ROW_CHUNKS), EXPERT_ROWS * ROW_CHUNKS)
        return pltpu.make_async_copy(zblock, xs_ref.at[pl.ds(start, EXPERT_ROWS * ROW_CHUNKS)], zsem)

    def for_each_unused_block(fn):
        lax.fori_loop(fill_ref[2 * N_EXPERTS], n_blocks, lambda blk, c: (fn(blk), c)[1], 0)

    @pl.when(i == 0)
    def _():
        zrow[...] = jnp.zeros_like(zrow)
        zblock[...] = jnp.zeros_like(zblock)
        for_each_padding_row(lambda e, r: zero_copy(e, r).start())
        for_each_unused_block(lambda blk: zero_block_copy(blk).start())

    @pl.when(i >= 2)
    def _():
        drain(i - 2, slot)

    for s in range(ROW_CHUNKS):
        rows[slot, pl.ds(s, td, stride=ROW_CHUNKS), :] = v_ref[:, s * LANES:(s + 1) * LANES]
    issue(i, slot)

    @pl.when(i == nsteps - 1)
    def _():
        if nsteps >= 2:
            drain(i - 1, 1 - slot)
        drain(i, slot)
        for_each_padding_row(lambda e, r: zero_copy(e, r).wait())
        for_each_unused_block(lambda blk: zero_block_copy(blk).wait())


def _dispatch(dest, fill, v, n_rows, *, td=256):
    T = v.shape[0]
    nsteps = T // td
    n_blocks = n_rows // EXPERT_ROWS
    grid_spec = pltpu.PrefetchScalarGridSpec(
        num_scalar_prefetch=2,
        grid=(nsteps,),
        in_specs=[pl.BlockSpec((td, D_MODEL), lambda i, dest, fill: (i, 0))],
        out_specs=pl.BlockSpec(memory_space=pl.ANY),
        scratch_shapes=[pltpu.VMEM((2, td * ROW_CHUNKS, LANES), F32),
                        pltpu.VMEM((ROW_CHUNKS, LANES), F32),
                        pltpu.VMEM((EXPERT_ROWS * ROW_CHUNKS, LANES), F32),
                        pltpu.SemaphoreType.DMA((2,)),
                        pltpu.SemaphoreType.DMA(())],
    )
    return pl.pallas_call(
        functools.partial(_dispatch_body, td=td, nsteps=nsteps, n_blocks=n_blocks),
        grid_spec=grid_spec,
        out_shape=jax.ShapeDtypeStruct((n_rows * ROW_CHUNKS, LANES), F32),
        compiler_params=_cparams(("arbitrary",)),
        name="dispatch",
    )(dest, fill, v)


def _expert_body(be_ref, nu_ref, x_ref, w13_ref, w2_ref, y_ref, *, rows):
    del be_ref
    i = pl.program_id(0)

    @pl.when(i < nu_ref[0])
    def _():
        xb = jnp.concatenate(
            [x_ref[pl.ds(s, rows, stride=ROW_CHUNKS), :].astype(BF16) for s in range(ROW_CHUNKS)], axis=-1)
        gu = jnp.dot(xb, w13_ref[0], preferred_element_type=F32)
        hb = (jax.nn.silu(gu[:, :D_EXPERT]) * gu[:, D_EXPERT:]).astype(BF16)
        y = jnp.dot(hb, w2_ref[0], preferred_element_type=F32)
        for s in range(ROW_CHUNKS):
            y_ref[pl.ds(s, rows, stride=ROW_CHUNKS), :] = y[:, s * LANES:(s + 1) * LANES]

    @pl.when(i >= nu_ref[0])
    def _():
        y_ref[...] = jnp.zeros_like(y_ref)


def _experts(blk_expert, n_used, xs, w13, w2, *, rows=EXPERT_ROWS):
    n_blocks = xs.shape[0] // (rows * ROW_CHUNKS)
    grid_spec = pltpu.PrefetchScalarGridSpec(
        num_scalar_prefetch=2,
        grid=(n_blocks,),
        in_specs=[
            pl.BlockSpec((rows * ROW_CHUNKS, LANES), lambda i, be, nu: (jnp.minimum(i, nu[0] - 1), 0)),
            pl.BlockSpec((1, D_MODEL, 2 * D_EXPERT), lambda i, be, nu: (be[i], 0, 0)),
            pl.BlockSpec((1, D_EXPERT, D_MODEL), lambda i, be, nu: (be[i], 0, 0)),
        ],
        out_specs=pl.BlockSpec((rows * ROW_CHUNKS, LANES), lambda i, be, nu: (i, 0)),
    )
    return pl.pallas_call(
        functools.partial(_expert_body, rows=rows),
        grid_spec=grid_spec,
        out_shape=jax.ShapeDtypeStruct(xs.shape, F32),
        compiler_params=_cparams(("arbitrary",)),
        name="experts",
    )(blk_expert, n_used, xs, w13, w2)


def _combine_body(dest_ref, y_ref, x1_ref, ri_ref, g_ref, out_ref, buf, sems, *, tc, nsteps, final_norm):
    i = pl.program_id(0)
    slot = i % 2

    def copy(step, sl, r, k):
        d = dest_ref[TOP_K * (step * tc + r) + k]
        return pltpu.make_async_copy(y_ref.at[_row_slice(d)], buf.at[sl, k, _row_slice(r)], sems.at[sl])

    def issue(step, sl):
        def f(r, carry):
            for k in range(TOP_K):
                copy(step, sl, r, k).start()
            return carry
        lax.fori_loop(0, tc, f, 0)

    def drain(step, sl):
        def f(r, carry):
            for k in range(TOP_K):
                copy(step, sl, r, k).wait()
            return carry
        lax.fori_loop(0, tc, f, 0)

    @pl.when(i == 0)
    def _():
        issue(i, slot)

    @pl.when(i + 1 < nsteps)
    def _():
        issue(i + 1, 1 - slot)

    drain(i, slot)

    def rows_of(k):
        return jnp.concatenate(
            [buf[slot, k, pl.ds(s, tc, stride=ROW_CHUNKS), :] for s in range(ROW_CHUNKS)], axis=-1)

    ri = ri_ref[...]
    w0 = ri[:, 2:3]
    w1 = ri[:, 3:4]
    x2 = x1_ref[...] + (w0 * rows_of(0) + w1 * rows_of(1))
    if final_norm:
        ms = jnp.mean(x2 * x2, axis=-1, keepdims=True)
        x2 = x2 * lax.rsqrt(ms + EPS) * g_ref[...]
    out_ref[...] = x2


def _combine(dest, y, x1, ri, g, *, final_norm, tc=256):
    T = x1.shape[0]
    nsteps = T // tc
    grid_spec = pltpu.PrefetchScalarGridSpec(
        num_scalar_prefetch=1,
        grid=(nsteps,),
        in_specs=[
            pl.BlockSpec(memory_space=pl.ANY),
            pl.BlockSpec((tc, D_MODEL), lambda i, dest: (i, 0)),
            pl.BlockSpec((tc, LANES), lambda i, dest: (i, 0)),
            pl.BlockSpec((1, D_MODEL), lambda i, dest: (0, 0)),
        ],
        out_specs=pl.BlockSpec((tc, D_MODEL), lambda i, dest: (i, 0)),
        scratch_shapes=[pltpu.VMEM((2, TOP_K, tc * ROW_CHUNKS, LANES), F32),
                        pltpu.SemaphoreType.DMA((2,))],
    )
    return pl.pallas_call(
        functools.partial(_combine_body, tc=tc, nsteps=nsteps, final_norm=final_norm),
        grid_spec=grid_spec,
        out_shape=jax.ShapeDtypeStruct((T, D_MODEL), F32),
        compiler_params=_cparams(("arbitrary",)),
        name="combine",
    )(dest, y, x1, ri, g)


def _pad_lanes(a, width=LANES):
    return jnp.pad(a, ((0, 0), (0, width - a.shape[-1])))


def _layer(x2, batch, seq, g_mix, w_in, conv_w, conv_b, w_rg_a, b_rg_a, w_rg_x, b_rg_x, lam, b_forget,
           w_lru_out, w_attn_out, w_out, g_ffn, w_rgrp, b_rgrp, w_rexp, b_rexp, w1, w3, w2):
    T = batch * seq
    row = lambda a: a.reshape(1, -1).astype(F32)

    fl0 = 5 * D_MODEL
    w_main = jnp.concatenate([w_in[:, :fl0], w_in[:, fl0 + N_HEADS:]], axis=1).astype(BF16)
    w_fl = _pad_lanes(w_in[:, fl0:fl0 + N_HEADS]).astype(BF16)
    z, fl = _inproj(x2, row(g_mix), w_main, w_fl)

    caug = _forget(fl, _pad_lanes(row(b_forget)), batch=batch, seq=seq)
    o = _attention(z, caug, batch=batch, seq=seq)

    hg = _lru(z, conv_w.astype(F32), row(conv_b), w_rg_a.astype(BF16), w_rg_x.astype(BF16),
              row(b_rg_a), row(b_rg_x), row(lam), batch=batch, seq=seq)

    wr = _pad_lanes(jnp.concatenate([w_rgrp, w_rexp], axis=1)).astype(BF16)
    br = _pad_lanes(jnp.concatenate([row(b_rgrp), row(b_rexp)], axis=1))
    x1, v, ri, cnt = _merge(hg, o, z, x2, w_lru_out.astype(BF16), w_attn_out.astype(BF16),
                            w_out.astype(BF16), row(g_ffn), wr, br)

    e = ri[:, 0:TOP_K].astype(jnp.int32)
    rank = ri[:, 4:4 + TOP_K].astype(jnp.int32)
    counts = cnt[0, R_EXP0:R_EXP0 + N_EXPERTS].astype(jnp.int32)
    padded = (counts + EXPERT_ROWS - 1) // EXPERT_ROWS * EXPERT_ROWS
    pad_end = jnp.cumsum(padded)
    pad_start = pad_end - padded
    dest = (pad_start[e] + rank).reshape(T * TOP_K).astype(jnp.int32)
    n_blocks = (T * TOP_K + N_EXPERTS * (EXPERT_ROWS - 1) + EXPERT_ROWS - 1) // EXPERT_ROWS
    blk_start = jnp.arange(n_blocks, dtype=jnp.int32) * EXPERT_ROWS
    blk_expert = jnp.minimum(
        jnp.sum((pad_end[None, :] <= blk_start[:, None]).astype(jnp.int32), axis=1), N_EXPERTS - 1)
    n_used = (pad_end[-1:] // EXPERT_ROWS).astype(jnp.int32)

    fill = jnp.concatenate([pad_start + counts, padded - counts, n_used]).astype(jnp.int32)
    xs = _dispatch(dest, fill, v, n_blocks * EXPERT_ROWS)
    w13 = jnp.concatenate([w1, w3], axis=-1).astype(BF16)
    y = _experts(blk_expert, n_used, xs, w13, w2.astype(BF16))
    return x1, y, dest, ri


def kernel(x, g_mix, w_in, conv_w, conv_b, w_rg_a, b_rg_a, w_rg_x, b_rg_x, lru_lambda, b_forget, w_lru_out, w_attn_out, w_out, g_ffn, w_route_group, b_route_group, w_route_expert, b_route_expert, w_exp_gate, w_exp_up, w_exp_down, g_final):
    batch, seq, _ = x.shape
    depth = g_mix.shape[0]
    x2 = x.reshape(batch * seq, D_MODEL)
    for l in range(depth):
        x1, y, dest, ri = _layer(
            x2, batch, seq, g_mix[l], w_in[l], conv_w[l], conv_b[l], w_rg_a[l], b_rg_a[l], w_rg_x[l],
            b_rg_x[l], lru_lambda[l], b_forget[l], w_lru_out[l], w_attn_out[l], w_out[l], g_ffn[l],
            w_route_group[l], b_route_group[l], w_route_expert[l], b_route_expert[l],
            w_exp_gate[l], w_exp_up[l], w_exp_down[l])
        x2 = _combine(dest, y, x1, ri, g_final.reshape(1, -1).astype(F32), final_norm=l == depth - 1)
    return x2.reshape(batch, seq, D_MODEL)
```

```python
import functools

import jax
import jax.numpy as jnp
from jax import lax
from jax.experimental import pallas as pl
from jax.experimental.pallas import tpu as pltpu

F32 = jnp.float32
BF16 = jnp.bfloat16

D_MODEL = 1024
LRU_BLOCK_W = 256
LRU_BLOCKS = D_MODEL // LRU_BLOCK_W
CONV_W = 4
LRU_C = 8.0
N_HEADS = 8
HEAD_DIM = D_MODEL // N_HEADS
N_GROUPS = 4
EXPERTS_PER_GROUP = 8
N_EXPERTS = N_GROUPS * EXPERTS_PER_GROUP
TOP_K = 2
D_EXPERT = D_MODEL // 2
EPS = 1e-6

LANES = 128
SUBLANES = 8
ROW_CHUNKS = D_MODEL // LANES
VMEM_LIMIT = 48 * 1024 * 1024

ZC_XR, ZC_GR, ZC_Q, ZC_K, ZC_V, ZC_ML, ZC_MA = 0, 8, 16, 24, 32, 40, 48
Z_WIDTH = 7 * D_MODEL

R_EXP0 = N_GROUPS

EXPERT_ROWS = 256


def _cparams(sem):
    return pltpu.CompilerParams(dimension_semantics=sem, vmem_limit_bytes=VMEM_LIMIT)


LOG2E = 1.4426950408889634
Q_PRESCALE = HEAD_DIM ** -0.5 * LOG2E


def _inproj_body(x_ref, g_ref, w_ref, wfl_ref, z_ref, fl_ref, u_ref, *, q_block):
    j = pl.program_id(1)

    @pl.when(j == 0)
    def _():
        x = x_ref[...]
        ms = jnp.mean(x * x, axis=-1, keepdims=True)
        u = (x * lax.rsqrt(ms + EPS) * g_ref[...]).astype(BF16)
        u_ref[...] = u
        fl_ref[...] = jnp.dot(u, wfl_ref[...], preferred_element_type=F32)

    @pl.when(j == q_block)
    def _():
        acc = jnp.dot(u_ref[...], w_ref[...], preferred_element_type=F32)
        z_ref[...] = (acc * Q_PRESCALE).astype(BF16)

    @pl.when(j != q_block)
    def _():
        z_ref[...] = jnp.dot(u_ref[...], w_ref[...], preferred_element_type=F32).astype(BF16)


def _inproj(x2, g, w_main, w_fl, *, tm=1024, tn=D_MODEL):
    T = x2.shape[0]
    return pl.pallas_call(
        functools.partial(_inproj_body, q_block=ZC_Q * LANES // tn),
        grid=(T // tm, Z_WIDTH // tn),
        in_specs=[
            pl.BlockSpec((tm, D_MODEL), lambda i, j: (i, 0)),
            pl.BlockSpec((1, D_MODEL), lambda i, j: (0, 0)),
            pl.BlockSpec((D_MODEL, tn), lambda i, j: (0, j)),
            pl.BlockSpec((D_MODEL, LANES), lambda i, j: (0, 0)),
        ],
        out_specs=[
            pl.BlockSpec((tm, tn), lambda i, j: (i, j)),
            pl.BlockSpec((tm, LANES), lambda i, j: (i, 0)),
        ],
        out_shape=[
            jax.ShapeDtypeStruct((T, Z_WIDTH), BF16),
            jax.ShapeDtypeStruct((T, LANES), F32),
        ],
        scratch_shapes=[pltpu.VMEM((tm, D_MODEL), BF16)],
        compiler_params=_cparams(("arbitrary", "arbitrary")),
        name="inproj",
    )(x2, g, w_main, w_fl)


C_TERMS = 3


def _log_sigmoid(z):
    return jnp.minimum(z, 0.0) - jnp.log1p(jnp.exp(-jnp.abs(z)))


def _forget_body(fl_ref, b_ref, c_ref, *, seq):
    lf = _log_sigmoid(fl_ref[...] + b_ref[...])
    row = lax.broadcasted_iota(jnp.int32, lf.shape, 0)
    lane = lax.broadcasted_iota(jnp.int32, lf.shape, 1)
    c = lf
    k = 1
    while k < seq:
        c = c + jnp.where(row >= k, pltpu.roll(c, k, axis=0), 0.0)
        k *= 2
    rem = c * LOG2E
    out = jnp.zeros(lf.shape, F32)
    for n in range(C_TERMS):
        t = rem.astype(BF16).astype(F32)
        rem = rem - t
        shifted = t if n == 0 else pltpu.roll(t, n * N_HEADS, axis=1)
        out = jnp.where((lane >= n * N_HEADS) & (lane < (n + 1) * N_HEADS), shifted, out)
    c_ref[...] = out.astype(BF16)


def _forget(fl, b_pad, *, batch, seq):
    return pl.pallas_call(
        functools.partial(_forget_body, seq=seq),
        grid=(batch,),
        in_specs=[
            pl.BlockSpec((seq, LANES), lambda b: (b, 0)),
            pl.BlockSpec((1, LANES), lambda b: (0, 0)),
        ],
        out_specs=pl.BlockSpec((seq, LANES), lambda b: (b, 0)),
        out_shape=jax.ShapeDtypeStruct((batch * seq, LANES), BF16),
        compiler_params=_cparams(("arbitrary",)),
        name="forget",
    )(fl, b_pad)


ATTN_HEADS_PER_STEP = 2


def _attn_body(q_ref, k_ref, v_ref, kc_ref, o_ref, *, tq):
    i = pl.program_id(2)
    lane = lax.broadcasted_iota(jnp.int32, (tq, LANES), 1)
    heads = range(ATTN_HEADS_PER_STEP)
    cols =[slice(g * HEAD_DIM, (g + 1) * HEAD_DIM) for g in heads]
    qs = []
    for g in heads:
        h = pl.program_id(1) * ATTN_HEADS_PER_STEP + g
        mine = (lane < C_TERMS * N_HEADS) & ((lane & (N_HEADS - 1)) == h)
        qc = jnp.where(mine, -1.0, 0.0).astype(BF16)
        qs.append(jnp.concatenate([q_ref[:, cols[g]], qc], axis=1))

    def step(j, carry, masked):
        off = pl.multiple_of(j * tq, tq)
        out = []
        for g in heads:
            m, l, acc = carry[g]
            kj = jnp.concatenate([k_ref[pl.ds(off, tq), cols[g]], kc_ref[pl.ds(off, tq), :]], axis=1)
            s = lax.dot_general(qs[g], kj, (((1,), (1,)), ((), ())), preferred_element_type=F32)
            if masked:
                r = lax.broadcasted_iota(jnp.int32, s.shape, 0)
                cidx = lax.broadcasted_iota(jnp.int32, s.shape, 1)
                s = jnp.where(cidx <= r, s, -jnp.inf)
            m_new = jnp.maximum(m, jnp.max(s, axis=-1, keepdims=True))
            p = jnp.exp2(s - m_new)
            alpha = jnp.exp2(m - m_new)
            l = alpha * l + jnp.sum(p, axis=-1, keepdims=True)
            acc = alpha * acc + jnp.dot(p.astype(BF16), v_ref[pl.ds(off, tq), cols[g]],
                                        preferred_element_type=F32)
            out.append((m_new, l, acc))
        return tuple(out)

    init = tuple((jnp.full((tq, 1), -jnp.inf, F32), jnp.zeros((tq, 1), F32), jnp.zeros((tq, HEAD_DIM), F32))
                 for _ in heads)
    carry = lax.fori_loop(0, i, lambda j, c: step(j, c, False), init)
    carry = step(i, carry, True)
    for g in heads:
        _, l, acc = carry[g]
        o_ref[:, cols[g]] = (acc / l).astype(BF16)


def _attention(z, caug, *, batch, seq, tq=512):
    nq = seq // tq
    T = batch * seq
    G = ATTN_HEADS_PER_STEP
    W = G * HEAD_DIM
    return pl.pallas_call(
        functools.partial(_attn_body, tq=tq),
        grid=(batch, N_HEADS // G, nq),
        in_specs=[
            pl.BlockSpec((tq, W), lambda b, h, i: (b * nq + i, ZC_Q // G + h)),
            pl.BlockSpec((seq, W), lambda b, h, i: (b, ZC_K // G + h)),
            pl.BlockSpec((seq, W), lambda b, h, i: (b, ZC_V // G + h)),
            pl.BlockSpec((seq, LANES), lambda b, h, i: (b, 0)),
        ],
        out_specs=pl.BlockSpec((tq, W), lambda b, h, i: (b * nq + i, h)),
        out_shape=jax.ShapeDtypeStruct((T, D_MODEL), BF16),
        compiler_params=_cparams(("arbitrary", "arbitrary", "arbitrary")),
        name="attn",
    )(z, z, z, caug)


def _lru_body(xr_ref, gr_ref, cw_ref, cb_ref, wa_ref, wx_ref, ba_ref, bx_ref, lam_ref,
              hg_ref, xbuf, hcar, *, ts):
    i = pl.program_id(1)

    @pl.when(i == 0)
    def _():
        xbuf[0:SUBLANES, :] = jnp.zeros((SUBLANES, D_MODEL), F32)
        hcar[...] = jnp.zeros_like(hcar)

    @pl.when(i > 0)
    def _():
        xbuf[0:SUBLANES, :] = xbuf[ts:ts + SUBLANES, :]

    xbuf[SUBLANES:ts + SUBLANES, :] = xr_ref[...].astype(F32)

    base = SUBLANES - (CONV_W - 1)
    xc = cw_ref[0:1, :] * xbuf[base:base + ts, :]
    for k in range(1, CONV_W):
        xc = xc + cw_ref[k:k + 1, :] * xbuf[base + k:base + k + ts, :]
    xc = xc + cb_ref[...]

    xcb = xc.astype(BF16)
    ra = jnp.concatenate(
        [jnp.dot(xcb[:, n * LRU_BLOCK_W:(n + 1) * LRU_BLOCK_W], wa_ref[n], preferred_element_type=F32)
         for n in range(LRU_BLOCKS)], axis=-1)
    rx = jnp.concatenate(
        [jnp.dot(xcb[:, n * LRU_BLOCK_W:(n + 1) * LRU_BLOCK_W], wx_ref[n], preferred_element_type=F32)
         for n in range(LRU_BLOCKS)], axis=-1)
    r = jax.nn.sigmoid(ra + ba_ref[...])
    ig = jax.nn.sigmoid(rx + bx_ref[...])
    nlam = -lam_ref[...]
    softplus = jnp.maximum(nlam, 0.0) + jnp.log1p(jnp.exp(-jnp.abs(nlam)))
    log_a = (-LRU_C * r) * softplus
    a = jnp.exp(log_a)
    th = jnp.tanh(log_a)
    mult = jnp.sqrt(-2.0 * th / (1.0 - th))
    b = mult * ig * xc

    row = lax.broadcasted_iota(jnp.int32, (SUBLANES, D_MODEL), 0)
    keeps = [(k, row >= k) for k in (1, 2, 4)]
    hprev = jnp.broadcast_to(hcar[...], (SUBLANES, D_MODEL))
    pieces = []
    for j in range(ts // SUBLANES):
        aj = a[j * SUBLANES:(j + 1) * SUBLANES, :]
        bj = b[j * SUBLANES:(j + 1) * SUBLANES, :]
        for k, keep in keeps:
            a_sh = jnp.where(keep, pltpu.roll(aj, k, axis=0), 1.0)
            b_sh = jnp.where(keep, pltpu.roll(bj, k, axis=0), 0.0)
            bj = aj * b_sh + bj
            aj = aj * a_sh
        hj = bj + aj * hprev
        hprev = jnp.broadcast_to(hj[SUBLANES - 1:SUBLANES, :], (SUBLANES, D_MODEL))
        pieces.append(hj)
    h = jnp.concatenate(pieces, axis=0)
    hcar[...] = h[ts - 1:ts, :]

    hg_ref[...] = (h * jax.nn.gelu(gr_ref[...].astype(F32))).astype(BF16)


def _lru(z, conv_w, conv_b, wa, wx, ba, bx, lam, *, batch, seq, ts=256):
    ns = seq // ts
    T = batch * seq
    full = lambda shape: pl.BlockSpec(shape, lambda b, i: (0,) * len(shape))
    return pl.pallas_call(
        functools.partial(_lru_body, ts=ts),
        grid=(batch, ns),
        in_specs=[
            pl.BlockSpec((ts, D_MODEL), lambda b, i: (b * ns + i, ZC_XR // SUBLANES)),
            pl.BlockSpec((ts, D_MODEL), lambda b, i: (b * ns + i, ZC_GR // SUBLANES)),
            full((CONV_W, D_MODEL)),
            full((1, D_MODEL)),
            full((LRU_BLOCKS, LRU_BLOCK_W, LRU_BLOCK_W)),
            full((LRU_BLOCKS, LRU_BLOCK_W, LRU_BLOCK_W)),
            full((1, D_MODEL)),
            full((1, D_MODEL)),
            full((1, D_MODEL)),
        ],
        out_specs=pl.BlockSpec((ts, D_MODEL), lambda b, i: (b * ns + i, 0)),
        out_shape=jax.ShapeDtypeStruct((T, D_MODEL), BF16),
        scratch_shapes=[pltpu.VMEM((ts + SUBLANES, D_MODEL), F32), pltpu.VMEM((1, D_MODEL), F32)],
        compiler_params=_cparams(("arbitrary", "arbitrary")),
        name="lru",
    )(z, z, conv_w, conv_b, wa, wx, ba, bx, lam)


MERGE_SUBTILES = 1


def _merge_body(hg_ref, o_ref, ml_ref, ma_ref, x_ref, wl_ref, wat_ref, wo_ref, g_ref, wr_ref, br_ref,
                x1_ref, v_ref, ri_ref, rt_ref, cnt_ref, carry_ref, *, tm):
    @pl.when(pl.program_id(0) == 0)
    def _():
        carry_ref[...] = jnp.zeros_like(carry_ref)

    ts = tm // MERGE_SUBTILES
    lane = lax.broadcasted_iota(jnp.int32, (ts, LANES), 1)
    rr = lax.broadcasted_iota(jnp.int32, (ts, ts), 0)
    cc = lax.broadcasted_iota(jnp.int32, (ts, ts), 1)
    tri = jnp.where(cc < rr, 1.0, 0.0).astype(BF16)
    ninf = -jnp.inf
    big = jnp.int32(1 << 20)
    carry = carry_ref[...]

    for sub in range(MERGE_SUBTILES):
        rows = slice(sub * ts, (sub + 1) * ts)
        yl = jnp.dot(hg_ref[rows, :], wl_ref[...], preferred_element_type=F32)
        ya = jnp.dot(o_ref[rows, :], wat_ref[...], preferred_element_type=F32)
        merged = (jax.nn.sigmoid(ml_ref[rows, :].astype(F32)) * yl
                  + jax.nn.sigmoid(ma_ref[rows, :].astype(F32)) * ya)
        x1 = x_ref[rows, :] + jnp.dot(merged.astype(BF16), wo_ref[...], preferred_element_type=F32)
        x1_ref[rows, :] = x1
        ms = jnp.mean(x1 * x1, axis=-1, keepdims=True)
        v = x1 * lax.rsqrt(ms + EPS) * g_ref[...]
        v_ref[rows, :] = v

        logits = jnp.dot(v.astype(BF16), wr_ref[...], preferred_element_type=F32) + br_ref[...]

        gl = jnp.where(lane < N_GROUPS, logits, ninf)
        gmax = jnp.max(gl, axis=-1, keepdims=True)
        gsel = jnp.min(jnp.where(gl == gmax, lane, big), axis=-1, keepdims=True)
        pg = 1.0 / jnp.sum(jnp.exp(gl - gmax), axis=-1, keepdims=True)

        lo = R_EXP0 + gsel * EXPERTS_PER_GROUP
        el = jnp.where(lane >= lo, jnp.where(lane < lo + EXPERTS_PER_GROUP, logits, ninf), ninf)
        v1 = jnp.max(el, axis=-1, keepdims=True)
        i1 = jnp.min(jnp.where(el == v1, lane, big), axis=-1, keepdims=True)
        el2 = jnp.where(lane == i1, ninf, el)
        v2 = jnp.max(el2, axis=-1, keepdims=True)
        i2 = jnp.min(jnp.where(el2 == v2, lane, big), axis=-1, keepdims=True)
        e21 = jnp.exp(v2 - v1)
        p1 = 1.0 / (1.0 + e21)
        w0 = pg * p1
        w1 = pg * (e21 * p1)

        hit0 = lane == i1
        hit1 = lane == i2
        onehot = jnp.where(hit0, 1.0, jnp.where(hit1, 1.0, 0.0))
        before = jnp.dot(tri, onehot.astype(BF16), preferred_element_type=F32) + carry
        rank0 = jnp.sum(jnp.where(hit0, before, 0.0), axis=-1, keepdims=True)
        rank1 = jnp.sum(jnp.where(hit1, before, 0.0), axis=-1, keepdims=True)
        carry = carry + jnp.sum(onehot, axis=0, keepdims=True)

        e0 = (i1 - R_EXP0).astype(F32)
        e1 = (i2 - R_EXP0).astype(F32)
        ri = jnp.where(lane == 0, e0,
             jnp.where(lane == 1, e1,
             jnp.where(lane == 2, w0,
             jnp.where(lane == 3, w1,
             jnp.where(lane == 4, rank0,
             jnp.where(lane == 5, rank1, 0.0))))))
        ri_ref[rows, :] = ri
        rt_ref[:, rows] = ri.T[:SUBLANES, :]

    carry_ref[...] = carry
    cnt_ref[...] = carry


def _merge(hg, o, z, x2, wl, wat, wo, g, wr, br, *, tm=512):
    T = x2.shape[0]
    full = lambda shape: pl.BlockSpec(shape, lambda i: (0,) * len(shape))
    return pl.pallas_call(
        functools.partial(_merge_body, tm=tm),
        grid=(T // tm,),
        in_specs=[
            pl.BlockSpec((tm, D_MODEL), lambda i: (i, 0)),
            pl.BlockSpec((tm, D_MODEL), lambda i: (i, 0)),
            pl.BlockSpec((tm, D_MODEL), lambda i: (i, ZC_ML // SUBLANES)),
            pl.BlockSpec((tm, D_MODEL), lambda i: (i, ZC_MA // SUBLANES)),
            pl.BlockSpec((tm, D_MODEL), lambda i: (i, 0)),
            full((D_MODEL, D_MODEL)),
            full((D_MODEL, D_MODEL)),
            full((D_MODEL, D_MODEL)),
            full((1, D_MODEL)),
            full((D_MODEL, LANES)),
            full((1, LANES)),
        ],
        out_specs=[
            pl.BlockSpec((tm, D_MODEL), lambda i: (i, 0)),
            pl.BlockSpec((tm, D_MODEL), lambda i: (i, 0)),
            pl.BlockSpec((tm, LANES), lambda i: (i, 0)),
            pl.BlockSpec((SUBLANES, tm), lambda i: (0, i)),
            pl.BlockSpec((1, LANES), lambda i: (0, 0)),
        ],
        out_shape=[
            jax.ShapeDtypeStruct((T, D_MODEL), F32),
            jax.ShapeDtypeStruct((T, D_MODEL), F32),
            jax.ShapeDtypeStruct((T, LANES), F32),
            jax.ShapeDtypeStruct((SUBLANES, T), F32),
            jax.ShapeDtypeStruct((1, LANES), F32),
        ],
        scratch_shapes=[pltpu.VMEM((1, LANES), F32)],
        compiler_params=_cparams(("arbitrary",)),
        name="merge",
    )(hg, o, z, z, x2, wl, wat, wo, g, wr, br)


def _row_slice(r):
    return pl.ds(pl.multiple_of(r * ROW_CHUNKS, ROW_CHUNKS), ROW_CHUNKS)


def _dispatch_body(dest_ref, fill_ref, v_ref, xs_ref, rows, zrow, zblock, sem, zsem, *, td, nsteps, n_blocks):
    i = pl.program_id(0)

    def copy(r, k):
        d = dest_ref[k * (nsteps * td) + i * td + r]
        return pltpu.make_async_copy(rows.at[_row_slice(r)], xs_ref.at[_row_slice(d)], sem)

    def for_each_row(fn):
        def f(r, carry):
            for k in range(TOP_K):
                fn(copy(r, k))
            return carry
        lax.fori_loop(0, td, f, 0)

    def zero_copy(e, r):
        return pltpu.make_async_copy(zrow, xs_ref.at[_row_slice(fill_ref[e] + r)], zsem)

    def for_each_padding_row(fn):
        def per_expert(e, carry):
            lax.fori_loop(0, fill_ref[N_EXPERTS + e], lambda r, c: (fn(zero_copy(e, r)), c)[1], 0)
            return carry
        lax.fori_loop(0, N_EXPERTS, per_expert, 0)

    def zero_block_copy(blk):
        start = pl.multiple_of(blk * (EXPERT_ROWS * ROW_CHUNKS), EXPERT_ROWS * ROW_CHUNKS)
        return pltpu.make_async_copy(zblock, xs_ref.at[pl.ds(start, EXPERT_ROWS * ROW_CHUNKS)], zsem)

    def for_each_unused_block(fn):
        lax.fori_loop(fill_ref[2 * N_EXPERTS], n_blocks, lambda blk, c: (fn(zero_block_copy(blk)), c)[1], 0)

    @pl.when(i == 0)
    def _():
        zrow[...] = jnp.zeros_like(zrow)
        zblock[...] = jnp.zeros_like(zblock)
        for_each_padding_row(lambda cp: cp.start())
        for_each_unused_block(lambda cp: cp.start())

    for s in range(ROW_CHUNKS):
        rows[pl.ds(s, td, stride=ROW_CHUNKS), :] = v_ref[:, s * LANES:(s + 1) * LANES]
    for_each_row(lambda cp: cp.start())
    for_each_row(lambda cp: cp.wait())

    @pl.when(i == nsteps - 1)
    def _():
        for_each_padding_row(lambda cp: cp.wait())
        for_each_unused_block(lambda cp: cp.wait())


def _dispatch(dest, fill, v, n_rows, *, td=256):
    T = v.shape[0]
    nsteps = T // td
    n_blocks = n_rows // EXPERT_ROWS
    grid_spec = pltpu.PrefetchScalarGridSpec(
        num_scalar_prefetch=2,
        grid=(nsteps,),
        in_specs=[pl.BlockSpec((td, D_MODEL), lambda i, dest, fill: (i, 0))],
        out_specs=pl.BlockSpec(memory_space=pl.ANY),
        scratch_shapes=[pltpu.VMEM((td * ROW_CHUNKS, LANES), F32),
                        pltpu.VMEM((ROW_CHUNKS, LANES), F32),
                        pltpu.VMEM((EXPERT_ROWS * ROW_CHUNKS, LANES), F32),
                        pltpu.SemaphoreType.DMA(()),
                        pltpu.SemaphoreType.DMA(())],
    )
    return pl.pallas_call(
        functools.partial(_dispatch_body, td=td, nsteps=nsteps, n_blocks=n_blocks),
        grid_spec=grid_spec,
        out_shape=jax.ShapeDtypeStruct((n_rows * ROW_CHUNKS, LANES), F32),
        compiler_params=_cparams(("arbitrary",)),
        name="dispatch",
    )(dest, fill, v)


def _expert_body(be_ref, nu_ref, x_ref, w1_ref, w3_ref, w2_ref, y_ref, w13b, w2b, *, rows):
    i = pl.program_id(0)

    @pl.when((i == 0) | (be_ref[i] != be_ref[jnp.maximum(i - 1, 0)]))
    def _():
        w13b[:, :D_EXPERT] = w1_ref[0].astype(BF16)
        w13b[:, D_EXPERT:] = w3_ref[0].astype(BF16)
        w2b[...] = w2_ref[0].astype(BF16)

    @pl.when(i < nu_ref[0])
    def _():
        xb = jnp.concatenate(
            [x_ref[pl.ds(s, rows, stride=ROW_CHUNKS), :].astype(BF16) for s in range(ROW_CHUNKS)], axis=-1)
        gu = jnp.dot(xb, w13b[...], preferred_element_type=F32)
        hb = (jax.nn.silu(gu[:, :D_EXPERT]) * gu[:, D_EXPERT:]).astype(BF16)
        y = jnp.dot(hb, w2b[...], preferred_element_type=F32)
        for s in range(ROW_CHUNKS):
            y_ref[pl.ds(s, rows, stride=ROW_CHUNKS), :] = y[:, s * LANES:(s + 1) * LANES]

    @pl.when(i >= nu_ref[0])
    def _():
        y_ref[...] = jnp.zeros_like(y_ref)


def _experts(blk_expert, n_used, xs, w1, w3, w2, *, rows=EXPERT_ROWS):
    n_blocks = xs.shape[0] // (rows * ROW_CHUNKS)
    grid_spec = pltpu.PrefetchScalarGridSpec(
        num_scalar_prefetch=2,
        grid=(n_blocks,),
        in_specs=[
            pl.BlockSpec((rows * ROW_CHUNKS, LANES), lambda i, be, nu: (jnp.minimum(i, nu[0] - 1), 0)),
            pl.BlockSpec((1, D_MODEL, D_EXPERT), lambda i, be, nu: (be[i], 0, 0)),
            pl.BlockSpec((1, D_MODEL, D_EXPERT), lambda i, be, nu: (be[i], 0, 0)),
            pl.BlockSpec((1, D_EXPERT, D_MODEL), lambda i, be, nu: (be[i], 0, 0)),
        ],
        out_specs=pl.BlockSpec((rows * ROW_CHUNKS, LANES), lambda i, be, nu: (i, 0)),
        scratch_shapes=[pltpu.VMEM((D_MODEL, 2 * D_EXPERT), BF16), pltpu.VMEM((D_EXPERT, D_MODEL), BF16)],
    )
    return pl.pallas_call(
        functools.partial(_expert_body, rows=rows),
        grid_spec=grid_spec,
        out_shape=jax.ShapeDtypeStruct(xs.shape, F32),
        compiler_params=_cparams(("arbitrary",)),
        name="experts",
    )(blk_expert, n_used, xs, w1, w3, w2)


def _combine_body(dest_ref, y_ref, x1_ref, ri_ref, g_ref, out_ref, buf, sem, *, tc, nsteps, final_norm):
    i = pl.program_id(0)

    def copy(r, k):
        d = dest_ref[k * (nsteps * tc) + i * tc + r]
        return pltpu.make_async_copy(y_ref.at[_row_slice(d)], buf.at[k, _row_slice(r)], sem)

    def for_each_row(fn):
        def f(r, carry):
            for k in range(TOP_K):
                fn(copy(r, k))
            return carry
        lax.fori_loop(0, tc, f, 0)

    for_each_row(lambda cp: cp.start())
    for_each_row(lambda cp: cp.wait())

    def rows_of(k):
        return jnp.concatenate(
            [buf[k, pl.ds(s, tc, stride=ROW_CHUNKS), :] for s in range(ROW_CHUNKS)], axis=-1)

    ri = ri_ref[...]
    w0 = ri[:, 2:3]
    w1 = ri[:, 3:4]
    x2 = x1_ref[...] + (w0 * rows_of(0) + w1 * rows_of(1))
    if final_norm:
        ms = jnp.mean(x2 * x2, axis=-1, keepdims=True)
        x2 = x2 * lax.rsqrt(ms + EPS) * g_ref[...]
    out_ref[...] = x2


def _combine(dest, y, x1, ri, g, *, final_norm, tc=256):
    T = x1.shape[0]
    nsteps = T // tc
    grid_spec = pltpu.PrefetchScalarGridSpec(
        num_scalar_prefetch=1,
        grid=(nsteps,),
        in_specs=[
            pl.BlockSpec(memory_space=pl.ANY),
            pl.BlockSpec((tc, D_MODEL), lambda i, dest: (i, 0)),
            pl.BlockSpec((tc, LANES), lambda i, dest: (i, 0)),
            pl.BlockSpec((1, D_MODEL), lambda i, dest: (0, 0)),
        ],
        out_specs=pl.BlockSpec((tc, D_MODEL), lambda i, dest: (i, 0)),
        scratch_shapes=[pltpu.VMEM((TOP_K, tc * ROW_CHUNKS, LANES), F32),
                        pltpu.SemaphoreType.DMA(())],
    )
    return pl.pallas_call(
        functools.partial(_combine_body, tc=tc, nsteps=nsteps, final_norm=final_norm),
        grid_spec=grid_spec,
        out_shape=jax.ShapeDtypeStruct((T, D_MODEL), F32),
        compiler_params=_cparams(("arbitrary",)),
        name="combine",
    )(dest, y, x1, ri, g)


def _pad_lanes(a, width=LANES):
    return jnp.pad(a, ((0, 0), (0, width - a.shape[-1])))


def _layer(x2, batch, seq, g_mix, w_in, conv_w, conv_b, w_rg_a, b_rg_a, w_rg_x, b_rg_x, lam, b_forget,
           w_lru_out, w_attn_out, w_out, g_ffn, w_rgrp, b_rgrp, w_rexp, b_rexp, w1, w3, w2):
    T = batch * seq
    row = lambda a: a.reshape(1, -1).astype(F32)

    fl0 = 5 * D_MODEL
    w_main = jnp.concatenate([w_in[:, :fl0], w_in[:, fl0 + N_HEADS:]], axis=1).astype(BF16)
    w_fl = _pad_lanes(w_in[:, fl0:fl0 + N_HEADS]).astype(BF16)
    z, fl = _inproj(x2, row(g_mix), w_main, w_fl)

    caug = _forget(fl, _pad_lanes(row(b_forget)), batch=batch, seq=seq)
    o = _attention(z, caug, batch=batch, seq=seq)

    hg = _lru(z, conv_w.astype(F32), row(conv_b), w_rg_a.astype(BF16), w_rg_x.astype(BF16),
              row(b_rg_a), row(b_rg_x), row(lam), batch=batch, seq=seq)

    wr = _pad_lanes(jnp.concatenate([w_rgrp, w_rexp], axis=1)).astype(BF16)
    br = _pad_lanes(jnp.concatenate([row(b_rgrp), row(b_rexp)], axis=1))
    x1, v, ri, rt, cnt = _merge(hg, o, z, x2, w_lru_out.astype(BF16), w_attn_out.astype(BF16),
                            w_out.astype(BF16), row(g_ffn), wr, br)

    e = rt[0:TOP_K].astype(jnp.int32)
    rank = rt[4:4 + TOP_K].astype(jnp.int32)
    counts = cnt[0, R_EXP0:R_EXP0 + N_EXPERTS].astype(jnp.int32)
    padded = (counts + EXPERT_ROWS - 1) // EXPERT_ROWS * EXPERT_ROWS
    pad_end = jnp.cumsum(padded)
    pad_start = pad_end - padded
    dest = (pad_start[e] + rank).reshape(T * TOP_K).astype(jnp.int32)
    n_blocks = (T * TOP_K + N_EXPERTS * (EXPERT_ROWS - 1) + EXPERT_ROWS - 1) // EXPERT_ROWS
    blk_start = jnp.arange(n_blocks, dtype=jnp.int32) * EXPERT_ROWS
    blk_expert = jnp.minimum(
        jnp.sum((pad_end[None, :] <= blk_start[:, None]).astype(jnp.int32), axis=1), N_EXPERTS - 1)
    n_used = (pad_end[-1:] // EXPERT_ROWS).astype(jnp.int32)

    fill = jnp.concatenate([pad_start + counts, padded - counts, n_used]).astype(jnp.int32)
    xs = _dispatch(dest, fill, v, n_blocks * EXPERT_ROWS)
    y = _experts(blk_expert, n_used, xs, w1, w3, w2)
    return x1, y, dest, ri


def kernel(x, g_mix, w_in, conv_w, conv_b, w_rg_a, b_rg_a, w_rg_x, b_rg_x, lru_lambda, b_forget, w_lru_out, w_attn_out, w_out, g_ffn, w_route_group, b_route_group, w_route_expert, b_route_expert, w_exp_gate, w_exp_up, w_exp_down, g_final):
    batch, seq, _ = x.shape
    depth = g_mix.shape[0]
    x2 = x.reshape(batch * seq, D_MODEL)
    for l in range(depth):
        x1, y, dest, ri = _layer(
            x2, batch, seq, g_mix[l], w_in[l], conv_w[l], conv_b[l], w_rg_a[l], b_rg_a[l], w_rg_x[l],
            b_rg_x[l], lru_lambda[l], b_forget[l], w_lru_out[l], w_attn_out[l], w_out[l], g_ffn[l],
            w_route_group[l], b_route_group[l], w_route_expert[l], b_route_expert[l],
            w_exp_gate[l], w_exp_up[l], w_exp_down[l])
        x2 = _combine(dest, y, x1, ri, g_final.reshape(1, -1).astype(F32), final_norm=l == depth - 1)
    return x2.reshape(batch, seq, D_MODEL)
```

```python
import functools

import jax
import jax.numpy as jnp
from jax import lax
from jax.experimental import pallas as pl
from jax.experimental.pallas import tpu as pltpu

F32 = jnp.float32
BF16 = jnp.bfloat16

D_MODEL = 1024
LRU_BLOCK_W = 256
LRU_BLOCKS = D_MODEL // LRU_BLOCK_W
CONV_W = 4
LRU_C = 8.0
N_HEADS = 8
HEAD_DIM = D_MODEL // N_HEADS
N_GROUPS = 4
EXPERTS_PER_GROUP = 8
N_EXPERTS = N_GROUPS * EXPERTS_PER_GROUP
TOP_K = 2
D_EXPERT = D_MODEL // 2
EPS = 1e-6

LANES = 128
SUBLANES = 8
ROW_CHUNKS = D_MODEL // LANES
VMEM_LIMIT = 48 * 1024 * 1024

ZC_XR, ZC_GR, ZC_Q, ZC_K, ZC_V, ZC_ML, ZC_MA = 0, 8, 16, 24, 32, 40, 48
Z_WIDTH = 7 * D_MODEL

R_EXP0 = N_GROUPS

EXPERT_ROWS = 256


def _cparams(sem):
    return pltpu.CompilerParams(dimension_semantics=sem, vmem_limit_bytes=VMEM_LIMIT)


LOG2E = 1.4426950408889634
Q_PRESCALE = HEAD_DIM ** -0.5 * LOG2E


def _inproj_body(x_ref, g_ref, w_ref, wfl_ref, z_ref, fl_ref, u_ref, *, q_block):
    j = pl.program_id(1)

    @pl.when(j == 0)
    def _():
        x = x_ref[...]
        ms = jnp.mean(x * x, axis=-1, keepdims=True)
        u = (x * lax.rsqrt(ms + EPS) * g_ref[...]).astype(BF16)
        u_ref[...] = u
        fl_ref[...] = jnp.dot(u, wfl_ref[...], preferred_element_type=F32)

    @pl.when(j == q_block)
    def _():
        acc = jnp.dot(u_ref[...], w_ref[...], preferred_element_type=F32)
        z_ref[...] = (acc * Q_PRESCALE).astype(BF16)

    @pl.when(j != q_block)
    def _():
        z_ref[...] = jnp.dot(u_ref[...], w_ref[...], preferred_element_type=F32).astype(BF16)


def _inproj(x2, g, w_main, w_fl, *, tm=1024, tn=D_MODEL):
    T = x2.shape[0]
    return pl.pallas_call(
        functools.partial(_inproj_body, q_block=ZC_Q * LANES // tn),
        grid=(T // tm, Z_WIDTH // tn),
        in_specs=[
            pl.BlockSpec((tm, D_MODEL), lambda i, j: (i, 0)),
            pl.BlockSpec((1, D_MODEL), lambda i, j: (0, 0)),
            pl.BlockSpec((D_MODEL, tn), lambda i, j: (0, j)),
            pl.BlockSpec((D_MODEL, LANES), lambda i, j: (0, 0)),
        ],
        out_specs=[
            pl.BlockSpec((tm, tn), lambda i, j: (i, j)),
            pl.BlockSpec((tm, LANES), lambda i, j: (i, 0)),
        ],
        out_shape=[
            jax.ShapeDtypeStruct((T, Z_WIDTH), BF16),
            jax.ShapeDtypeStruct((T, LANES), F32),
        ],
        scratch_shapes=[pltpu.VMEM((tm, D_MODEL), BF16)],
        compiler_params=_cparams(("arbitrary", "arbitrary")),
        name="inproj",
    )(x2, g, w_main, w_fl)


C_TERMS = 3


def _log_sigmoid(z):
    return jnp.minimum(z, 0.0) - jnp.log1p(jnp.exp(-jnp.abs(z)))


def _forget_body(fl_ref, b_ref, c_ref, *, seq):
    lf = _log_sigmoid(fl_ref[...] + b_ref[...])
    row = lax.broadcasted_iota(jnp.int32, lf.shape, 0)
    lane = lax.broadcasted_iota(jnp.int32, lf.shape, 1)
    c = lf
    k = 1
    while k < seq:
        c = c + jnp.where(row >= k, pltpu.roll(c, k, axis=0), 0.0)
        k *= 2
    rem = c * LOG2E
    out = jnp.zeros(lf.shape, F32)
    for n in range(C_TERMS):
        t = rem.astype(BF16).astype(F32)
        rem = rem - t
        shifted = t if n == 0 else pltpu.roll(t, n * N_HEADS, axis=1)
        out = jnp.where((lane >= n * N_HEADS) & (lane < (n + 1) * N_HEADS), shifted, out)
    c_ref[...] = out.astype(BF16)


def _forget(fl, b_pad, *, batch, seq):
    return pl.pallas_call(
        functools.partial(_forget_body, seq=seq),
        grid=(batch,),
        in_specs=[
            pl.BlockSpec((seq, LANES), lambda b: (b, 0)),
            pl.BlockSpec((1, LANES), lambda b: (0, 0)),
        ],
        out_specs=pl.BlockSpec((seq, LANES), lambda b: (b, 0)),
        out_shape=jax.ShapeDtypeStruct((batch * seq, LANES), BF16),
        compiler_params=_cparams(("arbitrary",)),
        name="forget",
    )(fl, b_pad)


ATTN_HEADS_PER_STEP = 2


def _attn_body(q_ref, k_ref, v_ref, kc_ref, o_ref, *, tq):
    i = pl.program_id(2)
    lane = lax.broadcasted_iota(jnp.int32, (tq, LANES), 1)
    heads = range(ATTN_HEADS_PER_STEP)
    cols =[slice(g * HEAD_DIM, (g + 1) * HEAD_DIM) for g in heads]
    qs = []
    for g in heads:
        h = pl.program_id(1) * ATTN_HEADS_PER_STEP + g
        mine = (lane < C_TERMS * N_HEADS) & ((lane & (N_HEADS - 1)) == h)
        qc = jnp.where(mine, -1.0, 0.0).astype(BF16)
        qs.append(jnp.concatenate([q_ref[:, cols[g]], qc], axis=1))

    def step(j, carry, masked):
        off = pl.multiple_of(j * tq, tq)
        out = []
        for g in heads:
            m, l, acc = carry[g]
            kj = jnp.concatenate([k_ref[pl.ds(off, tq), cols[g]], kc_ref[pl.ds(off, tq), :]], axis=1)
            s = lax.dot_general(qs[g], kj, (((1,), (1,)), ((), ())), preferred_element_type=F32)
            if masked:
                r = lax.broadcasted_iota(jnp.int32, s.shape, 0)
                cidx = lax.broadcasted_iota(jnp.int32, s.shape, 1)
                s = jnp.where(cidx <= r, s, -jnp.inf)
            m_new = jnp.maximum(m, jnp.max(s, axis=-1, keepdims=True))
            p = jnp.exp2(s - m_new)
            alpha = jnp.exp2(m - m_new)
            l = alpha * l + jnp.sum(p, axis=-1, keepdims=True)
            acc = alpha * acc + jnp.dot(p.astype(BF16), v_ref[pl.ds(off, tq), cols[g]],
                                        preferred_element_type=F32)
            out.append((m_new, l, acc))
        return tuple(out)

    init = tuple((jnp.full((tq, 1), -jnp.inf, F32), jnp.zeros((tq, 1), F32), jnp.zeros((tq, HEAD_DIM), F32))
                 for _ in heads)
    carry = lax.fori_loop(0, i, lambda j, c: step(j, c, False), init)
    carry = step(i, carry, True)
    for g in heads:
        _, l, acc = carry[g]
        o_ref[:, cols[g]] = (acc / l).astype(BF16)


def _attention(z, caug, *, batch, seq, tq=512):
    nq = seq // tq
    T = batch * seq
    G = ATTN_HEADS_PER_STEP
    W = G * HEAD_DIM
    return pl.pallas_call(
        functools.partial(_attn_body, tq=tq),
        grid=(batch, N_HEADS // G, nq),
        in_specs=[
            pl.BlockSpec((tq, W), lambda b, h, i: (b * nq + i, ZC_Q // G + h)),
            pl.BlockSpec((seq, W), lambda b, h, i: (b, ZC_K // G + h)),
            pl.BlockSpec((seq, W), lambda b, h, i: (b, ZC_V // G + h)),
            pl.BlockSpec((seq, LANES), lambda b, h, i: (b, 0)),
        ],
        out_specs=pl.BlockSpec((tq, W), lambda b, h, i: (b * nq + i, h)),
        out_shape=jax.ShapeDtypeStruct((T, D_MODEL), BF16),
        compiler_params=_cparams(("arbitrary", "arbitrary", "arbitrary")),
        name="attn",
    )(z, z, z, caug)


def _lru_body(xr_ref, gr_ref, cw_ref, cb_ref, wa_ref, wx_ref, ba_ref, bx_ref, lam_ref,
              hg_ref, xbuf, hcar, *, ts):
    i = pl.program_id(1)

    @pl.when(i == 0)
    def _():
        xbuf[0:SUBLANES, :] = jnp.zeros((SUBLANES, D_MODEL), F32)
        hcar[...] = jnp.zeros_like(hcar)

    @pl.when(i > 0)
    def _():
        xbuf[0:SUBLANES, :] = xbuf[ts:ts + SUBLANES, :]

    xbuf[SUBLANES:ts + SUBLANES, :] = xr_ref[...].astype(F32)

    base = SUBLANES - (CONV_W - 1)
    xc = cw_ref[0:1, :] * xbuf[base:base + ts, :]
    for k in range(1, CONV_W):
        xc = xc + cw_ref[k:k + 1, :] * xbuf[base + k:base + k + ts, :]
    xc = xc + cb_ref[...]

    xcb = xc.astype(BF16)
    ra = jnp.concatenate(
        [jnp.dot(xcb[:, n * LRU_BLOCK_W:(n + 1) * LRU_BLOCK_W], wa_ref[n], preferred_element_type=F32)
         for n in range(LRU_BLOCKS)], axis=-1)
    rx = jnp.concatenate(
        [jnp.dot(xcb[:, n * LRU_BLOCK_W:(n + 1) * LRU_BLOCK_W], wx_ref[n], preferred_element_type=F32)
         for n in range(LRU_BLOCKS)], axis=-1)
    r = jax.nn.sigmoid(ra + ba_ref[...])
    ig = jax.nn.sigmoid(rx + bx_ref[...])
    nlam = -lam_ref[...]
    softplus = jnp.maximum(nlam, 0.0) + jnp.log1p(jnp.exp(-jnp.abs(nlam)))
    log_a = (-LRU_C * r) * softplus
    a = jnp.exp(log_a)
    th = jnp.tanh(log_a)
    mult = jnp.sqrt(-2.0 * th / (1.0 - th))
    b = mult * ig * xc

    row = lax.broadcasted_iota(jnp.int32, (SUBLANES, D_MODEL), 0)
    keeps = [(k, row >= k) for k in (1, 2, 4)]
    hprev = jnp.broadcast_to(hcar[...], (SUBLANES, D_MODEL))
    pieces = []
    for j in range(ts // SUBLANES):
        aj = a[j * SUBLANES:(j + 1) * SUBLANES, :]
        bj = b[j * SUBLANES:(j + 1) * SUBLANES, :]
        for k, keep in keeps:
            a_sh = jnp.where(keep, pltpu.roll(aj, k, axis=0), 1.0)
            b_sh = jnp.where(keep, pltpu.roll(bj, k, axis=0), 0.0)
            bj = aj * b_sh + bj
            aj = aj * a_sh
        hj = bj + aj * hprev
        hprev = jnp.broadcast_to(hj[SUBLANES - 1:SUBLANES, :], (SUBLANES, D_MODEL))
        pieces.append(hj)
    h = jnp.concatenate(pieces, axis=0)
    hcar[...] = h[ts - 1:ts, :]

    hg_ref[...] = (h * jax.nn.gelu(gr_ref[...].astype(F32))).astype(BF16)


def _lru(z, conv_w, conv_b, wa, wx, ba, bx, lam, *, batch, seq, ts=256):
    ns = seq // ts
    T = batch * seq
    full = lambda shape: pl.BlockSpec(shape, lambda b, i: (0,) * len(shape))
    return pl.pallas_call(
        functools.partial(_lru_body, ts=ts),
        grid=(batch, ns),
        in_specs=[
            pl.BlockSpec((ts, D_MODEL), lambda b, i: (b * ns + i, ZC_XR // SUBLANES)),
            pl.BlockSpec((ts, D_MODEL), lambda b, i: (b * ns + i, ZC_GR // SUBLANES)),
            full((CONV_W, D_MODEL)),
            full((1, D_MODEL)),
            full((LRU_BLOCKS, LRU_BLOCK_W, LRU_BLOCK_W)),
            full((LRU_BLOCKS, LRU_BLOCK_W, LRU_BLOCK_W)),
            full((1, D_MODEL)),
            full((1, D_MODEL)),
            full((1, D_MODEL)),
        ],
        out_specs=pl.BlockSpec((ts, D_MODEL), lambda b, i: (b * ns + i, 0)),
        out_shape=jax.ShapeDtypeStruct((T, D_MODEL), BF16),
        scratch_shapes=[pltpu.VMEM((ts + SUBLANES, D_MODEL), F32), pltpu.VMEM((1, D_MODEL), F32)],
        compiler_params=_cparams(("arbitrary", "arbitrary")),
        name="lru",
    )(z, z, conv_w, conv_b, wa, wx, ba, bx, lam)


MERGE_SUBTILES = 1


def _merge_body(hg_ref, o_ref, ml_ref, ma_ref, x_ref, wl_ref, wat_ref, wo_ref, g_ref, wr_ref, br_ref,
                x1_ref, v_ref, ri_ref, rt_ref, cnt_ref, carry_ref, *, tm):
    @pl.when(pl.program_id(0) == 0)
    def _():
        carry_ref[...] = jnp.zeros_like(carry_ref)

    ts = tm // MERGE_SUBTILES
    lane = lax.broadcasted_iota(jnp.int32, (ts, LANES), 1)
    rr = lax.broadcasted_iota(jnp.int32, (ts, ts), 0)
    cc = lax.broadcasted_iota(jnp.int32, (ts, ts), 1)
    tri = jnp.where(cc < rr, 1.0, 0.0).astype(BF16)
    ninf = -jnp.inf
    big = jnp.int32(1 << 20)
    carry = carry_ref[...]

    for sub in range(MERGE_SUBTILES):
        rows = slice(sub * ts, (sub + 1) * ts)
        yl = jnp.dot(hg_ref[rows, :], wl_ref[...], preferred_element_type=F32)
        ya = jnp.dot(o_ref[rows, :], wat_ref[...], preferred_element_type=F32)
        merged = (jax.nn.sigmoid(ml_ref[rows, :].astype(F32)) * yl
                  + jax.nn.sigmoid(ma_ref[rows, :].astype(F32)) * ya)
        x1 = x_ref[rows, :] + jnp.dot(merged.astype(BF16), wo_ref[...], preferred_element_type=F32)
        x1_ref[rows, :] = x1
        ms = jnp.mean(x1 * x1, axis=-1, keepdims=True)
        v = x1 * lax.rsqrt(ms + EPS) * g_ref[...]
        v_ref[rows, :] = v

        logits = jnp.dot(v.astype(BF16), wr_ref[...], preferred_element_type=F32) + br_ref[...]

        gl = jnp.where(lane < N_GROUPS, logits, ninf)
        gmax = jnp.max(gl, axis=-1, keepdims=True)
        gsel = jnp.min(jnp.where(gl == gmax, lane, big), axis=-1, keepdims=True)
        pg = 1.0 / jnp.sum(jnp.exp(gl - gmax), axis=-1, keepdims=True)

        lo = R_EXP0 + gsel * EXPERTS_PER_GROUP
        el = jnp.where(lane >= lo, jnp.where(lane < lo + EXPERTS_PER_GROUP, logits, ninf), ninf)
        v1 = jnp.max(el, axis=-1, keepdims=True)
        i1 = jnp.min(jnp.where(el == v1, lane, big), axis=-1, keepdims=True)
        el2 = jnp.where(lane == i1, ninf, el)
        v2 = jnp.max(el2, axis=-1, keepdims=True)
        i2 = jnp.min(jnp.where(el2 == v2, lane, big), axis=-1, keepdims=True)
        e21 = jnp.exp(v2 - v1)
        p1 = 1.0 / (1.0 + e21)
        w0 = pg * p1
        w1 = pg * (e21 * p1)

        hit0 = lane == i1
        hit1 = lane == i2
        onehot = jnp.where(hit0, 1.0, jnp.where(hit1, 1.0, 0.0))
        before = jnp.dot(tri, onehot.astype(BF16), preferred_element_type=F32) + carry
        rank0 = jnp.sum(jnp.where(hit0, before, 0.0), axis=-1, keepdims=True)
        rank1 = jnp.sum(jnp.where(hit1, before, 0.0), axis=-1, keepdims=True)
        carry = carry + jnp.sum(onehot, axis=0, keepdims=True)

        e0 = (i1 - R_EXP0).astype(F32)
        e1 = (i2 - R_EXP0).astype(F32)
        ri = jnp.where(lane == 0, e0,
             jnp.where(lane == 1, e1,
             jnp.where(lane == 2, w0,
             jnp.where(lane == 3, w1,
             jnp.where(lane == 4, rank0,
             jnp.where(lane == 5, rank1, 0.0))))))
        ri_ref[rows, :] = ri
        rt_ref[:, rows] = ri.T[:SUBLANES, :]

    carry_ref[...] = carry
    cnt_ref[...] = carry


def _merge(hg, o, z, x2, wl, wat, wo, g, wr, br, *, tm=512):
    T = x2.shape[0]
    full = lambda shape: pl.BlockSpec(shape, lambda i: (0,) * len(shape))
    return pl.pallas_call(
        functools.partial(_merge_body, tm=tm),
        grid=(T // tm,),
        in_specs=[
            pl.BlockSpec((tm, D_MODEL), lambda i: (i, 0)),
            pl.BlockSpec((tm, D_MODEL), lambda i: (i, 0)),
            pl.BlockSpec((tm, D_MODEL), lambda i: (i, ZC_ML // SUBLANES)),
            pl.BlockSpec((tm, D_MODEL), lambda i: (i, ZC_MA // SUBLANES)),
            pl.BlockSpec((tm, D_MODEL), lambda i: (i, 0)),
            full((D_MODEL, D_MODEL)),
            full((D_MODEL, D_MODEL)),
            full((D_MODEL, D_MODEL)),
            full((1, D_MODEL)),
            full((D_MODEL, LANES)),
            full((1, LANES)),
        ],
        out_specs=[
            pl.BlockSpec((tm, D_MODEL), lambda i: (i, 0)),
            pl.BlockSpec((tm, D_MODEL), lambda i: (i, 0)),
            pl.BlockSpec((tm, LANES), lambda i: (i, 0)),
            pl.BlockSpec((SUBLANES, tm), lambda i: (0, i)),
            pl.BlockSpec((1, LANES), lambda i: (0, 0)),
        ],
        out_shape=[
            jax.ShapeDtypeStruct((T, D_MODEL), F32),
            jax.ShapeDtypeStruct((T, D_MODEL), F32),
            jax.ShapeDtypeStruct((T, LANES), F32),
            jax.ShapeDtypeStruct((SUBLANES, T), F32),
            jax.ShapeDtypeStruct((1, LANES), F32),
        ],
        scratch_shapes=[pltpu.VMEM((1, LANES), F32)],
        compiler_params=_cparams(("arbitrary",)),
        name="merge",
    )(hg, o, z, z, x2, wl, wat, wo, g, wr, br)


def _row_slice(r):
    return pl.ds(pl.multiple_of(r * ROW_CHUNKS, ROW_CHUNKS), ROW_CHUNKS)


def _dispatch_body(dest_ref, fill_ref, v_ref, xs_ref, rows, zrow, zblock, sem, zsem, *, td, nsteps, n_blocks):
    i = pl.program_id(0)

    def copy(r, k):
        d = dest_ref[k * (nsteps * td) + i * td + r]
        return pltpu.make_async_copy(rows.at[_row_slice(r)], xs_ref.at[_row_slice(d)], sem)

    def for_each_row(fn):
        def f(r, carry):
            for k in range(TOP_K):
                fn(copy(r, k), k)
            return carry
        lax.fori_loop(0, td, f, 0)

    def zero_copy(e, r):
        return pltpu.make_async_copy(zrow, xs_ref.at[_row_slice(fill_ref[e] + r)], zsem)

    def for_each_padding_row(fn):
        def per_expert(e, carry):
            lax.fori_loop(0, fill_ref[N_EXPERTS + e], lambda r, c: (fn(zero_copy(e, r)), c)[1], 0)
            return carry
        lax.fori_loop(0, N_EXPERTS, per_expert, 0)

    def zero_block_copy(blk):
        start = pl.multiple_of(blk * (EXPERT_ROWS * ROW_CHUNKS), EXPERT_ROWS * ROW_CHUNKS)
        return pltpu.make_async_copy(zblock, xs_ref.at[pl.ds(start, EXPERT_ROWS * ROW_CHUNKS)], zsem)

    def for_each_unused_block(fn):
        lax.fori_loop(fill_ref[2 * N_EXPERTS], n_blocks, lambda blk, c: (fn(zero_block_copy(blk)), c)[1], 0)

    @pl.when(i == 0)
    def _():
        zrow[...] = jnp.zeros_like(zrow)
        zblock[...] = jnp.zeros_like(zblock)
        for_each_padding_row(lambda cp: cp.start())
        for_each_unused_block(lambda cp: cp.start())

    for s in range(ROW_CHUNKS):
        rows[pl.ds(s, td, stride=ROW_CHUNKS), :] = v_ref[:, s * LANES:(s + 1) * LANES]
    for_each_row(lambda cp, k: cp.start(priority=k))
    for_each_row(lambda cp, k: cp.wait())

    @pl.when(i == nsteps - 1)
    def _():
        for_each_padding_row(lambda cp: cp.wait())
        for_each_unused_block(lambda cp: cp.wait())


def _dispatch(dest, fill, v, n_rows, *, td=256):
    T = v.shape[0]
    nsteps = T // td
    n_blocks = n_rows // EXPERT_ROWS
    grid_spec = pltpu.PrefetchScalarGridSpec(
        num_scalar_prefetch=2,
        grid=(nsteps,),
        in_specs=[pl.BlockSpec((td, D_MODEL), lambda i, dest, fill: (i, 0))],
        out_specs=pl.BlockSpec(memory_space=pl.ANY),
        scratch_shapes=[pltpu.VMEM((td * ROW_CHUNKS, LANES), F32),
                        pltpu.VMEM((ROW_CHUNKS, LANES), F32),
                        pltpu.VMEM((EXPERT_ROWS * ROW_CHUNKS, LANES), F32),
                        pltpu.SemaphoreType.DMA(()),
                        pltpu.SemaphoreType.DMA(())],
    )
    return pl.pallas_call(
        functools.partial(_dispatch_body, td=td, nsteps=nsteps, n_blocks=n_blocks),
        grid_spec=grid_spec,
        out_shape=jax.ShapeDtypeStruct((n_rows * ROW_CHUNKS, LANES), F32),
        compiler_params=_cparams(("arbitrary",)),
        name="dispatch",
    )(dest, fill, v)


def _expert_body(be_ref, nu_ref, x_ref, w1_ref, w3_ref, w2_ref, y_ref, w13b, w2b, *, rows):
    i = pl.program_id(0)

    @pl.when((i == 0) | (be_ref[i] != be_ref[jnp.maximum(i - 1, 0)]))
    def _():
        w13b[:, :D_EXPERT] = w1_ref[0].astype(BF16)
        w13b[:, D_EXPERT:] = w3_ref[0].astype(BF16)
        w2b[...] = w2_ref[0].astype(BF16)

    @pl.when(i < nu_ref[0])
    def _():
        xb = jnp.concatenate(
            [x_ref[pl.ds(s, rows, stride=ROW_CHUNKS), :].astype(BF16) for s in range(ROW_CHUNKS)], axis=-1)
        gu = jnp.dot(xb, w13b[...], preferred_element_type=F32)
        hb = (jax.nn.silu(gu[:, :D_EXPERT]) * gu[:, D_EXPERT:]).astype(BF16)
        y = jnp.dot(hb, w2b[...], preferred_element_type=F32)
        for s in range(ROW_CHUNKS):
            y_ref[pl.ds(s, rows, stride=ROW_CHUNKS), :] = y[:, s * LANES:(s + 1) * LANES]

    @pl.when(i >= nu_ref[0])
    def _():
        y_ref[...] = jnp.zeros_like(y_ref)


def _experts(blk_expert, n_used, xs, w1, w3, w2, *, rows=EXPERT_ROWS):
    n_blocks = xs.shape[0] // (rows * ROW_CHUNKS)
    grid_spec = pltpu.PrefetchScalarGridSpec(
        num_scalar_prefetch=2,
        grid=(n_blocks,),
        in_specs=[
            pl.BlockSpec((rows * ROW_CHUNKS, LANES), lambda i, be, nu: (jnp.minimum(i, nu[0] - 1), 0)),
            pl.BlockSpec((1, D_MODEL, D_EXPERT), lambda i, be, nu: (be[i], 0, 0)),
            pl.BlockSpec((1, D_MODEL, D_EXPERT), lambda i, be, nu: (be[i], 0, 0)),
            pl.BlockSpec((1, D_EXPERT, D_MODEL), lambda i, be, nu: (be[i], 0, 0)),
        ],
        out_specs=pl.BlockSpec((rows * ROW_CHUNKS, LANES), lambda i, be, nu: (i, 0)),
        scratch_shapes=[pltpu.VMEM((D_MODEL, 2 * D_EXPERT), BF16), pltpu.VMEM((D_EXPERT, D_MODEL), BF16)],
    )
    return pl.pallas_call(
        functools.partial(_expert_body, rows=rows),
        grid_spec=grid_spec,
        out_shape=jax.ShapeDtypeStruct(xs.shape, F32),
        compiler_params=_cparams(("arbitrary",)),
        name="experts",
    )(blk_expert, n_used, xs, w1, w3, w2)


def _combine_body(dest_ref, y_ref, x1_ref, ri_ref, g_ref, out_ref, buf, sem, *, tc, nsteps, final_norm):
    i = pl.program_id(0)

    def copy(r, k):
        d = dest_ref[k * (nsteps * tc) + i * tc + r]
        return pltpu.make_async_copy(y_ref.at[_row_slice(d)], buf.at[k, _row_slice(r)], sem)

    def for_each_row(fn):
        def f(r, carry):
            for k in range(TOP_K):
                fn(copy(r, k), k)
            return carry
        lax.fori_loop(0, tc, f, 0)

    for_each_row(lambda cp, k: cp.start(priority=k))
    for_each_row(lambda cp, k: cp.wait())

    def rows_of(k):
        return jnp.concatenate(
            [buf[k, pl.ds(s, tc, stride=ROW_CHUNKS), :] for s in range(ROW_CHUNKS)], axis=-1)

    ri = ri_ref[...]
    w0 = ri[:, 2:3]
    w1 = ri[:, 3:4]
    x2 = x1_ref[...] + (w0 * rows_of(0) + w1 * rows_of(1))
    if final_norm:
        ms = jnp.mean(x2 * x2, axis=-1, keepdims=True)
        x2 = x2 * lax.rsqrt(ms + EPS) * g_ref[...]
    out_ref[...] = x2


def _combine(dest, y, x1, ri, g, *, final_norm, tc=256):
    T = x1.shape[0]
    nsteps = T // tc
    grid_spec = pltpu.PrefetchScalarGridSpec(
        num_scalar_prefetch=1,
        grid=(nsteps,),
        in_specs=[
            pl.BlockSpec(memory_space=pl.ANY),
            pl.BlockSpec((tc, D_MODEL), lambda i, dest: (i, 0)),
            pl.BlockSpec((tc, LANES), lambda i, dest: (i, 0)),
            pl.BlockSpec((1, D_MODEL), lambda i, dest: (0, 0)),
        ],
        out_specs=pl.BlockSpec((tc, D_MODEL), lambda i, dest: (i, 0)),
        scratch_shapes=[pltpu.VMEM((TOP_K, tc * ROW_CHUNKS, LANES), F32),
                        pltpu.SemaphoreType.DMA(())],
    )
    return pl.pallas_call(
        functools.partial(_combine_body, tc=tc, nsteps=nsteps, final_norm=final_norm),
        grid_spec=grid_spec,
        out_shape=jax.ShapeDtypeStruct((T, D_MODEL), F32),
        compiler_params=_cparams(("arbitrary",)),
        name="combine",
    )(dest, y, x1, ri, g)


def _pad_lanes(a, width=LANES):
    return jnp.pad(a, ((0, 0), (0, width - a.shape[-1])))


def _layer(x2, batch, seq, g_mix, w_in, conv_w, conv_b, w_rg_a, b_rg_a, w_rg_x, b_rg_x, lam, b_forget,
           w_lru_out, w_attn_out, w_out, g_ffn, w_rgrp, b_rgrp, w_rexp, b_rexp, w1, w3, w2):
    T = batch * seq
    row = lambda a: a.reshape(1, -1).astype(F32)

    fl0 = 5 * D_MODEL
    w_main = jnp.concatenate([w_in[:, :fl0], w_in[:, fl0 + N_HEADS:]], axis=1).astype(BF16)
    w_fl = _pad_lanes(w_in[:, fl0:fl0 + N_HEADS]).astype(BF16)
    z, fl = _inproj(x2, row(g_mix), w_main, w_fl)

    caug = _forget(fl, _pad_lanes(row(b_forget)), batch=batch, seq=seq)
    o = _attention(z, caug, batch=batch, seq=seq)

    hg = _lru(z, conv_w.astype(F32), row(conv_b), w_rg_a.astype(BF16), w_rg_x.astype(BF16),
              row(b_rg_a), row(b_rg_x), row(lam), batch=batch, seq=seq)

    wr = _pad_lanes(jnp.concatenate([w_rgrp, w_rexp], axis=1)).astype(BF16)
    br = _pad_lanes(jnp.concatenate([row(b_rgrp), row(b_rexp)], axis=1))
    x1, v, ri, rt, cnt = _merge(hg, o, z, x2, w_lru_out.astype(BF16), w_attn_out.astype(BF16),
                            w_out.astype(BF16), row(g_ffn), wr, br)

    e = rt[0:TOP_K].astype(jnp.int32)
    rank = rt[4:4 + TOP_K].astype(jnp.int32)
    counts = cnt[0, R_EXP0:R_EXP0 + N_EXPERTS].astype(jnp.int32)
    padded = (counts + EXPERT_ROWS - 1) // EXPERT_ROWS * EXPERT_ROWS
    pad_end = jnp.cumsum(padded)
    pad_start = pad_end - padded
    seg_start = jnp.zeros_like(e)
    for j in range(N_EXPERTS):
        seg_start = jnp.where(e == j, pad_start[j], seg_start)
    dest = (seg_start + rank).reshape(T * TOP_K).astype(jnp.int32)
    n_blocks = (T * TOP_K + N_EXPERTS * (EXPERT_ROWS - 1) + EXPERT_ROWS - 1) // EXPERT_ROWS
    blk_start = jnp.arange(n_blocks, dtype=jnp.int32) * EXPERT_ROWS
    blk_expert = jnp.minimum(
        jnp.sum((pad_end[None, :] <= blk_start[:, None]).astype(jnp.int32), axis=1), N_EXPERTS - 1)
    n_used = (pad_end[-1:] // EXPERT_ROWS).astype(jnp.int32)

    fill = jnp.concatenate([pad_start + counts, padded - counts, n_used]).astype(jnp.int32)
    xs = _dispatch(dest, fill, v, n_blocks * EXPERT_ROWS)
    y = _experts(blk_expert, n_used, xs, w1, w3, w2)
    return x1, y, dest, ri


def kernel(x, g_mix, w_in, conv_w, conv_b, w_rg_a, b_rg_a, w_rg_x, b_rg_x, lru_lambda, b_forget, w_lru_out, w_attn_out, w_out, g_ffn, w_route_group, b_route_group, w_route_expert, b_route_expert, w_exp_gate, w_exp_up, w_exp_down, g_final):
    batch, seq, _ = x.shape
    depth = g_mix.shape[0]
    x2 = x.reshape(batch * seq, D_MODEL)
    for l in range(depth):
        x1, y, dest, ri = _layer(
            x2, batch, seq, g_mix[l], w_in[l], conv_w[l], conv_b[l], w_rg_a[l], b_rg_a[l], w_rg_x[l],
            b_rg_x[l], lru_lambda[l], b_forget[l], w_lru_out[l], w_attn_out[l], w_out[l], g_ffn[l],
            w_route_group[l], b_route_group[l], w_route_expert[l], b_route_expert[l],
            w_exp_gate[l], w_exp_up[l], w_exp_down[l])
        x2 = _combine(dest, y, x1, ri, g_final.reshape(1, -1).astype(F32), final_norm=l == depth - 1)
    return x2.reshape(batch, seq, D_MODEL)
```

```python
import functools

import jax
import jax.numpy as jnp
from jax import lax
from jax.experimental import pallas as pl
from jax.experimental.pallas import tpu as pltpu
from jax.experimental.pallas import tpu_sc as plsc

F32 = jnp.float32
BF16 = jnp.bfloat16

D_MODEL = 1024
LRU_BLOCK_W = 256
LRU_BLOCKS = D_MODEL // LRU_BLOCK_W
CONV_W = 4
LRU_C = 8.0
N_HEADS = 8
HEAD_DIM = D_MODEL // N_HEADS
N_GROUPS = 4
EXPERTS_PER_GROUP = 8
N_EXPERTS = N_GROUPS * EXPERTS_PER_GROUP
TOP_K = 2
D_EXPERT = D_MODEL // 2
EPS = 1e-6

LANES = 128
SUBLANES = 8
ROW_CHUNKS = D_MODEL // LANES
VMEM_LIMIT = 48 * 1024 * 1024

ZC_XR, ZC_GR, ZC_Q, ZC_K, ZC_V, ZC_ML, ZC_MA = 0, 8, 16, 24, 32, 40, 48
Z_WIDTH = 7 * D_MODEL

R_EXP0 = N_GROUPS

EXPERT_ROWS = 256


def _cparams(sem):
    return pltpu.CompilerParams(dimension_semantics=sem, vmem_limit_bytes=VMEM_LIMIT)


LOG2E = 1.4426950408889634
Q_PRESCALE = HEAD_DIM ** -0.5 * LOG2E


def _inproj_body(x_ref, g_ref, w_ref, wfl_ref, z_ref, fl_ref, u_ref, *, q_block):
    j = pl.program_id(1)

    @pl.when(j == 0)
    def _():
        x = x_ref[...]
        ms = jnp.mean(x * x, axis=-1, keepdims=True)
        u = (x * lax.rsqrt(ms + EPS) * g_ref[...]).astype(BF16)
        u_ref[...] = u
        fl_ref[...] = jnp.dot(u, wfl_ref[...], preferred_element_type=F32)

    @pl.when(j == q_block)
    def _():
        acc = jnp.dot(u_ref[...], w_ref[...], preferred_element_type=F32)
        z_ref[...] = (acc * Q_PRESCALE).astype(BF16)

    @pl.when(j != q_block)
    def _():
        z_ref[...] = jnp.dot(u_ref[...], w_ref[...], preferred_element_type=F32).astype(BF16)


def _inproj(x2, g, w_main, w_fl, *, tm=1024, tn=D_MODEL):
    T = x2.shape[0]
    return pl.pallas_call(
        functools.partial(_inproj_body, q_block=ZC_Q * LANES // tn),
        grid=(T // tm, Z_WIDTH // tn),
        in_specs=[
            pl.BlockSpec((tm, D_MODEL), lambda i, j: (i, 0)),
            pl.BlockSpec((1, D_MODEL), lambda i, j: (0, 0)),
            pl.BlockSpec((D_MODEL, tn), lambda i, j: (0, j)),
            pl.BlockSpec((D_MODEL, LANES), lambda i, j: (0, 0)),
        ],
        out_specs=[
            pl.BlockSpec((tm, tn), lambda i, j: (i, j)),
            pl.BlockSpec((tm, LANES), lambda i, j: (i, 0)),
        ],
        out_shape=[
            jax.ShapeDtypeStruct((T, Z_WIDTH), BF16),
            jax.ShapeDtypeStruct((T, LANES), F32),
        ],
        scratch_shapes=[pltpu.VMEM((tm, D_MODEL), BF16)],
        compiler_params=_cparams(("arbitrary", "arbitrary")),
        name="inproj",
    )(x2, g, w_main, w_fl)


C_TERMS = 3


def _log_sigmoid(z):
    return jnp.minimum(z, 0.0) - jnp.log1p(jnp.exp(-jnp.abs(z)))


def _forget_body(fl_ref, b_ref, c_ref, *, seq):
    lf = _log_sigmoid(fl_ref[...] + b_ref[...])
    row = lax.broadcasted_iota(jnp.int32, lf.shape, 0)
    lane = lax.broadcasted_iota(jnp.int32, lf.shape, 1)
    c = lf
    k = 1
    while k < seq:
        c = c + jnp.where(row >= k, pltpu.roll(c, k, axis=0), 0.0)
        k *= 2
    rem = c * LOG2E
    out = jnp.zeros(lf.shape, F32)
    for n in range(C_TERMS):
        t = rem.astype(BF16).astype(F32)
        rem = rem - t
        shifted = t if n == 0 else pltpu.roll(t, n * N_HEADS, axis=1)
        out = jnp.where((lane >= n * N_HEADS) & (lane < (n + 1) * N_HEADS), shifted, out)
    c_ref[...] = out.astype(BF16)


def _forget(fl, b_pad, *, batch, seq):
    return pl.pallas_call(
        functools.partial(_forget_body, seq=seq),
        grid=(batch,),
        in_specs=[
            pl.BlockSpec((seq, LANES), lambda b: (b, 0)),
            pl.BlockSpec((1, LANES), lambda b: (0, 0)),
        ],
        out_specs=pl.BlockSpec((seq, LANES), lambda b: (b, 0)),
        out_shape=jax.ShapeDtypeStruct((batch * seq, LANES), BF16),
        compiler_params=_cparams(("arbitrary",)),
        name="forget",
    )(fl, b_pad)


ATTN_HEADS_PER_STEP = 2


def _attn_body(q_ref, k_ref, v_ref, kc_ref, o_ref, *, tq):
    i = pl.program_id(2)
    lane = lax.broadcasted_iota(jnp.int32, (tq, LANES), 1)
    heads = range(ATTN_HEADS_PER_STEP)
    cols =[slice(g * HEAD_DIM, (g + 1) * HEAD_DIM) for g in heads]
    qs = []
    for g in heads:
        h = pl.program_id(1) * ATTN_HEADS_PER_STEP + g
        mine = (lane < C_TERMS * N_HEADS) & ((lane & (N_HEADS - 1)) == h)
        qc = jnp.where(mine, -1.0, 0.0).astype(BF16)
        qs.append(jnp.concatenate([q_ref[:, cols[g]], qc], axis=1))

    def step(j, carry, masked):
        off = pl.multiple_of(j * tq, tq)
        out = []
        for g in heads:
            m, l, acc = carry[g]
            kj = jnp.concatenate([k_ref[pl.ds(off, tq), cols[g]], kc_ref[pl.ds(off, tq), :]], axis=1)
            s = lax.dot_general(qs[g], kj, (((1,), (1,)), ((), ())), preferred_element_type=F32)
            if masked:
                r = lax.broadcasted_iota(jnp.int32, s.shape, 0)
                cidx = lax.broadcasted_iota(jnp.int32, s.shape, 1)
                s = jnp.where(cidx <= r, s, -jnp.inf)
            m_new = jnp.maximum(m, jnp.max(s, axis=-1, keepdims=True))
            p = jnp.exp2(s - m_new)
            alpha = jnp.exp2(m - m_new)
            l = alpha * l + jnp.sum(p, axis=-1, keepdims=True)
            acc = alpha * acc + jnp.dot(p.astype(BF16), v_ref[pl.ds(off, tq), cols[g]],
                                        preferred_element_type=F32)
            out.append((m_new, l, acc))
        return tuple(out)

    init = tuple((jnp.full((tq, 1), -jnp.inf, F32), jnp.zeros((tq, 1), F32), jnp.zeros((tq, HEAD_DIM), F32))
                 for _ in heads)
    carry = lax.fori_loop(0, i, lambda j, c: step(j, c, False), init)
    carry = step(i, carry, True)
    for g in heads:
        _, l, acc = carry[g]
        o_ref[:, cols[g]] = (acc / l).astype(BF16)


def _attention(z, caug, *, batch, seq, tq=512):
    nq = seq // tq
    T = batch * seq
    G = ATTN_HEADS_PER_STEP
    W = G * HEAD_DIM
    return pl.pallas_call(
        functools.partial(_attn_body, tq=tq),
        grid=(batch, N_HEADS // G, nq),
        in_specs=[
            pl.BlockSpec((tq, W), lambda b, h, i: (b * nq + i, ZC_Q // G + h)),
            pl.BlockSpec((seq, W), lambda b, h, i: (b, ZC_K // G + h)),
            pl.BlockSpec((seq, W), lambda b, h, i: (b, ZC_V // G + h)),
            pl.BlockSpec((seq, LANES), lambda b, h, i: (b, 0)),
        ],
        out_specs=pl.BlockSpec((tq, W), lambda b, h, i: (b * nq + i, h)),
        out_shape=jax.ShapeDtypeStruct((T, D_MODEL), BF16),
        compiler_params=_cparams(("arbitrary", "arbitrary", "arbitrary")),
        name="attn",
    )(z, z, z, caug)


def _lru_body(xr_ref, gr_ref, cw_ref, cb_ref, wa_ref, wx_ref, ba_ref, bx_ref, lam_ref,
              hg_ref, xbuf, hcar, *, ts):
    i = pl.program_id(1)

    @pl.when(i == 0)
    def _():
        xbuf[0:SUBLANES, :] = jnp.zeros((SUBLANES, D_MODEL), F32)
        hcar[...] = jnp.zeros_like(hcar)

    @pl.when(i > 0)
    def _():
        xbuf[0:SUBLANES, :] = xbuf[ts:ts + SUBLANES, :]

    xbuf[SUBLANES:ts + SUBLANES, :] = xr_ref[...].astype(F32)

    base = SUBLANES - (CONV_W - 1)
    xc = cw_ref[0:1, :] * xbuf[base:base + ts, :]
    for k in range(1, CONV_W):
        xc = xc + cw_ref[k:k + 1, :] * xbuf[base + k:base + k + ts, :]
    xc = xc + cb_ref[...]

    xcb = xc.astype(BF16)
    ra = jnp.concatenate(
        [jnp.dot(xcb[:, n * LRU_BLOCK_W:(n + 1) * LRU_BLOCK_W], wa_ref[n], preferred_element_type=F32)
         for n in range(LRU_BLOCKS)], axis=-1)
    rx = jnp.concatenate(
        [jnp.dot(xcb[:, n * LRU_BLOCK_W:(n + 1) * LRU_BLOCK_W], wx_ref[n], preferred_element_type=F32)
         for n in range(LRU_BLOCKS)], axis=-1)
    r = jax.nn.sigmoid(ra + ba_ref[...])
    ig = jax.nn.sigmoid(rx + bx_ref[...])
    nlam = -lam_ref[...]
    softplus = jnp.maximum(nlam, 0.0) + jnp.log1p(jnp.exp(-jnp.abs(nlam)))
    log_a = (-LRU_C * r) * softplus
    a = jnp.exp(log_a)
    th = jnp.tanh(log_a)
    mult = jnp.sqrt(-2.0 * th / (1.0 - th))
    b = mult * ig * xc

    row = lax.broadcasted_iota(jnp.int32, (SUBLANES, D_MODEL), 0)
    keeps = [(k, row >= k) for k in (1, 2, 4)]
    hprev = jnp.broadcast_to(hcar[...], (SUBLANES, D_MODEL))
    pieces = []
    for j in range(ts // SUBLANES):
        aj = a[j * SUBLANES:(j + 1) * SUBLANES, :]
        bj = b[j * SUBLANES:(j + 1) * SUBLANES, :]
        for k, keep in keeps:
            a_sh = jnp.where(keep, pltpu.roll(aj, k, axis=0), 1.0)
            b_sh = jnp.where(keep, pltpu.roll(bj, k, axis=0), 0.0)
            bj = aj * b_sh + bj
            aj = aj * a_sh
        hj = bj + aj * hprev
        hprev = jnp.broadcast_to(hj[SUBLANES - 1:SUBLANES, :], (SUBLANES, D_MODEL))
        pieces.append(hj)
    h = jnp.concatenate(pieces, axis=0)
    hcar[...] = h[ts - 1:ts, :]

    hg_ref[...] = (h * jax.nn.gelu(gr_ref[...].astype(F32))).astype(BF16)


def _lru(z, conv_w, conv_b, wa, wx, ba, bx, lam, *, batch, seq, ts=256):
    ns = seq // ts
    T = batch * seq
    full = lambda shape: pl.BlockSpec(shape, lambda b, i: (0,) * len(shape))
    return pl.pallas_call(
        functools.partial(_lru_body, ts=ts),
        grid=(batch, ns),
        in_specs=[
            pl.BlockSpec((ts, D_MODEL), lambda b, i: (b * ns + i, ZC_XR // SUBLANES)),
            pl.BlockSpec((ts, D_MODEL), lambda b, i: (b * ns + i, ZC_GR // SUBLANES)),
            full((CONV_W, D_MODEL)),
            full((1, D_MODEL)),
            full((LRU_BLOCKS, LRU_BLOCK_W, LRU_BLOCK_W)),
            full((LRU_BLOCKS, LRU_BLOCK_W, LRU_BLOCK_W)),
            full((1, D_MODEL)),
            full((1, D_MODEL)),
            full((1, D_MODEL)),
        ],
        out_specs=pl.BlockSpec((ts, D_MODEL), lambda b, i: (b * ns + i, 0)),
        out_shape=jax.ShapeDtypeStruct((T, D_MODEL), BF16),
        scratch_shapes=[pltpu.VMEM((ts + SUBLANES, D_MODEL), F32), pltpu.VMEM((1, D_MODEL), F32)],
        compiler_params=_cparams(("arbitrary", "arbitrary")),
        name="lru",
    )(z, z, conv_w, conv_b, wa, wx, ba, bx, lam)


MERGE_SUBTILES = 1


def _merge_body(hg_ref, o_ref, ml_ref, ma_ref, x_ref, wl_ref, wat_ref, wo_ref, g_ref, wr_ref, br_ref,
                x1_ref, v_ref, ri_ref, rt_ref, cnt_ref, carry_ref, *, tm):
    @pl.when(pl.program_id(0) == 0)
    def _():
        carry_ref[...] = jnp.zeros_like(carry_ref)

    ts = tm // MERGE_SUBTILES
    lane = lax.broadcasted_iota(jnp.int32, (ts, LANES), 1)
    rr = lax.broadcasted_iota(jnp.int32, (ts, ts), 0)
    cc = lax.broadcasted_iota(jnp.int32, (ts, ts), 1)
    tri = jnp.where(cc < rr, 1.0, 0.0).astype(BF16)
    ninf = -jnp.inf
    big = jnp.int32(1 << 20)
    carry = carry_ref[...]

    for sub in range(MERGE_SUBTILES):
        rows = slice(sub * ts, (sub + 1) * ts)
        yl = jnp.dot(hg_ref[rows, :], wl_ref[...], preferred_element_type=F32)
        ya = jnp.dot(o_ref[rows, :], wat_ref[...], preferred_element_type=F32)
        merged = (jax.nn.sigmoid(ml_ref[rows, :].astype(F32)) * yl
                  + jax.nn.sigmoid(ma_ref[rows, :].astype(F32)) * ya)
        x1 = x_ref[rows, :] + jnp.dot(merged.astype(BF16), wo_ref[...], preferred_element_type=F32)
        x1_ref[rows, :] = x1
        ms = jnp.mean(x1 * x1, axis=-1, keepdims=True)
        v = x1 * lax.rsqrt(ms + EPS) * g_ref[...]
        v_ref[rows, :] = v

        logits = jnp.dot(v.astype(BF16), wr_ref[...], preferred_element_type=F32) + br_ref[...]

        gl = jnp.where(lane < N_GROUPS, logits, ninf)
        gmax = jnp.max(gl, axis=-1, keepdims=True)
        gsel = jnp.min(jnp.where(gl == gmax, lane, big), axis=-1, keepdims=True)
        pg = 1.0 / jnp.sum(jnp.exp(gl - gmax), axis=-1, keepdims=True)

        lo = R_EXP0 + gsel * EXPERTS_PER_GROUP
        el = jnp.where(lane >= lo, jnp.where(lane < lo + EXPERTS_PER_GROUP, logits, ninf), ninf)
        v1 = jnp.max(el, axis=-1, keepdims=True)
        i1 = jnp.min(jnp.where(el == v1, lane, big), axis=-1, keepdims=True)
        el2 = jnp.where(lane == i1, ninf, el)
        v2 = jnp.max(el2, axis=-1, keepdims=True)
        i2 = jnp.min(jnp.where(el2 == v2, lane, big), axis=-1, keepdims=True)
        e21 = jnp.exp(v2 - v1)
        p1 = 1.0 / (1.0 + e21)
        w0 = pg * p1
        w1 = pg * (e21 * p1)

        hit0 = lane == i1
        hit1 = lane == i2
        onehot = jnp.where(hit0, 1.0, jnp.where(hit1, 1.0, 0.0))
        before = jnp.dot(tri, onehot.astype(BF16), preferred_element_type=F32) + carry
        rank0 = jnp.sum(jnp.where(hit0, before, 0.0), axis=-1, keepdims=True)
        rank1 = jnp.sum(jnp.where(hit1, before, 0.0), axis=-1, keepdims=True)
        carry = carry + jnp.sum(onehot, axis=0, keepdims=True)

        e0 = (i1 - R_EXP0).astype(F32)
        e1 = (i2 - R_EXP0).astype(F32)
        ri = jnp.where(lane == 0, e0,
             jnp.where(lane == 1, e1,
             jnp.where(lane == 2, w0,
             jnp.where(lane == 3, w1,
             jnp.where(lane == 4, rank0,
             jnp.where(lane == 5, rank1, 0.0))))))
        ri_ref[rows, :] = ri
        rt_ref[:, rows] = ri.T[:SUBLANES, :]

    carry_ref[...] = carry
    cnt_ref[...] = carry


def _merge(hg, o, z, x2, wl, wat, wo, g, wr, br, *, tm=512):
    T = x2.shape[0]
    full = lambda shape: pl.BlockSpec(shape, lambda i: (0,) * len(shape))
    return pl.pallas_call(
        functools.partial(_merge_body, tm=tm),
        grid=(T // tm,),
        in_specs=[
            pl.BlockSpec((tm, D_MODEL), lambda i: (i, 0)),
            pl.BlockSpec((tm, D_MODEL), lambda i: (i, 0)),
            pl.BlockSpec((tm, D_MODEL), lambda i: (i, ZC_ML // SUBLANES)),
            pl.BlockSpec((tm, D_MODEL), lambda i: (i, ZC_MA // SUBLANES)),
            pl.BlockSpec((tm, D_MODEL), lambda i: (i, 0)),
            full((D_MODEL, D_MODEL)),
            full((D_MODEL, D_MODEL)),
            full((D_MODEL, D_MODEL)),
            full((1, D_MODEL)),
            full((D_MODEL, LANES)),
            full((1, LANES)),
        ],
        out_specs=[
            pl.BlockSpec((tm, D_MODEL), lambda i: (i, 0)),
            pl.BlockSpec((tm, D_MODEL), lambda i: (i, 0)),
            pl.BlockSpec((tm, LANES), lambda i: (i, 0)),
            pl.BlockSpec((SUBLANES, tm), lambda i: (0, i)),
            pl.BlockSpec((1, LANES), lambda i: (0, 0)),
        ],
        out_shape=[
            jax.ShapeDtypeStruct((T, D_MODEL), F32),
            jax.ShapeDtypeStruct((T, D_MODEL), F32),
            jax.ShapeDtypeStruct((T, LANES), F32),
            jax.ShapeDtypeStruct((SUBLANES, T), F32),
            jax.ShapeDtypeStruct((1, LANES), F32),
        ],
        scratch_shapes=[pltpu.VMEM((1, LANES), F32)],
        compiler_params=_cparams(("arbitrary",)),
        name="merge",
    )(hg, o, z, z, x2, wl, wat, wo, g, wr, br)


def _row_slice(r):
    return pl.ds(pl.multiple_of(r * ROW_CHUNKS, ROW_CHUNKS), ROW_CHUNKS)


def _dispatch_body(dest_ref, fill_ref, v_ref, xs_ref, rows, zrow, zblock, sem, zsem, *, td, nsteps, n_blocks):
    i = pl.program_id(0)

    def copy(r, k):
        d = dest_ref[k * (nsteps * td) + i * td + r]
        return pltpu.make_async_copy(rows.at[_row_slice(r)], xs_ref.at[_row_slice(d)], sem)

    def for_each_row(fn):
        def f(r, carry):
            for k in range(TOP_K):
                fn(copy(r, k), k)
            return carry
        lax.fori_loop(0, td, f, 0)

    def zero_copy(e, r):
        return pltpu.make_async_copy(zrow, xs_ref.at[_row_slice(fill_ref[e] + r)], zsem)

    def for_each_padding_row(fn):
        def per_expert(e, carry):
            lax.fori_loop(0, fill_ref[N_EXPERTS + e], lambda r, c: (fn(zero_copy(e, r)), c)[1], 0)
            return carry
        lax.fori_loop(0, N_EXPERTS, per_expert, 0)

    def zero_block_copy(blk):
        start = pl.multiple_of(blk * (EXPERT_ROWS * ROW_CHUNKS), EXPERT_ROWS * ROW_CHUNKS)
        return pltpu.make_async_copy(zblock, xs_ref.at[pl.ds(start, EXPERT_ROWS * ROW_CHUNKS)], zsem)

    def for_each_unused_block(fn):
        lax.fori_loop(fill_ref[2 * N_EXPERTS], n_blocks, lambda blk, c: (fn(zero_block_copy(blk)), c)[1], 0)

    @pl.when(i == 0)
    def _():
        zrow[...] = jnp.zeros_like(zrow)
        zblock[...] = jnp.zeros_like(zblock)
        for_each_padding_row(lambda cp: cp.start())
        for_each_unused_block(lambda cp: cp.start())

    for s in range(ROW_CHUNKS):
        rows[pl.ds(s, td, stride=ROW_CHUNKS), :] = v_ref[:, s * LANES:(s + 1) * LANES]
    for_each_row(lambda cp, k: cp.start(priority=k))
    for_each_row(lambda cp, k: cp.wait())

    @pl.when(i == nsteps - 1)
    def _():
        for_each_padding_row(lambda cp: cp.wait())
        for_each_unused_block(lambda cp: cp.wait())


def _dispatch(dest, fill, v, n_rows, *, td=256):
    T = v.shape[0]
    nsteps = T // td
    n_blocks = n_rows // EXPERT_ROWS
    grid_spec = pltpu.PrefetchScalarGridSpec(
        num_scalar_prefetch=2,
        grid=(nsteps,),
        in_specs=[pl.BlockSpec((td, D_MODEL), lambda i, dest, fill: (i, 0))],
        out_specs=pl.BlockSpec(memory_space=pl.ANY),
        scratch_shapes=[pltpu.VMEM((td * ROW_CHUNKS, LANES), F32),
                        pltpu.VMEM((ROW_CHUNKS, LANES), F32),
                        pltpu.VMEM((EXPERT_ROWS * ROW_CHUNKS, LANES), F32),
                        pltpu.SemaphoreType.DMA(()),
                        pltpu.SemaphoreType.DMA(())],
    )
    return pl.pallas_call(
        functools.partial(_dispatch_body, td=td, nsteps=nsteps, n_blocks=n_blocks),
        grid_spec=grid_spec,
        out_shape=jax.ShapeDtypeStruct((n_rows * ROW_CHUNKS, LANES), F32),
        compiler_params=_cparams(("arbitrary",)),
        name="dispatch",
    )(dest, fill, v)


def _expert_body(be_ref, nu_ref, x_ref, w1_ref, w3_ref, w2_ref, y_ref, w13b, w2b, *, rows):
    i = pl.program_id(0)

    @pl.when((i == 0) | (be_ref[i] != be_ref[jnp.maximum(i - 1, 0)]))
    def _():
        w13b[:, :D_EXPERT] = w1_ref[0].astype(BF16)
        w13b[:, D_EXPERT:] = w3_ref[0].astype(BF16)
        w2b[...] = w2_ref[0].astype(BF16)

    @pl.when(i < nu_ref[0])
    def _():
        xb = jnp.concatenate(
            [x_ref[pl.ds(s, rows, stride=ROW_CHUNKS), :].astype(BF16) for s in range(ROW_CHUNKS)], axis=-1)
        gu = jnp.dot(xb, w13b[...], preferred_element_type=F32)
        hb = (jax.nn.silu(gu[:, :D_EXPERT]) * gu[:, D_EXPERT:]).astype(BF16)
        y = jnp.dot(hb, w2b[...], preferred_element_type=F32)
        for s in range(ROW_CHUNKS):
            y_ref[pl.ds(s, rows, stride=ROW_CHUNKS), :] = y[:, s * LANES:(s + 1) * LANES]

    @pl.when(i >= nu_ref[0])
    def _():
        y_ref[...] = jnp.zeros_like(y_ref)


def _experts(blk_expert, n_used, xs, w1, w3, w2, *, rows=EXPERT_ROWS):
    n_blocks = xs.shape[0] // (rows * ROW_CHUNKS)
    grid_spec = pltpu.PrefetchScalarGridSpec(
        num_scalar_prefetch=2,
        grid=(n_blocks,),
        in_specs=[
            pl.BlockSpec((rows * ROW_CHUNKS, LANES), lambda i, be, nu: (jnp.minimum(i, nu[0] - 1), 0)),
            pl.BlockSpec((1, D_MODEL, D_EXPERT), lambda i, be, nu: (be[i], 0, 0)),
            pl.BlockSpec((1, D_MODEL, D_EXPERT), lambda i, be, nu: (be[i], 0, 0)),
            pl.BlockSpec((1, D_EXPERT, D_MODEL), lambda i, be, nu: (be[i], 0, 0)),
        ],
        out_specs=pl.BlockSpec((rows * ROW_CHUNKS, LANES), lambda i, be, nu: (i, 0)),
        scratch_shapes=[pltpu.VMEM((D_MODEL, 2 * D_EXPERT), BF16), pltpu.VMEM((D_EXPERT, D_MODEL), BF16)],
    )
    return pl.pallas_call(
        functools.partial(_expert_body, rows=rows),
        grid_spec=grid_spec,
        out_shape=jax.ShapeDtypeStruct(xs.shape, F32),
        compiler_params=_cparams(("arbitrary",)),
        name="experts",
    )(blk_expert, n_used, xs, w1, w3, w2)


SC_CORES = 2
SC_SUBCORES = 16
SC_WORKERS = SC_CORES * SC_SUBCORES
SC_CHUNK = 32


def _sc_gather_rows(y3, dests):
    T = dests[0].shape[0]
    per_worker = T // SC_WORKERS
    mesh = plsc.VectorSubcoreMesh(core_axis_name="c", subcore_axis_name="s")

    @functools.partial(
        pl.kernel, mesh=mesh,
        out_type=jax.ShapeDtypeStruct((TOP_K, T, ROW_CHUNKS, LANES), F32),
        scratch_types=[pltpu.VMEM((SC_CHUNK,), jnp.int32),
                       pltpu.VMEM((SC_CHUNK, ROW_CHUNKS, LANES), F32),
                       pltpu.SemaphoreType.DMA],
        name="sc_gather")
    def gather(y_hbm, d0_hbm, d1_hbm, out_hbm, idx, rows, sem):
        wid = lax.axis_index("s") * SC_CORES + lax.axis_index("c")
        base = wid * per_worker
        for k, d_hbm in enumerate((d0_hbm, d1_hbm)):
            @pl.loop(0, per_worker, step=SC_CHUNK)
            def _(c):
                off = base + c
                pltpu.sync_copy(d_hbm.at[pl.ds(off, SC_CHUNK)], idx)
                pltpu.async_copy(y_hbm.at[idx], rows, sem).wait()
                pltpu.sync_copy(rows, out_hbm.at[k, pl.ds(off, SC_CHUNK)])

    return gather(y3, *dests)


def _combine_body(yk_ref, x1_ref, ri_ref, g_ref, out_ref, *, tc, final_norm):
    def rows_of(k):
        return jnp.concatenate(
            [yk_ref[k, pl.ds(s, tc, stride=ROW_CHUNKS), :] for s in range(ROW_CHUNKS)], axis=-1)

    ri = ri_ref[...]
    w0 = ri[:, 2:3]
    w1 = ri[:, 3:4]
    x2 = x1_ref[...] + (w0 * rows_of(0) + w1 * rows_of(1))
    if final_norm:
        ms = jnp.mean(x2 * x2, axis=-1, keepdims=True)
        x2 = x2 * lax.rsqrt(ms + EPS) * g_ref[...]
    out_ref[...] = x2


def _combine(yk, x1, ri, g, *, final_norm, tc=256):
    T = x1.shape[0]
    return pl.pallas_call(
        functools.partial(_combine_body, tc=tc, final_norm=final_norm),
        grid=(T // tc,),
        in_specs=[
            pl.BlockSpec((TOP_K, tc * ROW_CHUNKS, LANES), lambda i: (0, i, 0)),
            pl.BlockSpec((tc, D_MODEL), lambda i: (i, 0)),
            pl.BlockSpec((tc, LANES), lambda i: (i, 0)),
            pl.BlockSpec((1, D_MODEL), lambda i: (0, 0)),
        ],
        out_specs=pl.BlockSpec((tc, D_MODEL), lambda i: (i, 0)),
        out_shape=jax.ShapeDtypeStruct((T, D_MODEL), F32),
        compiler_params=_cparams(("arbitrary",)),
        name="combine",
    )(yk, x1, ri, g)


def _pad_lanes(a, width=LANES):
    return jnp.pad(a, ((0, 0), (0, width - a.shape[-1])))


def _layer(x2, batch, seq, g_mix, w_in, conv_w, conv_b, w_rg_a, b_rg_a, w_rg_x, b_rg_x, lam, b_forget,
           w_lru_out, w_attn_out, w_out, g_ffn, w_rgrp, b_rgrp, w_rexp, b_rexp, w1, w3, w2):
    T = batch * seq
    row = lambda a: a.reshape(1, -1).astype(F32)

    fl0 = 5 * D_MODEL
    w_main = jnp.concatenate([w_in[:, :fl0], w_in[:, fl0 + N_HEADS:]], axis=1).astype(BF16)
    w_fl = _pad_lanes(w_in[:, fl0:fl0 + N_HEADS]).astype(BF16)
    z, fl = _inproj(x2, row(g_mix), w_main, w_fl)

    caug = _forget(fl, _pad_lanes(row(b_forget)), batch=batch, seq=seq)
    o = _attention(z, caug, batch=batch, seq=seq)

    hg = _lru(z, conv_w.astype(F32), row(conv_b), w_rg_a.astype(BF16), w_rg_x.astype(BF16),
              row(b_rg_a), row(b_rg_x), row(lam), batch=batch, seq=seq)

    wr = _pad_lanes(jnp.concatenate([w_rgrp, w_rexp], axis=1)).astype(BF16)
    br = _pad_lanes(jnp.concatenate([row(b_rgrp), row(b_rexp)], axis=1))
    x1, v, ri, rt, cnt = _merge(hg, o, z, x2, w_lru_out.astype(BF16), w_attn_out.astype(BF16),
                            w_out.astype(BF16), row(g_ffn), wr, br)

    e = rt[0:TOP_K].astype(jnp.int32)
    rank = rt[4:4 + TOP_K].astype(jnp.int32)
    counts = cnt[0, R_EXP0:R_EXP0 + N_EXPERTS].astype(jnp.int32)
    padded = (counts + EXPERT_ROWS - 1) // EXPERT_ROWS * EXPERT_ROWS
    pad_end = jnp.cumsum(padded)
    pad_start = pad_end - padded
    seg_start = jnp.zeros_like(e)
    for j in range(N_EXPERTS):
        seg_start = jnp.where(e == j, pad_start[j], seg_start)
    dest2 = (seg_start + rank).astype(jnp.int32)
    dest = dest2.reshape(T * TOP_K)
    n_blocks = (T * TOP_K + N_EXPERTS * (EXPERT_ROWS - 1) + EXPERT_ROWS - 1) // EXPERT_ROWS
    blk_start = jnp.arange(n_blocks, dtype=jnp.int32) * EXPERT_ROWS
    blk_expert = jnp.minimum(
        jnp.sum((pad_end[None, :] <= blk_start[:, None]).astype(jnp.int32), axis=1), N_EXPERTS - 1)
    n_used = (pad_end[-1:] // EXPERT_ROWS).astype(jnp.int32)

    fill = jnp.concatenate([pad_start + counts, padded - counts, n_used]).astype(jnp.int32)
    xs = _dispatch(dest, fill, v, n_blocks * EXPERT_ROWS)
    y = _experts(blk_expert, n_used, xs, w1, w3, w2)
    yk = _sc_gather_rows(y.reshape(-1, ROW_CHUNKS, LANES), [dest2[k] for k in range(TOP_K)])
    return x1, yk.reshape(TOP_K, T * ROW_CHUNKS, LANES), ri


def kernel(x, g_mix, w_in, conv_w, conv_b, w_rg_a, b_rg_a, w_rg_x, b_rg_x, lru_lambda, b_forget, w_lru_out, w_attn_out, w_out, g_ffn, w_route_group, b_route_group, w_route_expert, b_route_expert, w_exp_gate, w_exp_up, w_exp_down, g_final):
    batch, seq, _ = x.shape
    depth = g_mix.shape[0]
    x2 = x.reshape(batch * seq, D_MODEL)
    for l in range(depth):
        x1, yk, ri = _layer(
            x2, batch, seq, g_mix[l], w_in[l], conv_w[l], conv_b[l], w_rg_a[l], b_rg_a[l], w_rg_x[l],
            b_rg_x[l], lru_lambda[l], b_forget[l], w_lru_out[l], w_attn_out[l], w_out[l], g_ffn[l],
            w_route_group[l], b_route_group[l], w_route_expert[l], b_route_expert[l],
            w_exp_gate[l], w_exp_up[l], w_exp_down[l])
        x2 = _combine(yk, x1, ri, g_final.reshape(1, -1).astype(F32), final_norm=l == depth - 1)
    return x2.reshape(batch, seq, D_MODEL)
```

```python
import functools

import jax
import jax.numpy as jnp
from jax import lax
from jax.experimental import pallas as pl
from jax.experimental.pallas import tpu as pltpu
from jax.experimental.pallas import tpu_sc as plsc

F32 = jnp.float32
BF16 = jnp.bfloat16

D_MODEL = 1024
LRU_BLOCK_W = 256
LRU_BLOCKS = D_MODEL // LRU_BLOCK_W
CONV_W = 4
LRU_C = 8.0
N_HEADS = 8
HEAD_DIM = D_MODEL // N_HEADS
N_GROUPS = 4
EXPERTS_PER_GROUP = 8
N_EXPERTS = N_GROUPS * EXPERTS_PER_GROUP
TOP_K = 2
D_EXPERT = D_MODEL // 2
EPS = 1e-6

LANES = 128
SUBLANES = 8
ROW_CHUNKS = D_MODEL // LANES
VMEM_LIMIT = 48 * 1024 * 1024

ZC_XR, ZC_GR, ZC_Q, ZC_K, ZC_V, ZC_ML, ZC_MA = 0, 8, 16, 24, 32, 40, 48
Z_WIDTH = 7 * D_MODEL

R_EXP0 = N_GROUPS

EXPERT_ROWS = 256


def _cparams(sem):
    return pltpu.CompilerParams(dimension_semantics=sem, vmem_limit_bytes=VMEM_LIMIT)


LOG2E = 1.4426950408889634
Q_PRESCALE = HEAD_DIM ** -0.5 * LOG2E


def _inproj_body(x_ref, g_ref, w_ref, wfl_ref, z_ref, fl_ref, u_ref, *, q_block):
    j = pl.program_id(1)

    @pl.when(j == 0)
    def _():
        x = x_ref[...]
        ms = jnp.mean(x * x, axis=-1, keepdims=True)
        u = (x * lax.rsqrt(ms + EPS) * g_ref[...]).astype(BF16)
        u_ref[...] = u
        fl_ref[...] = jnp.dot(u, wfl_ref[...], preferred_element_type=F32)

    @pl.when(j == q_block)
    def _():
        acc = jnp.dot(u_ref[...], w_ref[...], preferred_element_type=F32)
        z_ref[...] = (acc * Q_PRESCALE).astype(BF16)

    @pl.when(j != q_block)
    def _():
        z_ref[...] = jnp.dot(u_ref[...], w_ref[...], preferred_element_type=F32).astype(BF16)


def _inproj(x2, g, w_main, w_fl, *, tm=1024, tn=D_MODEL):
    T = x2.shape[0]
    return pl.pallas_call(
        functools.partial(_inproj_body, q_block=ZC_Q * LANES // tn),
        grid=(T // tm, Z_WIDTH // tn),
        in_specs=[
            pl.BlockSpec((tm, D_MODEL), lambda i, j: (i, 0)),
            pl.BlockSpec((1, D_MODEL), lambda i, j: (0, 0)),
            pl.BlockSpec((D_MODEL, tn), lambda i, j: (0, j)),
            pl.BlockSpec((D_MODEL, LANES), lambda i, j: (0, 0)),
        ],
        out_specs=[
            pl.BlockSpec((tm, tn), lambda i, j: (i, j)),
            pl.BlockSpec((tm, LANES), lambda i, j: (i, 0)),
        ],
        out_shape=[
            jax.ShapeDtypeStruct((T, Z_WIDTH), BF16),
            jax.ShapeDtypeStruct((T, LANES), F32),
        ],
        scratch_shapes=[pltpu.VMEM((tm, D_MODEL), BF16)],
        compiler_params=_cparams(("arbitrary", "arbitrary")),
        name="inproj",
    )(x2, g, w_main, w_fl)


C_TERMS = 3


def _log_sigmoid(z):
    return jnp.minimum(z, 0.0) - jnp.log1p(jnp.exp(-jnp.abs(z)))


def _forget_body(fl_ref, b_ref, c_ref, *, seq):
    lf = _log_sigmoid(fl_ref[...] + b_ref[...])
    row = lax.broadcasted_iota(jnp.int32, lf.shape, 0)
    lane = lax.broadcasted_iota(jnp.int32, lf.shape, 1)
    c = lf
    k = 1
    while k < seq:
        c = c + jnp.where(row >= k, pltpu.roll(c, k, axis=0), 0.0)
        k *= 2
    rem = c * LOG2E
    out = jnp.zeros(lf.shape, F32)
    for n in range(C_TERMS):
        t = rem.astype(BF16).astype(F32)
        rem = rem - t
        shifted = t if n == 0 else pltpu.roll(t, n * N_HEADS, axis=1)
        out = jnp.where((lane >= n * N_HEADS) & (lane < (n + 1) * N_HEADS), shifted, out)
    c_ref[...] = out.astype(BF16)


def _forget(fl, b_pad, *, batch, seq):
    return pl.pallas_call(
        functools.partial(_forget_body, seq=seq),
        grid=(batch,),
        in_specs=[
            pl.BlockSpec((seq, LANES), lambda b: (b, 0)),
            pl.BlockSpec((1, LANES), lambda b: (0, 0)),
        ],
        out_specs=pl.BlockSpec((seq, LANES), lambda b: (b, 0)),
        out_shape=jax.ShapeDtypeStruct((batch * seq, LANES), BF16),
        compiler_params=_cparams(("arbitrary",)),
        name="forget",
    )(fl, b_pad)


ATTN_HEADS_PER_STEP = 2


def _attn_body(q_ref, k_ref, v_ref, kc_ref, o_ref, *, tq):
    i = pl.program_id(2)
    lane = lax.broadcasted_iota(jnp.int32, (tq, LANES), 1)
    heads = range(ATTN_HEADS_PER_STEP)
    cols =[slice(g * HEAD_DIM, (g + 1) * HEAD_DIM) for g in heads]
    qs = []
    for g in heads:
        h = pl.program_id(1) * ATTN_HEADS_PER_STEP + g
        mine = (lane < C_TERMS * N_HEADS) & ((lane & (N_HEADS - 1)) == h)
        qc = jnp.where(mine, -1.0, 0.0).astype(BF16)
        qs.append(jnp.concatenate([q_ref[:, cols[g]], qc], axis=1))

    def step(j, carry, masked):
        off = pl.multiple_of(j * tq, tq)
        out = []
        for g in heads:
            m, l, acc = carry[g]
            kj = jnp.concatenate([k_ref[pl.ds(off, tq), cols[g]], kc_ref[pl.ds(off, tq), :]], axis=1)
            s = lax.dot_general(qs[g], kj, (((1,), (1,)), ((), ())), preferred_element_type=F32)
            if masked:
                r = lax.broadcasted_iota(jnp.int32, s.shape, 0)
                cidx = lax.broadcasted_iota(jnp.int32, s.shape, 1)
                s = jnp.where(cidx <= r, s, -jnp.inf)
            m_new = jnp.maximum(m, jnp.max(s, axis=-1, keepdims=True))
            p = jnp.exp2(s - m_new)
            alpha = jnp.exp2(m - m_new)
            l = alpha * l + jnp.sum(p, axis=-1, keepdims=True)
            acc = alpha * acc + jnp.dot(p.astype(BF16), v_ref[pl.ds(off, tq), cols[g]],
                                        preferred_element_type=F32)
            out.append((m_new, l, acc))
        return tuple(out)

    init = tuple((jnp.full((tq, 1), -jnp.inf, F32), jnp.zeros((tq, 1), F32), jnp.zeros((tq, HEAD_DIM), F32))
                 for _ in heads)
    carry = lax.fori_loop(0, i, lambda j, c: step(j, c, False), init)
    carry = step(i, carry, True)
    for g in heads:
        _, l, acc = carry[g]
        o_ref[:, cols[g]] = (acc / l).astype(BF16)


def _attention(z, caug, *, batch, seq, tq=512):
    nq = seq // tq
    T = batch * seq
    G = ATTN_HEADS_PER_STEP
    W = G * HEAD_DIM
    return pl.pallas_call(
        functools.partial(_attn_body, tq=tq),
        grid=(batch, N_HEADS // G, nq),
        in_specs=[
            pl.BlockSpec((tq, W), lambda b, h, i: (b * nq + i, ZC_Q // G + h)),
            pl.BlockSpec((seq, W), lambda b, h, i: (b, ZC_K // G + h)),
            pl.BlockSpec((seq, W), lambda b, h, i: (b, ZC_V // G + h)),
            pl.BlockSpec((seq, LANES), lambda b, h, i: (b, 0)),
        ],
        out_specs=pl.BlockSpec((tq, W), lambda b, h, i: (b * nq + i, h)),
        out_shape=jax.ShapeDtypeStruct((T, D_MODEL), BF16),
        compiler_params=_cparams(("arbitrary", "arbitrary", "arbitrary")),
        name="attn",
    )(z, z, z, caug)


def _lru_body(xr_ref, gr_ref, cw_ref, cb_ref, wa_ref, wx_ref, ba_ref, bx_ref, lam_ref,
              hg_ref, xbuf, hcar, *, ts):
    i = pl.program_id(1)

    @pl.when(i == 0)
    def _():
        xbuf[0:SUBLANES, :] = jnp.zeros((SUBLANES, D_MODEL), F32)
        hcar[...] = jnp.zeros_like(hcar)

    @pl.when(i > 0)
    def _():
        xbuf[0:SUBLANES, :] = xbuf[ts:ts + SUBLANES, :]

    xbuf[SUBLANES:ts + SUBLANES, :] = xr_ref[...].astype(F32)

    base = SUBLANES - (CONV_W - 1)
    xc = cw_ref[0:1, :] * xbuf[base:base + ts, :]
    for k in range(1, CONV_W):
        xc = xc + cw_ref[k:k + 1, :] * xbuf[base + k:base + k + ts, :]
    xc = xc + cb_ref[...]

    xcb = xc.astype(BF16)
    ra = jnp.concatenate(
        [jnp.dot(xcb[:, n * LRU_BLOCK_W:(n + 1) * LRU_BLOCK_W], wa_ref[n], preferred_element_type=F32)
         for n in range(LRU_BLOCKS)], axis=-1)
    rx = jnp.concatenate(
        [jnp.dot(xcb[:, n * LRU_BLOCK_W:(n + 1) * LRU_BLOCK_W], wx_ref[n], preferred_element_type=F32)
         for n in range(LRU_BLOCKS)], axis=-1)
    r = jax.nn.sigmoid(ra + ba_ref[...])
    ig = jax.nn.sigmoid(rx + bx_ref[...])
    nlam = -lam_ref[...]
    softplus = jnp.maximum(nlam, 0.0) + jnp.log1p(jnp.exp(-jnp.abs(nlam)))
    log_a = (-LRU_C * r) * softplus
    a = jnp.exp(log_a)
    th = jnp.tanh(log_a)
    mult = jnp.sqrt(-2.0 * th / (1.0 - th))
    b = mult * ig * xc

    row = lax.broadcasted_iota(jnp.int32, (SUBLANES, D_MODEL), 0)
    keeps = [(k, row >= k) for k in (1, 2, 4)]
    hprev = jnp.broadcast_to(hcar[...], (SUBLANES, D_MODEL))
    pieces = []
    for j in range(ts // SUBLANES):
        aj = a[j * SUBLANES:(j + 1) * SUBLANES, :]
        bj = b[j * SUBLANES:(j + 1) * SUBLANES, :]
        for k, keep in keeps:
            a_sh = jnp.where(keep, pltpu.roll(aj, k, axis=0), 1.0)
            b_sh = jnp.where(keep, pltpu.roll(bj, k, axis=0), 0.0)
            bj = aj * b_sh + bj
            aj = aj * a_sh
        hj = bj + aj * hprev
        hprev = jnp.broadcast_to(hj[SUBLANES - 1:SUBLANES, :], (SUBLANES, D_MODEL))
        pieces.append(hj)
    h = jnp.concatenate(pieces, axis=0)
    hcar[...] = h[ts - 1:ts, :]

    hg_ref[...] = (h * jax.nn.gelu(gr_ref[...].astype(F32))).astype(BF16)


def _lru(z, conv_w, conv_b, wa, wx, ba, bx, lam, *, batch, seq, ts=256):
    ns = seq // ts
    T = batch * seq
    full = lambda shape: pl.BlockSpec(shape, lambda b, i: (0,) * len(shape))
    return pl.pallas_call(
        functools.partial(_lru_body, ts=ts),
        grid=(batch, ns),
        in_specs=[
            pl.BlockSpec((ts, D_MODEL), lambda b, i: (b * ns + i, ZC_XR // SUBLANES)),
            pl.BlockSpec((ts, D_MODEL), lambda b, i: (b * ns + i, ZC_GR // SUBLANES)),
            full((CONV_W, D_MODEL)),
            full((1, D_MODEL)),
            full((LRU_BLOCKS, LRU_BLOCK_W, LRU_BLOCK_W)),
            full((LRU_BLOCKS, LRU_BLOCK_W, LRU_BLOCK_W)),
            full((1, D_MODEL)),
            full((1, D_MODEL)),
            full((1, D_MODEL)),
        ],
        out_specs=pl.BlockSpec((ts, D_MODEL), lambda b, i: (b * ns + i, 0)),
        out_shape=jax.ShapeDtypeStruct((T, D_MODEL), BF16),
        scratch_shapes=[pltpu.VMEM((ts + SUBLANES, D_MODEL), F32), pltpu.VMEM((1, D_MODEL), F32)],
        compiler_params=_cparams(("arbitrary", "arbitrary")),
        name="lru",
    )(z, z, conv_w, conv_b, wa, wx, ba, bx, lam)


MERGE_SUBTILES = 1


def _merge_body(hg_ref, o_ref, ml_ref, ma_ref, x_ref, wl_ref, wat_ref, wo_ref, g_ref, wr_ref, br_ref,
                x1_ref, v_ref, ri_ref, rt_ref, cnt_ref, carry_ref, *, tm):
    @pl.when(pl.program_id(0) == 0)
    def _():
        carry_ref[...] = jnp.zeros_like(carry_ref)

    ts = tm // MERGE_SUBTILES
    lane = lax.broadcasted_iota(jnp.int32, (ts, LANES), 1)
    rr = lax.broadcasted_iota(jnp.int32, (ts, ts), 0)
    cc = lax.broadcasted_iota(jnp.int32, (ts, ts), 1)
    tri = jnp.where(cc < rr, 1.0, 0.0).astype(BF16)
    ninf = -jnp.inf
    big = jnp.int32(1 << 20)
    carry = carry_ref[...]

    for sub in range(MERGE_SUBTILES):
        rows = slice(sub * ts, (sub + 1) * ts)
        yl = jnp.dot(hg_ref[rows, :], wl_ref[...], preferred_element_type=F32)
        ya = jnp.dot(o_ref[rows, :], wat_ref[...], preferred_element_type=F32)
        merged = (jax.nn.sigmoid(ml_ref[rows, :].astype(F32)) * yl
                  + jax.nn.sigmoid(ma_ref[rows, :].astype(F32)) * ya)
        x1 = x_ref[rows, :] + jnp.dot(merged.astype(BF16), wo_ref[...], preferred_element_type=F32)
        x1_ref[rows, :] = x1
        ms = jnp.mean(x1 * x1, axis=-1, keepdims=True)
        v = x1 * lax.rsqrt(ms + EPS) * g_ref[...]
        for s in range(ROW_CHUNKS):
            v_ref[pl.ds(sub * ts * ROW_CHUNKS + s, ts, stride=ROW_CHUNKS), :] = v[:, s * LANES:(s + 1) * LANES]

        logits = jnp.dot(v.astype(BF16), wr_ref[...], preferred_element_type=F32) + br_ref[...]

        gl = jnp.where(lane < N_GROUPS, logits, ninf)
        gmax = jnp.max(gl, axis=-1, keepdims=True)
        gsel = jnp.min(jnp.where(gl == gmax, lane, big), axis=-1, keepdims=True)
        pg = 1.0 / jnp.sum(jnp.exp(gl - gmax), axis=-1, keepdims=True)

        lo = R_EXP0 + gsel * EXPERTS_PER_GROUP
        el = jnp.where(lane >= lo, jnp.where(lane < lo + EXPERTS_PER_GROUP, logits, ninf), ninf)
        v1 = jnp.max(el, axis=-1, keepdims=True)
        i1 = jnp.min(jnp.where(el == v1, lane, big), axis=-1, keepdims=True)
        el2 = jnp.where(lane == i1, ninf, el)
        v2 = jnp.max(el2, axis=-1, keepdims=True)
        i2 = jnp.min(jnp.where(el2 == v2, lane, big), axis=-1, keepdims=True)
        e21 = jnp.exp(v2 - v1)
        p1 = 1.0 / (1.0 + e21)
        w0 = pg * p1
        w1 = pg * (e21 * p1)

        hit0 = lane == i1
        hit1 = lane == i2
        onehot = jnp.where(hit0, 1.0, jnp.where(hit1, 1.0, 0.0))
        before = jnp.dot(tri, onehot.astype(BF16), preferred_element_type=F32) + carry
        rank0 = jnp.sum(jnp.where(hit0, before, 0.0), axis=-1, keepdims=True)
        rank1 = jnp.sum(jnp.where(hit1, before, 0.0), axis=-1, keepdims=True)
        carry = carry + jnp.sum(onehot, axis=0, keepdims=True)

        e0 = (i1 - R_EXP0).astype(F32)
        e1 = (i2 - R_EXP0).astype(F32)
        ri = jnp.where(lane == 0, e0,
             jnp.where(lane == 1, e1,
             jnp.where(lane == 2, w0,
             jnp.where(lane == 3, w1,
             jnp.where(lane == 4, rank0,
             jnp.where(lane == 5, rank1, 0.0))))))
        ri_ref[rows, :] = ri
        rt_ref[:, rows] = ri.T[:SUBLANES, :]

    carry_ref[...] = carry
    cnt_ref[...] = carry


def _merge(hg, o, z, x2, wl, wat, wo, g, wr, br, *, tm=512):
    T = x2.shape[0]
    full = lambda shape: pl.BlockSpec(shape, lambda i: (0,) * len(shape))
    return pl.pallas_call(
        functools.partial(_merge_body, tm=tm),
        grid=(T // tm,),
        in_specs=[
            pl.BlockSpec((tm, D_MODEL), lambda i: (i, 0)),
            pl.BlockSpec((tm, D_MODEL), lambda i: (i, 0)),
            pl.BlockSpec((tm, D_MODEL), lambda i: (i, ZC_ML // SUBLANES)),
            pl.BlockSpec((tm, D_MODEL), lambda i: (i, ZC_MA // SUBLANES)),
            pl.BlockSpec((tm, D_MODEL), lambda i: (i, 0)),
            full((D_MODEL, D_MODEL)),
            full((D_MODEL, D_MODEL)),
            full((D_MODEL, D_MODEL)),
            full((1, D_MODEL)),
            full((D_MODEL, LANES)),
            full((1, LANES)),
        ],
        out_specs=[
            pl.BlockSpec((tm, D_MODEL), lambda i: (i, 0)),
            pl.BlockSpec((tm * ROW_CHUNKS, LANES), lambda i: (i, 0)),
            pl.BlockSpec((tm, LANES), lambda i: (i, 0)),
            pl.BlockSpec((SUBLANES, tm), lambda i: (0, i)),
            pl.BlockSpec((1, LANES), lambda i: (0, 0)),
        ],
        out_shape=[
            jax.ShapeDtypeStruct((T, D_MODEL), F32),
            jax.ShapeDtypeStruct((T * ROW_CHUNKS, LANES), F32),
            jax.ShapeDtypeStruct((T, LANES), F32),
            jax.ShapeDtypeStruct((SUBLANES, T), F32),
            jax.ShapeDtypeStruct((1, LANES), F32),
        ],
        scratch_shapes=[pltpu.VMEM((1, LANES), F32)],
        compiler_params=_cparams(("arbitrary",)),
        name="merge",
    )(hg, o, z, z, x2, wl, wat, wo, g, wr, br)


def _row_slice(r):
    return pl.ds(pl.multiple_of(r * ROW_CHUNKS, ROW_CHUNKS), ROW_CHUNKS)


def _fill_padding_body(fill_ref, xs_in_ref, xs_ref, zrow, zblock, zsem, *, n_blocks):
    del xs_in_ref

    def zero_copy(e, r):
        return pltpu.make_async_copy(zrow, xs_ref.at[_row_slice(fill_ref[e] + r)], zsem)

    def for_each_padding_row(fn):
        def per_expert(e, carry):
            lax.fori_loop(0, fill_ref[N_EXPERTS + e], lambda r, c: (fn(zero_copy(e, r)), c)[1], 0)
            return carry
        lax.fori_loop(0, N_EXPERTS, per_expert, 0)

    def zero_block_copy(blk):
        start = pl.multiple_of(blk * (EXPERT_ROWS * ROW_CHUNKS), EXPERT_ROWS * ROW_CHUNKS)
        return pltpu.make_async_copy(zblock, xs_ref.at[pl.ds(start, EXPERT_ROWS * ROW_CHUNKS)], zsem)

    def for_each_unused_block(fn):
        lax.fori_loop(fill_ref[2 * N_EXPERTS], n_blocks, lambda blk, c: (fn(zero_block_copy(blk)), c)[1], 0)

    zrow[...] = jnp.zeros_like(zrow)
    zblock[...] = jnp.zeros_like(zblock)
    for_each_padding_row(lambda cp: cp.start())
    for_each_unused_block(lambda cp: cp.start())
    for_each_padding_row(lambda cp: cp.wait())
    for_each_unused_block(lambda cp: cp.wait())


def _fill_padding(fill, xs):
    n_blocks = xs.shape[0] // (EXPERT_ROWS * ROW_CHUNKS)
    grid_spec = pltpu.PrefetchScalarGridSpec(
        num_scalar_prefetch=1,
        grid=(1,),
        in_specs=[pl.BlockSpec(memory_space=pl.ANY)],
        out_specs=pl.BlockSpec(memory_space=pl.ANY),
        scratch_shapes=[pltpu.VMEM((ROW_CHUNKS, LANES), F32),
                        pltpu.VMEM((EXPERT_ROWS * ROW_CHUNKS, LANES), F32),
                        pltpu.SemaphoreType.DMA(())],
    )
    return pl.pallas_call(
        functools.partial(_fill_padding_body, n_blocks=n_blocks),
        grid_spec=grid_spec,
        out_shape=jax.ShapeDtypeStruct(xs.shape, F32),
        input_output_aliases={1: 0},
        compiler_params=_cparams(("arbitrary",)),
        name="fill_padding",
    )(fill, xs)


def _expert_body(be_ref, nu_ref, x_ref, w1_ref, w3_ref, w2_ref, y_ref, w13b, w2b, *, rows):
    i = pl.program_id(0)

    @pl.when((i == 0) | (be_ref[i] != be_ref[jnp.maximum(i - 1, 0)]))
    def _():
        w13b[:, :D_EXPERT] = w1_ref[0].astype(BF16)
        w13b[:, D_EXPERT:] = w3_ref[0].astype(BF16)
        w2b[...] = w2_ref[0].astype(BF16)

    @pl.when(i < nu_ref[0])
    def _():
        xb = jnp.concatenate(
            [x_ref[pl.ds(s, rows, stride=ROW_CHUNKS), :].astype(BF16) for s in range(ROW_CHUNKS)], axis=-1)
        gu = jnp.dot(xb, w13b[...], preferred_element_type=F32)
        hb = (jax.nn.silu(gu[:, :D_EXPERT]) * gu[:, D_EXPERT:]).astype(BF16)
        y = jnp.dot(hb, w2b[...], preferred_element_type=F32)
        for s in range(ROW_CHUNKS):
            y_ref[pl.ds(s, rows, stride=ROW_CHUNKS), :] = y[:, s * LANES:(s + 1) * LANES]

    @pl.when(i >= nu_ref[0])
    def _():
        y_ref[...] = jnp.zeros_like(y_ref)


def _experts(blk_expert, n_used, xs, w1, w3, w2, *, rows=EXPERT_ROWS):
    n_blocks = xs.shape[0] // (rows * ROW_CHUNKS)
    grid_spec = pltpu.PrefetchScalarGridSpec(
        num_scalar_prefetch=2,
        grid=(n_blocks,),
        in_specs=[
            pl.BlockSpec((rows * ROW_CHUNKS, LANES), lambda i, be, nu: (jnp.minimum(i, nu[0] - 1), 0)),
            pl.BlockSpec((1, D_MODEL, D_EXPERT), lambda i, be, nu: (be[i], 0, 0)),
            pl.BlockSpec((1, D_MODEL, D_EXPERT), lambda i, be, nu: (be[i], 0, 0)),
            pl.BlockSpec((1, D_EXPERT, D_MODEL), lambda i, be, nu: (be[i], 0, 0)),
        ],
        out_specs=pl.BlockSpec((rows * ROW_CHUNKS, LANES), lambda i, be, nu: (i, 0)),
        scratch_shapes=[pltpu.VMEM((D_MODEL, 2 * D_EXPERT), BF16), pltpu.VMEM((D_EXPERT, D_MODEL), BF16)],
    )
    return pl.pallas_call(
        functools.partial(_expert_body, rows=rows),
        grid_spec=grid_spec,
        out_shape=jax.ShapeDtypeStruct(xs.shape, F32),
        compiler_params=_cparams(("arbitrary",)),
        name="experts",
    )(blk_expert, n_used, xs, w1, w3, w2)


SC_CORES = 2
SC_SUBCORES = 16
SC_WORKERS = SC_CORES * SC_SUBCORES
SC_CHUNK = 32


def _sc_worker_base(per_worker):
    return (lax.axis_index("s") * SC_CORES + lax.axis_index("c")) * per_worker


def _sc_scatter_rows(v3, dests, n_rows):
    T = v3.shape[0]
    per_worker = T // SC_WORKERS
    mesh = plsc.VectorSubcoreMesh(core_axis_name="c", subcore_axis_name="s")
    bufs = range(2)

    @functools.partial(
        pl.kernel, mesh=mesh,
        out_type=jax.ShapeDtypeStruct((n_rows, ROW_CHUNKS, LANES), F32),
        scratch_types=[pltpu.VMEM((2 * TOP_K, SC_CHUNK), jnp.int32),
                       pltpu.VMEM((2, SC_CHUNK, ROW_CHUNKS, LANES), F32),
                       pltpu.SemaphoreType.DMA((2,)),
                       pltpu.SemaphoreType.DMA((2,))],
        name="sc_scatter")
    def scatter(v_hbm, d0_hbm, d1_hbm, xs_hbm, idx, rows, lsem, ssem):
        base = _sc_worker_base(per_worker)

        @pl.loop(0, per_worker, step=2 * SC_CHUNK)
        def _(c):
            loads = []
            for b in bufs:
                off = base + c + b * SC_CHUNK
                loads.append(pltpu.async_copy(v_hbm.at[pl.ds(off, SC_CHUNK)], rows.at[b], lsem.at[b]))
                for k, d_hbm in enumerate((d0_hbm, d1_hbm)):
                    pltpu.sync_copy(d_hbm.at[pl.ds(off, SC_CHUNK)], idx.at[b * TOP_K + k])
            stores = []
            for b in bufs:
                loads[b].wait()
                for k in range(TOP_K):
                    stores.append(pltpu.async_copy(rows.at[b], xs_hbm.at[idx.at[b * TOP_K + k]], ssem.at[b]))
            for st in stores:
                st.wait()

    return scatter(v3, *dests)


def _sc_gather_rows(y3, dests):
    T = dests[0].shape[0]
    per_worker = T // SC_WORKERS
    mesh = plsc.VectorSubcoreMesh(core_axis_name="c", subcore_axis_name="s")
    bufs = range(2)

    @functools.partial(
        pl.kernel, mesh=mesh,
        out_type=jax.ShapeDtypeStruct((TOP_K, T, ROW_CHUNKS, LANES), F32),
        scratch_types=[pltpu.VMEM((2, SC_CHUNK), jnp.int32),
                       pltpu.VMEM((2, SC_CHUNK, ROW_CHUNKS, LANES), F32),
                       pltpu.SemaphoreType.DMA((2,)),
                       pltpu.SemaphoreType.DMA((2,))],
        name="sc_gather")
    def gather(y_hbm, d0_hbm, d1_hbm, out_hbm, idx, rows, gsem, wsem):
        base = _sc_worker_base(per_worker)
        for k, d_hbm in enumerate((d0_hbm, d1_hbm)):
            @pl.loop(0, per_worker, step=2 * SC_CHUNK)
            def _(c):
                gathers = []
                for b in bufs:
                    off = base + c + b * SC_CHUNK
                    pltpu.sync_copy(d_hbm.at[pl.ds(off, SC_CHUNK)], idx.at[b])
                    gathers.append(pltpu.async_copy(y_hbm.at[idx.at[b]], rows.at[b], gsem.at[b]))
                writes = []
                for b in bufs:
                    off = base + c + b * SC_CHUNK
                    gathers[b].wait()
                    writes.append(pltpu.async_copy(rows.at[b], out_hbm.at[k, pl.ds(off, SC_CHUNK)], wsem.at[b]))
                for w in writes:
                    w.wait()

    return gather(y3, *dests)


def _combine_body(yk_ref, x1_ref, ri_ref, g_ref, out_ref, *, tc, final_norm):
    def rows_of(k):
        return jnp.concatenate(
            [yk_ref[k, pl.ds(s, tc, stride=ROW_CHUNKS), :] for s in range(ROW_CHUNKS)], axis=-1)

    ri = ri_ref[...]
    w0 = ri[:, 2:3]
    w1 = ri[:, 3:4]
    x2 = x1_ref[...] + (w0 * rows_of(0) + w1 * rows_of(1))
    if final_norm:
        ms = jnp.mean(x2 * x2, axis=-1, keepdims=True)
        x2 = x2 * lax.rsqrt(ms + EPS) * g_ref[...]
    out_ref[...] = x2


def _combine(yk, x1, ri, g, *, final_norm, tc=256):
    T = x1.shape[0]
    return pl.pallas_call(
        functools.partial(_combine_body, tc=tc, final_norm=final_norm),
        grid=(T // tc,),
        in_specs=[
            pl.BlockSpec((TOP_K, tc * ROW_CHUNKS, LANES), lambda i: (0, i, 0)),
            pl.BlockSpec((tc, D_MODEL), lambda i: (i, 0)),
            pl.BlockSpec((tc, LANES), lambda i: (i, 0)),
            pl.BlockSpec((1, D_MODEL), lambda i: (0, 0)),
        ],
        out_specs=pl.BlockSpec((tc, D_MODEL), lambda i: (i, 0)),
        out_shape=jax.ShapeDtypeStruct((T, D_MODEL), F32),
        compiler_params=_cparams(("arbitrary",)),
        name="combine",
    )(yk, x1, ri, g)


def _pad_lanes(a, width=LANES):
    return jnp.pad(a, ((0, 0), (0, width - a.shape[-1])))


def _layer(x2, batch, seq, g_mix, w_in, conv_w, conv_b, w_rg_a, b_rg_a, w_rg_x, b_rg_x, lam, b_forget,
           w_lru_out, w_attn_out, w_out, g_ffn, w_rgrp, b_rgrp, w_rexp, b_rexp, w1, w3, w2):
    T = batch * seq
    row = lambda a: a.reshape(1, -1).astype(F32)

    fl0 = 5 * D_MODEL
    w_main = jnp.concatenate([w_in[:, :fl0], w_in[:, fl0 + N_HEADS:]], axis=1).astype(BF16)
    w_fl = _pad_lanes(w_in[:, fl0:fl0 + N_HEADS]).astype(BF16)
    z, fl = _inproj(x2, row(g_mix), w_main, w_fl)

    caug = _forget(fl, _pad_lanes(row(b_forget)), batch=batch, seq=seq)
    o = _attention(z, caug, batch=batch, seq=seq)

    hg = _lru(z, conv_w.astype(F32), row(conv_b), w_rg_a.astype(BF16), w_rg_x.astype(BF16),
              row(b_rg_a), row(b_rg_x), row(lam), batch=batch, seq=seq)

    wr = _pad_lanes(jnp.concatenate([w_rgrp, w_rexp], axis=1)).astype(BF16)
    br = _pad_lanes(jnp.concatenate([row(b_rgrp), row(b_rexp)], axis=1))
    x1, v, ri, rt, cnt = _merge(hg, o, z, x2, w_lru_out.astype(BF16), w_attn_out.astype(BF16),
                            w_out.astype(BF16), row(g_ffn), wr, br)

    e = rt[0:TOP_K].astype(jnp.int32)
    rank = rt[4:4 + TOP_K].astype(jnp.int32)
    counts = cnt[0, R_EXP0:R_EXP0 + N_EXPERTS].astype(jnp.int32)
    padded = (counts + EXPERT_ROWS - 1) // EXPERT_ROWS * EXPERT_ROWS
    pad_end = jnp.cumsum(padded)
    pad_start = pad_end - padded
    seg_start = jnp.zeros_like(e)
    for j in range(N_EXPERTS):
        seg_start = jnp.where(e == j, pad_start[j], seg_start)
    dest2 = (seg_start + rank).astype(jnp.int32)
    n_blocks = (T * TOP_K + N_EXPERTS * (EXPERT_ROWS - 1) + EXPERT_ROWS - 1) // EXPERT_ROWS
    blk_start = jnp.arange(n_blocks, dtype=jnp.int32) * EXPERT_ROWS
    blk_expert = jnp.minimum(
        jnp.sum((pad_end[None, :] <= blk_start[:, None]).astype(jnp.int32), axis=1), N_EXPERTS - 1)
    n_used = (pad_end[-1:] // EXPERT_ROWS).astype(jnp.int32)

    fill = jnp.concatenate([pad_start + counts, padded - counts, n_used]).astype(jnp.int32)
    dests = [dest2[k] for k in range(TOP_K)]
    xs = _sc_scatter_rows(v.reshape(T, ROW_CHUNKS, LANES), dests, n_blocks * EXPERT_ROWS)
    xs = _fill_padding(fill, xs.reshape(-1, LANES))
    y = _experts(blk_expert, n_used, xs, w1, w3, w2)
    yk = _sc_gather_rows(y.reshape(-1, ROW_CHUNKS, LANES), dests)
    return x1, yk.reshape(TOP_K, T * ROW_CHUNKS, LANES), ri


def kernel(x, g_mix, w_in, conv_w, conv_b, w_rg_a, b_rg_a, w_rg_x, b_rg_x, lru_lambda, b_forget, w_lru_out, w_attn_out, w_out, g_ffn, w_route_group, b_route_group, w_route_expert, b_route_expert, w_exp_gate, w_exp_up, w_exp_down, g_final):
    batch, seq, _ = x.shape
    depth = g_mix.shape[0]
    x2 = x.reshape(batch * seq, D_MODEL)
    for l in range(depth):
        x1, yk, ri = _layer(
            x2, batch, seq, g_mix[l], w_in[l], conv_w[l], conv_b[l], w_rg_a[l], b_rg_a[l], w_rg_x[l],
            b_rg_x[l], lru_lambda[l], b_forget[l], w_lru_out[l], w_attn_out[l], w_out[l], g_ffn[l],
            w_route_group[l], b_route_group[l], w_route_expert[l], b_route_expert[l],
            w_exp_gate[l], w_exp_up[l], w_exp_down[l])
        x2 = _combine(yk, x1, ri, g_final.reshape(1, -1).astype(F32), final_norm=l == depth - 1)
    return x2.reshape(batch, seq, D_MODEL)
```

```python
import functools

import jax
import jax.numpy as jnp
from jax import lax
from jax.experimental import pallas as pl
from jax.experimental.pallas import tpu as pltpu
from jax.experimental.pallas import tpu_sc as plsc

F32 = jnp.float32
BF16 = jnp.bfloat16

D_MODEL = 1024
LRU_BLOCK_W = 256
LRU_BLOCKS = D_MODEL // LRU_BLOCK_W
CONV_W = 4
LRU_C = 8.0
N_HEADS = 8
HEAD_DIM = D_MODEL // N_HEADS
N_GROUPS = 4
EXPERTS_PER_GROUP = 8
N_EXPERTS = N_GROUPS * EXPERTS_PER_GROUP
TOP_K = 2
D_EXPERT = D_MODEL // 2
EPS = 1e-6

LANES = 128
SUBLANES = 8
ROW_CHUNKS = D_MODEL // LANES
VMEM_LIMIT = 48 * 1024 * 1024

ZC_XR, ZC_GR, ZC_Q, ZC_K, ZC_V, ZC_ML, ZC_MA = 0, 8, 16, 24, 32, 40, 48
Z_WIDTH = 7 * D_MODEL

R_EXP0 = N_GROUPS

EXPERT_ROWS = 256


def _cparams(sem):
    return pltpu.CompilerParams(dimension_semantics=sem, vmem_limit_bytes=VMEM_LIMIT)


LOG2E = 1.4426950408889634
Q_PRESCALE = HEAD_DIM ** -0.5 * LOG2E


def _inproj_body(x_ref, g_ref, w_ref, wfl_ref, z_ref, fl_ref, u_ref, *, q_block):
    j = pl.program_id(1)

    @pl.when(j == 0)
    def _():
        x = x_ref[...]
        ms = jnp.mean(x * x, axis=-1, keepdims=True)
        u = (x * lax.rsqrt(ms + EPS) * g_ref[...]).astype(BF16)
        u_ref[...] = u
        fl_ref[...] = jnp.dot(u, wfl_ref[...], preferred_element_type=F32)

    @pl.when(j == q_block)
    def _():
        acc = jnp.dot(u_ref[...], w_ref[...], preferred_element_type=F32)
        z_ref[...] = (acc * Q_PRESCALE).astype(BF16)

    @pl.when(j != q_block)
    def _():
        z_ref[...] = jnp.dot(u_ref[...], w_ref[...], preferred_element_type=F32).astype(BF16)


def _inproj(x2, g, w_main, w_fl, *, tm=1024, tn=D_MODEL):
    T = x2.shape[0]
    return pl.pallas_call(
        functools.partial(_inproj_body, q_block=ZC_Q * LANES // tn),
        grid=(T // tm, Z_WIDTH // tn),
        in_specs=[
            pl.BlockSpec((tm, D_MODEL), lambda i, j: (i, 0)),
            pl.BlockSpec((1, D_MODEL), lambda i, j: (0, 0)),
            pl.BlockSpec((D_MODEL, tn), lambda i, j: (0, j)),
            pl.BlockSpec((D_MODEL, LANES), lambda i, j: (0, 0)),
        ],
        out_specs=[
            pl.BlockSpec((tm, tn), lambda i, j: (i, j)),
            pl.BlockSpec((tm, LANES), lambda i, j: (i, 0)),
        ],
        out_shape=[
            jax.ShapeDtypeStruct((T, Z_WIDTH), BF16),
            jax.ShapeDtypeStruct((T, LANES), F32),
        ],
        scratch_shapes=[pltpu.VMEM((tm, D_MODEL), BF16)],
        compiler_params=_cparams(("arbitrary", "arbitrary")),
        name="inproj",
    )(x2, g, w_main, w_fl)


C_TERMS = 3


def _log_sigmoid(z):
    return jnp.minimum(z, 0.0) - jnp.log1p(jnp.exp(-jnp.abs(z)))


def _forget_body(fl_ref, b_ref, c_ref, *, seq):
    lf = _log_sigmoid(fl_ref[...] + b_ref[...])
    row = lax.broadcasted_iota(jnp.int32, lf.shape, 0)
    lane = lax.broadcasted_iota(jnp.int32, lf.shape, 1)
    c = lf
    k = 1
    while k < seq:
        c = c + jnp.where(row >= k, pltpu.roll(c, k, axis=0), 0.0)
        k *= 2
    rem = c * LOG2E
    out = jnp.zeros(lf.shape, F32)
    for n in range(C_TERMS):
        t = rem.astype(BF16).astype(F32)
        rem = rem - t
        shifted = t if n == 0 else pltpu.roll(t, n * N_HEADS, axis=1)
        out = jnp.where((lane >= n * N_HEADS) & (lane < (n + 1) * N_HEADS), shifted, out)
    c_ref[...] = out.astype(BF16)


def _forget(fl, b_pad, *, batch, seq):
    return pl.pallas_call(
        functools.partial(_forget_body, seq=seq),
        grid=(batch,),
        in_specs=[
            pl.BlockSpec((seq, LANES), lambda b: (b, 0)),
            pl.BlockSpec((1, LANES), lambda b: (0, 0)),
        ],
        out_specs=pl.BlockSpec((seq, LANES), lambda b: (b, 0)),
        out_shape=jax.ShapeDtypeStruct((batch * seq, LANES), BF16),
        compiler_params=_cparams(("arbitrary",)),
        name="forget",
    )(fl, b_pad)


ATTN_HEADS_PER_STEP = 2


def _attn_body(q_ref, k_ref, v_ref, kc_ref, o_ref, *, tq):
    i = pl.program_id(2)
    lane = lax.broadcasted_iota(jnp.int32, (tq, LANES), 1)
    heads = range(ATTN_HEADS_PER_STEP)
    cols =[slice(g * HEAD_DIM, (g + 1) * HEAD_DIM) for g in heads]
    qs = []
    for g in heads:
        h = pl.program_id(1) * ATTN_HEADS_PER_STEP + g
        mine = (lane < C_TERMS * N_HEADS) & ((lane & (N_HEADS - 1)) == h)
        qc = jnp.where(mine, -1.0, 0.0).astype(BF16)
        qs.append(jnp.concatenate([q_ref[:, cols[g]], qc], axis=1))

    def step(j, carry, masked):
        off = pl.multiple_of(j * tq, tq)
        out = []
        for g in heads:
            m, l, acc = carry[g]
            kj = jnp.concatenate([k_ref[pl.ds(off, tq), cols[g]], kc_ref[pl.ds(off, tq), :]], axis=1)
            s = lax.dot_general(qs[g], kj, (((1,), (1,)), ((), ())), preferred_element_type=F32)
            if masked:
                r = lax.broadcasted_iota(jnp.int32, s.shape, 0)
                cidx = lax.broadcasted_iota(jnp.int32, s.shape, 1)
                s = jnp.where(cidx <= r, s, -jnp.inf)
            m_new = jnp.maximum(m, jnp.max(s, axis=-1, keepdims=True))
            p = jnp.exp2(s - m_new)
            alpha = jnp.exp2(m - m_new)
            l = alpha * l + jnp.sum(p, axis=-1, keepdims=True)
            acc = alpha * acc + jnp.dot(p.astype(BF16), v_ref[pl.ds(off, tq), cols[g]],
                                        preferred_element_type=F32)
            out.append((m_new, l, acc))
        return tuple(out)

    init = tuple((jnp.full((tq, 1), -jnp.inf, F32), jnp.zeros((tq, 1), F32), jnp.zeros((tq, HEAD_DIM), F32))
                 for _ in heads)
    carry = lax.fori_loop(0, i, lambda j, c: step(j, c, False), init)
    carry = step(i, carry, True)
    for g in heads:
        _, l, acc = carry[g]
        o_ref[:, cols[g]] = (acc / l).astype(BF16)


def _attention(z, caug, *, batch, seq, tq=512):
    nq = seq // tq
    T = batch * seq
    G = ATTN_HEADS_PER_STEP
    W = G * HEAD_DIM
    return pl.pallas_call(
        functools.partial(_attn_body, tq=tq),
        grid=(batch, N_HEADS // G, nq),
        in_specs=[
            pl.BlockSpec((tq, W), lambda b, h, i: (b * nq + i, ZC_Q // G + h)),
            pl.BlockSpec((seq, W), lambda b, h, i: (b, ZC_K // G + h)),
            pl.BlockSpec((seq, W), lambda b, h, i: (b, ZC_V // G + h)),
            pl.BlockSpec((seq, LANES), lambda b, h, i: (b, 0)),
        ],
        out_specs=pl.BlockSpec((tq, W), lambda b, h, i: (b * nq + i, h)),
        out_shape=jax.ShapeDtypeStruct((T, D_MODEL), BF16),
        compiler_params=_cparams(("arbitrary", "arbitrary", "arbitrary")),
        name="attn",
    )(z, z, z, caug)


def _lru_body(xr_ref, gr_ref, cw_ref, cb_ref, wa_ref, wx_ref, ba_ref, bx_ref, lam_ref,
              hg_ref, xbuf, hcar, *, ts):
    i = pl.program_id(1)

    @pl.when(i == 0)
    def _():
        xbuf[0:SUBLANES, :] = jnp.zeros((SUBLANES, D_MODEL), F32)
        hcar[...] = jnp.zeros_like(hcar)

    @pl.when(i > 0)
    def _():
        xbuf[0:SUBLANES, :] = xbuf[ts:ts + SUBLANES, :]

    xbuf[SUBLANES:ts + SUBLANES, :] = xr_ref[...].astype(F32)

    base = SUBLANES - (CONV_W - 1)
    xc = cw_ref[0:1, :] * xbuf[base:base + ts, :]
    for k in range(1, CONV_W):
        xc = xc + cw_ref[k:k + 1, :] * xbuf[base + k:base + k + ts, :]
    xc = xc + cb_ref[...]

    xcb = xc.astype(BF16)
    ra = jnp.concatenate(
        [jnp.dot(xcb[:, n * LRU_BLOCK_W:(n + 1) * LRU_BLOCK_W], wa_ref[n], preferred_element_type=F32)
         for n in range(LRU_BLOCKS)], axis=-1)
    rx = jnp.concatenate(
        [jnp.dot(xcb[:, n * LRU_BLOCK_W:(n + 1) * LRU_BLOCK_W], wx_ref[n], preferred_element_type=F32)
         for n in range(LRU_BLOCKS)], axis=-1)
    r = jax.nn.sigmoid(ra + ba_ref[...])
    ig = jax.nn.sigmoid(rx + bx_ref[...])
    nlam = -lam_ref[...]
    softplus = jnp.maximum(nlam, 0.0) + jnp.log1p(jnp.exp(-jnp.abs(nlam)))
    log_a = (-LRU_C * r) * softplus
    a = jnp.exp(log_a)
    th = jnp.tanh(log_a)
    mult = jnp.sqrt(-2.0 * th / (1.0 - th))
    b = mult * ig * xc

    row = lax.broadcasted_iota(jnp.int32, (SUBLANES, D_MODEL), 0)
    keeps = [(k, row >= k) for k in (1, 2, 4)]
    hprev = jnp.broadcast_to(hcar[...], (SUBLANES, D_MODEL))
    pieces = []
    for j in range(ts // SUBLANES):
        aj = a[j * SUBLANES:(j + 1) * SUBLANES, :]
        bj = b[j * SUBLANES:(j + 1) * SUBLANES, :]
        for k, keep in keeps:
            a_sh = jnp.where(keep, pltpu.roll(aj, k, axis=0), 1.0)
            b_sh = jnp.where(keep, pltpu.roll(bj, k, axis=0), 0.0)
            bj = aj * b_sh + bj
            aj = aj * a_sh
        hj = bj + aj * hprev
        hprev = jnp.broadcast_to(hj[SUBLANES - 1:SUBLANES, :], (SUBLANES, D_MODEL))
        pieces.append(hj)
    h = jnp.concatenate(pieces, axis=0)
    hcar[...] = h[ts - 1:ts, :]

    hg_ref[...] = (h * jax.nn.gelu(gr_ref[...].astype(F32))).astype(BF16)


def _lru(z, conv_w, conv_b, wa, wx, ba, bx, lam, *, batch, seq, ts=256):
    ns = seq // ts
    T = batch * seq
    full = lambda shape: pl.BlockSpec(shape, lambda b, i: (0,) * len(shape))
    return pl.pallas_call(
        functools.partial(_lru_body, ts=ts),
        grid=(batch, ns),
        in_specs=[
            pl.BlockSpec((ts, D_MODEL), lambda b, i: (b * ns + i, ZC_XR // SUBLANES)),
            pl.BlockSpec((ts, D_MODEL), lambda b, i: (b * ns + i, ZC_GR // SUBLANES)),
            full((CONV_W, D_MODEL)),
            full((1, D_MODEL)),
            full((LRU_BLOCKS, LRU_BLOCK_W, LRU_BLOCK_W)),
            full((LRU_BLOCKS, LRU_BLOCK_W, LRU_BLOCK_W)),
            full((1, D_MODEL)),
            full((1, D_MODEL)),
            full((1, D_MODEL)),
        ],
        out_specs=pl.BlockSpec((ts, D_MODEL), lambda b, i: (b * ns + i, 0)),
        out_shape=jax.ShapeDtypeStruct((T, D_MODEL), BF16),
        scratch_shapes=[pltpu.VMEM((ts + SUBLANES, D_MODEL), F32), pltpu.VMEM((1, D_MODEL), F32)],
        compiler_params=_cparams(("arbitrary", "arbitrary")),
        name="lru",
    )(z, z, conv_w, conv_b, wa, wx, ba, bx, lam)


MERGE_SUBTILES = 1
MOE_PARTS = 2


def _merge_body(hg_ref, o_ref, ml_ref, ma_ref, x_ref, wl_ref, wat_ref, wo_ref, g_ref, wr_ref, br_ref,
                x1_ref, v_ref, ri_ref, rt_ref, cnt_ref, carry_ref, *, tm, steps_per_part):
    @pl.when(pl.program_id(0) % steps_per_part == 0)
    def _():
        carry_ref[...] = jnp.zeros_like(carry_ref)

    ts = tm // MERGE_SUBTILES
    lane = lax.broadcasted_iota(jnp.int32, (ts, LANES), 1)
    rr = lax.broadcasted_iota(jnp.int32, (ts, ts), 0)
    cc = lax.broadcasted_iota(jnp.int32, (ts, ts), 1)
    tri = jnp.where(cc < rr, 1.0, 0.0).astype(BF16)
    ninf = -jnp.inf
    big = jnp.int32(1 << 20)
    carry = carry_ref[...]

    for sub in range(MERGE_SUBTILES):
        rows = slice(sub * ts, (sub + 1) * ts)
        yl = jnp.dot(hg_ref[rows, :], wl_ref[...], preferred_element_type=F32)
        ya = jnp.dot(o_ref[rows, :], wat_ref[...], preferred_element_type=F32)
        merged = (jax.nn.sigmoid(ml_ref[rows, :].astype(F32)) * yl
                  + jax.nn.sigmoid(ma_ref[rows, :].astype(F32)) * ya)
        x1 = x_ref[rows, :] + jnp.dot(merged.astype(BF16), wo_ref[...], preferred_element_type=F32)
        x1_ref[rows, :] = x1
        ms = jnp.mean(x1 * x1, axis=-1, keepdims=True)
        v = x1 * lax.rsqrt(ms + EPS) * g_ref[...]
        for s in range(ROW_CHUNKS):
            v_ref[pl.ds(sub * ts * ROW_CHUNKS + s, ts, stride=ROW_CHUNKS), :] = v[:, s * LANES:(s + 1) * LANES]

        logits = jnp.dot(v.astype(BF16), wr_ref[...], preferred_element_type=F32) + br_ref[...]

        gl = jnp.where(lane < N_GROUPS, logits, ninf)
        gmax = jnp.max(gl, axis=-1, keepdims=True)
        gsel = jnp.min(jnp.where(gl == gmax, lane, big), axis=-1, keepdims=True)
        pg = 1.0 / jnp.sum(jnp.exp(gl - gmax), axis=-1, keepdims=True)

        lo = R_EXP0 + gsel * EXPERTS_PER_GROUP
        el = jnp.where(lane >= lo, jnp.where(lane < lo + EXPERTS_PER_GROUP, logits, ninf), ninf)
        v1 = jnp.max(el, axis=-1, keepdims=True)
        i1 = jnp.min(jnp.where(el == v1, lane, big), axis=-1, keepdims=True)
        el2 = jnp.where(lane == i1, ninf, el)
        v2 = jnp.max(el2, axis=-1, keepdims=True)
        i2 = jnp.min(jnp.where(el2 == v2, lane, big), axis=-1, keepdims=True)
        e21 = jnp.exp(v2 - v1)
        p1 = 1.0 / (1.0 + e21)
        w0 = pg * p1
        w1 = pg * (e21 * p1)

        hit0 = lane == i1
        hit1 = lane == i2
        onehot = jnp.where(hit0, 1.0, jnp.where(hit1, 1.0, 0.0))
        before = jnp.dot(tri, onehot.astype(BF16), preferred_element_type=F32) + carry
        rank0 = jnp.sum(jnp.where(hit0, before, 0.0), axis=-1, keepdims=True)
        rank1 = jnp.sum(jnp.where(hit1, before, 0.0), axis=-1, keepdims=True)
        carry = carry + jnp.sum(onehot, axis=0, keepdims=True)

        e0 = (i1 - R_EXP0).astype(F32)
        e1 = (i2 - R_EXP0).astype(F32)
        ri = jnp.where(lane == 0, e0,
             jnp.where(lane == 1, e1,
             jnp.where(lane == 2, w0,
             jnp.where(lane == 3, w1,
             jnp.where(lane == 4, rank0,
             jnp.where(lane == 5, rank1, 0.0))))))
        ri_ref[rows, :] = ri
        rt_ref[:, rows] = ri.T[:SUBLANES, :]

    carry_ref[...] = carry
    cnt_ref[0] = carry


def _merge(hg, o, z, x2, wl, wat, wo, g, wr, br, *, tm=512):
    T = x2.shape[0]
    steps_per_part = T // tm // MOE_PARTS
    full = lambda shape: pl.BlockSpec(shape, lambda i: (0,) * len(shape))
    return pl.pallas_call(
        functools.partial(_merge_body, tm=tm, steps_per_part=steps_per_part),
        grid=(T // tm,),
        in_specs=[
            pl.BlockSpec((tm, D_MODEL), lambda i: (i, 0)),
            pl.BlockSpec((tm, D_MODEL), lambda i: (i, 0)),
            pl.BlockSpec((tm, D_MODEL), lambda i: (i, ZC_ML // SUBLANES)),
            pl.BlockSpec((tm, D_MODEL), lambda i: (i, ZC_MA // SUBLANES)),
            pl.BlockSpec((tm, D_MODEL), lambda i: (i, 0)),
            full((D_MODEL, D_MODEL)),
            full((D_MODEL, D_MODEL)),
            full((D_MODEL, D_MODEL)),
            full((1, D_MODEL)),
            full((D_MODEL, LANES)),
            full((1, LANES)),
        ],
        out_specs=[
            pl.BlockSpec((tm, D_MODEL), lambda i: (i, 0)),
            pl.BlockSpec((tm * ROW_CHUNKS, LANES), lambda i: (i, 0)),
            pl.BlockSpec((tm, LANES), lambda i: (i, 0)),
            pl.BlockSpec((SUBLANES, tm), lambda i: (0, i)),
            pl.BlockSpec((1, 1, LANES), lambda i: (i // steps_per_part, 0, 0)),
        ],
        out_shape=[
            jax.ShapeDtypeStruct((T, D_MODEL), F32),
            jax.ShapeDtypeStruct((T * ROW_CHUNKS, LANES), F32),
            jax.ShapeDtypeStruct((T, LANES), F32),
            jax.ShapeDtypeStruct((SUBLANES, T), F32),
            jax.ShapeDtypeStruct((MOE_PARTS, 1, LANES), F32),
        ],
        scratch_shapes=[pltpu.VMEM((1, LANES), F32)],
        compiler_params=_cparams(("arbitrary",)),
        name="merge",
    )(hg, o, z, z, x2, wl, wat, wo, g, wr, br)


def _fill_padding_body(fill_ref, xs_in_ref, xs_ref, zblock, zsem, *, n_blocks):
    del xs_in_ref

    def for_each_padding_piece(fn):
        def per_expert(e, carry):
            start = fill_ref[e]
            n = fill_ref[N_EXPERTS + e]
            size = EXPERT_ROWS // 2
            while size >= 1:
                take = (n & size) != 0

                @pl.when(take)
                def _(start=start, size=size):
                    dst = pl.ds(pl.multiple_of(start * ROW_CHUNKS, ROW_CHUNKS), size * ROW_CHUNKS)
                    fn(pltpu.make_async_copy(zblock.at[pl.ds(0, size * ROW_CHUNKS)], xs_ref.at[dst], zsem))

                start = start + jnp.where(take, size, 0)
                size //= 2
            return carry
        lax.fori_loop(0, N_EXPERTS, per_expert, 0)

    def zero_block_copy(blk):
        start = pl.multiple_of(blk * (EXPERT_ROWS * ROW_CHUNKS), EXPERT_ROWS * ROW_CHUNKS)
        return pltpu.make_async_copy(zblock, xs_ref.at[pl.ds(start, EXPERT_ROWS * ROW_CHUNKS)], zsem)

    def for_each_unused_block(fn):
        lax.fori_loop(fill_ref[2 * N_EXPERTS], n_blocks, lambda blk, c: (fn(zero_block_copy(blk)), c)[1], 0)

    zblock[...] = jnp.zeros_like(zblock)
    for_each_padding_piece(lambda cp: cp.start())
    for_each_unused_block(lambda cp: cp.start())
    for_each_padding_piece(lambda cp: cp.wait())
    for_each_unused_block(lambda cp: cp.wait())


def _fill_padding(fill, xs):
    n_blocks = xs.shape[0] // (EXPERT_ROWS * ROW_CHUNKS)
    grid_spec = pltpu.PrefetchScalarGridSpec(
        num_scalar_prefetch=1,
        grid=(1,),
        in_specs=[pl.BlockSpec(memory_space=pl.ANY)],
        out_specs=pl.BlockSpec(memory_space=pl.ANY),
        scratch_shapes=[pltpu.VMEM((EXPERT_ROWS * ROW_CHUNKS, LANES), F32),
                        pltpu.SemaphoreType.DMA(())],
    )
    return pl.pallas_call(
        functools.partial(_fill_padding_body, n_blocks=n_blocks),
        grid_spec=grid_spec,
        out_shape=jax.ShapeDtypeStruct(xs.shape, F32),
        input_output_aliases={1: 0},
        compiler_params=_cparams(("arbitrary",)),
        name="fill_padding",
    )(fill, xs)


def _expert_body(be_ref, nu_ref, x_ref, w1_ref, w3_ref, w2_ref, y_ref, w13b, w2b, *, rows):
    i = pl.program_id(0)

    @pl.when((i == 0) | (be_ref[i] != be_ref[jnp.maximum(i - 1, 0)]))
    def _():
        w13b[:, :D_EXPERT] = w1_ref[0].astype(BF16)
        w13b[:, D_EXPERT:] = w3_ref[0].astype(BF16)
        w2b[...] = w2_ref[0].astype(BF16)

    @pl.when(i < nu_ref[0])
    def _():
        xb = jnp.concatenate(
            [x_ref[pl.ds(s, rows, stride=ROW_CHUNKS), :].astype(BF16) for s in range(ROW_CHUNKS)], axis=-1)
        gu = jnp.dot(xb, w13b[...], preferred_element_type=F32)
        hb = (jax.nn.silu(gu[:, :D_EXPERT]) * gu[:, D_EXPERT:]).astype(BF16)
        y = jnp.dot(hb, w2b[...], preferred_element_type=F32)
        for s in range(ROW_CHUNKS):
            y_ref[pl.ds(s, rows, stride=ROW_CHUNKS), :] = y[:, s * LANES:(s + 1) * LANES]

    @pl.when(i >= nu_ref[0])
    def _():
        y_ref[...] = jnp.zeros_like(y_ref)


def _experts(blk_expert, n_used, xs, w1, w3, w2, *, rows=EXPERT_ROWS):
    n_blocks = xs.shape[0] // (rows * ROW_CHUNKS)
    grid_spec = pltpu.PrefetchScalarGridSpec(
        num_scalar_prefetch=2,
        grid=(n_blocks,),
        in_specs=[
            pl.BlockSpec((rows * ROW_CHUNKS, LANES), lambda i, be, nu: (jnp.minimum(i, nu[0] - 1), 0)),
            pl.BlockSpec((1, D_MODEL, D_EXPERT), lambda i, be, nu: (be[i], 0, 0)),
            pl.BlockSpec((1, D_MODEL, D_EXPERT), lambda i, be, nu: (be[i], 0, 0)),
            pl.BlockSpec((1, D_EXPERT, D_MODEL), lambda i, be, nu: (be[i], 0, 0)),
        ],
        out_specs=pl.BlockSpec((rows * ROW_CHUNKS, LANES), lambda i, be, nu: (i, 0)),
        scratch_shapes=[pltpu.VMEM((D_MODEL, 2 * D_EXPERT), BF16), pltpu.VMEM((D_EXPERT, D_MODEL), BF16)],
    )
    return pl.pallas_call(
        functools.partial(_expert_body, rows=rows),
        grid_spec=grid_spec,
        out_shape=jax.ShapeDtypeStruct(xs.shape, F32),
        compiler_params=_cparams(("arbitrary",)),
        name="experts",
    )(blk_expert, n_used, xs, w1, w3, w2)


SC_CORES = 2
SC_SUBCORES = 16
SC_WORKERS = SC_CORES * SC_SUBCORES
SC_CHUNK = 32


def _sc_worker_base(per_worker):
    return (lax.axis_index("s") * SC_CORES + lax.axis_index("c")) * per_worker


def _sc_scatter_rows(v3, dests, n_rows, tok0):
    per_worker = dests[0].shape[0] // SC_WORKERS
    mesh = plsc.VectorSubcoreMesh(core_axis_name="c", subcore_axis_name="s")
    bufs = range(2)

    @functools.partial(
        pl.kernel, mesh=mesh,
        out_type=jax.ShapeDtypeStruct((n_rows, ROW_CHUNKS, LANES), F32),
        scratch_types=[pltpu.VMEM((2 * TOP_K, SC_CHUNK), jnp.int32),
                       pltpu.VMEM((2, SC_CHUNK, ROW_CHUNKS, LANES), F32),
                       pltpu.SemaphoreType.DMA((2,)),
                       pltpu.SemaphoreType.DMA((2,))],
        name="sc_scatter")
    def scatter(v_hbm, d0_hbm, d1_hbm, xs_hbm, idx, rows, lsem, ssem):
        base = _sc_worker_base(per_worker)

        @pl.loop(0, per_worker, step=2 * SC_CHUNK)
        def _(c):
            loads = []
            for b in bufs:
                off = base + c + b * SC_CHUNK
                loads.append(pltpu.async_copy(v_hbm.at[pl.ds(tok0 + off, SC_CHUNK)], rows.at[b], lsem.at[b]))
                for k, d_hbm in enumerate((d0_hbm, d1_hbm)):
                    pltpu.sync_copy(d_hbm.at[pl.ds(off, SC_CHUNK)], idx.at[b * TOP_K + k])
            stores = []
            for b in bufs:
                loads[b].wait()
                for k in range(TOP_K):
                    stores.append(pltpu.async_copy(rows.at[b], xs_hbm.at[idx.at[b * TOP_K + k]], ssem.at[b]))
            for st in stores:
                st.wait()

    return scatter(v3, *dests)


def _sc_gather_rows(y3, dests):
    T = dests[0].shape[0]
    per_worker = T // SC_WORKERS
    mesh = plsc.VectorSubcoreMesh(core_axis_name="c", subcore_axis_name="s")
    bufs = range(2)

    @functools.partial(
        pl.kernel, mesh=mesh,
        out_type=jax.ShapeDtypeStruct((TOP_K, T, ROW_CHUNKS, LANES), F32),
        scratch_types=[pltpu.VMEM((2, SC_CHUNK), jnp.int32),
                       pltpu.VMEM((2, SC_CHUNK, ROW_CHUNKS, LANES), F32),
                       pltpu.SemaphoreType.DMA((2,)),
                       pltpu.SemaphoreType.DMA((2,))],
        name="sc_gather")
    def gather(y_hbm, d0_hbm, d1_hbm, out_hbm, idx, rows, gsem, wsem):
        base = _sc_worker_base(per_worker)
        for k, d_hbm in enumerate((d0_hbm, d1_hbm)):
            @pl.loop(0, per_worker, step=2 * SC_CHUNK)
            def _(c):
                gathers = []
                for b in bufs:
                    off = base + c + b * SC_CHUNK
                    pltpu.sync_copy(d_hbm.at[pl.ds(off, SC_CHUNK)], idx.at[b])
                    gathers.append(pltpu.async_copy(y_hbm.at[idx.at[b]], rows.at[b], gsem.at[b]))
                writes = []
                for b in bufs:
                    off = base + c + b * SC_CHUNK
                    gathers[b].wait()
                    writes.append(pltpu.async_copy(rows.at[b], out_hbm.at[k, pl.ds(off, SC_CHUNK)], wsem.at[b]))
                for w in writes:
                    w.wait()

    return gather(y3, *dests)


def _combine_body(yk_ref, x1_ref, ri_ref, g_ref, *rest, tc, final_norm):
    out_ref = rest[-1]

    def rows_of(k):
        return jnp.concatenate(
            [yk_ref[k, pl.ds(s, tc, stride=ROW_CHUNKS), :] for s in range(ROW_CHUNKS)], axis=-1)

    ri = ri_ref[...]
    w0 = ri[:, 2:3]
    w1 = ri[:, 3:4]
    x2 = x1_ref[...] + (w0 * rows_of(0) + w1 * rows_of(1))
    if final_norm:
        ms = jnp.mean(x2 * x2, axis=-1, keepdims=True)
        x2 = x2 * lax.rsqrt(ms + EPS) * g_ref[...]
    out_ref[...] = x2


def _combine(yk, x1, ri, g, out_prev, *, tok0, final_norm, tc=256):
    T = x1.shape[0]
    steps = yk.shape[1] // (tc * ROW_CHUNKS)
    blk0 = tok0 // tc
    in_specs = [
        pl.BlockSpec((TOP_K, tc * ROW_CHUNKS, LANES), lambda i: (0, i, 0)),
        pl.BlockSpec((tc, D_MODEL), lambda i: (i + blk0, 0)),
        pl.BlockSpec((tc, LANES), lambda i: (i + blk0, 0)),
        pl.BlockSpec((1, D_MODEL), lambda i: (0, 0)),
    ]
    args = [yk, x1, ri, g]
    aliases = {}
    if out_prev is not None:
        in_specs.append(pl.BlockSpec(memory_space=pl.ANY))
        args.append(out_prev)
        aliases = {len(args) - 1: 0}
    return pl.pallas_call(
        functools.partial(_combine_body, tc=tc, final_norm=final_norm),
        grid=(steps,),
        in_specs=in_specs,
        out_specs=pl.BlockSpec((tc, D_MODEL), lambda i: (i + blk0, 0)),
        out_shape=jax.ShapeDtypeStruct((T, D_MODEL), F32),
        input_output_aliases=aliases,
        compiler_params=_cparams(("arbitrary",)),
        name="combine",
    )(*args)


def _pad_lanes(a, width=LANES):
    return jnp.pad(a, ((0, 0), (0, width - a.shape[-1])))


def _layer(x2, batch, seq, g_mix, w_in, conv_w, conv_b, w_rg_a, b_rg_a, w_rg_x, b_rg_x, lam, b_forget,
           w_lru_out, w_attn_out, w_out, g_ffn, w_rgrp, b_rgrp, w_rexp, b_rexp, w1, w3, w2, g_out, final_norm):
    T = batch * seq
    row = lambda a: a.reshape(1, -1).astype(F32)

    fl0 = 5 * D_MODEL
    w_main = jnp.concatenate([w_in[:, :fl0], w_in[:, fl0 + N_HEADS:]], axis=1).astype(BF16)
    w_fl = _pad_lanes(w_in[:, fl0:fl0 + N_HEADS]).astype(BF16)
    z, fl = _inproj(x2, row(g_mix), w_main, w_fl)

    caug = _forget(fl, _pad_lanes(row(b_forget)), batch=batch, seq=seq)
    o = _attention(z, caug, batch=batch, seq=seq)

    hg = _lru(z, conv_w.astype(F32), row(conv_b), w_rg_a.astype(BF16), w_rg_x.astype(BF16),
              row(b_rg_a), row(b_rg_x), row(lam), batch=batch, seq=seq)

    wr = _pad_lanes(jnp.concatenate([w_rgrp, w_rexp], axis=1)).astype(BF16)
    br = _pad_lanes(jnp.concatenate([row(b_rgrp), row(b_rexp)], axis=1))
    x1, v, ri, rt, cnt = _merge(hg, o, z, x2, w_lru_out.astype(BF16), w_attn_out.astype(BF16),
                            w_out.astype(BF16), row(g_ffn), wr, br)

    v3 = v.reshape(T, ROW_CHUNKS, LANES)
    g_out = row(g_out)
    Tp = T // MOE_PARTS
    n_blocks = (Tp * TOP_K + N_EXPERTS * (EXPERT_ROWS - 1) + EXPERT_ROWS - 1) // EXPERT_ROWS
    blk_start = jnp.arange(n_blocks, dtype=jnp.int32) * EXPERT_ROWS
    out = None
    for part in range(MOE_PARTS):
        tok0 = part * Tp
        e = rt[0:TOP_K, tok0:tok0 + Tp].astype(jnp.int32)
        rank = rt[4:4 + TOP_K, tok0:tok0 + Tp].astype(jnp.int32)
        counts = cnt[part, 0, R_EXP0:R_EXP0 + N_EXPERTS].astype(jnp.int32)
        padded = (counts + EXPERT_ROWS - 1) // EXPERT_ROWS * EXPERT_ROWS
        pad_end = jnp.cumsum(padded)
        pad_start = pad_end - padded
        seg_start = jnp.zeros_like(e)
        for j in range(N_EXPERTS):
            seg_start = jnp.where(e == j, pad_start[j], seg_start)
        dest2 = (seg_start + rank).astype(jnp.int32)
        dests = [dest2[k] for k in range(TOP_K)]
        blk_expert = jnp.minimum(
            jnp.sum((pad_end[None, :] <= blk_start[:, None]).astype(jnp.int32), axis=1), N_EXPERTS - 1)
        n_used = (pad_end[-1:] // EXPERT_ROWS).astype(jnp.int32)
        fill = jnp.concatenate([pad_start + counts, padded - counts, n_used]).astype(jnp.int32)

        xs = _sc_scatter_rows(v3, dests, n_blocks * EXPERT_ROWS, tok0)
        xs = _fill_padding(fill, xs.reshape(-1, LANES))
        y = _experts(blk_expert, n_used, xs, w1, w3, w2)
        yk = _sc_gather_rows(y.reshape(-1, ROW_CHUNKS, LANES), dests)
        out = _combine(yk.reshape(TOP_K, Tp * ROW_CHUNKS, LANES), x1, ri, g_out, out,
                       tok0=tok0, final_norm=final_norm)
    return out


def kernel(x, g_mix, w_in, conv_w, conv_b, w_rg_a, b_rg_a, w_rg_x, b_rg_x, lru_lambda, b_forget, w_lru_out, w_attn_out, w_out, g_ffn, w_route_group, b_route_group, w_route_expert, b_route_expert, w_exp_gate, w_exp_up, w_exp_down, g_final):
    batch, seq, _ = x.shape
    depth = g_mix.shape[0]
    x2 = x.reshape(batch * seq, D_MODEL)
    for l in range(depth):
        x2 = _layer(
            x2, batch, seq, g_mix[l], w_in[l], conv_w[l], conv_b[l], w_rg_a[l], b_rg_a[l], w_rg_x[l],
            b_rg_x[l], lru_lambda[l], b_forget[l], w_lru_out[l], w_attn_out[l], w_out[l], g_ffn[l],
            w_route_group[l], b_route_group[l], w_route_expert[l], b_route_expert[l],
            w_exp_gate[l], w_exp_up[l], w_exp_down[l], g_final, l == depth - 1)
    return x2.reshape(batch, seq, D_MODEL)
```

```python
import functools

import jax
import jax.numpy as jnp
from jax import lax
from jax.experimental import pallas as pl
from jax.experimental.pallas import tpu as pltpu
from jax.experimental.pallas import tpu_sc as plsc

F32 = jnp.float32
BF16 = jnp.bfloat16

D_MODEL = 1024
LRU_BLOCK_W = 256
LRU_BLOCKS = D_MODEL // LRU_BLOCK_W
CONV_W = 4
LRU_C = 8.0
N_HEADS = 8
HEAD_DIM = D_MODEL // N_HEADS
N_GROUPS = 4
EXPERTS_PER_GROUP = 8
N_EXPERTS = N_GROUPS * EXPERTS_PER_GROUP
TOP_K = 2
D_EXPERT = D_MODEL // 2
EPS = 1e-6

LANES = 128
SUBLANES = 8
ROW_CHUNKS = D_MODEL // LANES
VMEM_LIMIT = 48 * 1024 * 1024

ZC_XR, ZC_GR, ZC_Q, ZC_K, ZC_V, ZC_ML, ZC_MA = 0, 8, 16, 24, 32, 40, 48
Z_WIDTH = 7 * D_MODEL

R_EXP0 = N_GROUPS

EXPERT_ROWS = 256


def _cparams(sem):
    return pltpu.CompilerParams(dimension_semantics=sem, vmem_limit_bytes=VMEM_LIMIT)


LOG2E = 1.4426950408889634
Q_PRESCALE = HEAD_DIM ** -0.5 * LOG2E


def _inproj_body(x_ref, g_ref, w_ref, wfl_ref, z_ref, fl_ref, u_ref, *, q_block):
    j = pl.program_id(1)

    @pl.when(j == 0)
    def _():
        x = x_ref[...]
        ms = jnp.mean(x * x, axis=-1, keepdims=True)
        u = (x * lax.rsqrt(ms + EPS) * g_ref[...]).astype(BF16)
        u_ref[...] = u
        fl_ref[...] = jnp.dot(u, wfl_ref[...], preferred_element_type=F32)

    @pl.when(j == q_block)
    def _():
        acc = jnp.dot(u_ref[...], w_ref[...], preferred_element_type=F32)
        z_ref[...] = (acc * Q_PRESCALE).astype(BF16)

    @pl.when(j != q_block)
    def _():
        z_ref[...] = jnp.dot(u_ref[...], w_ref[...], preferred_element_type=F32).astype(BF16)


def _inproj(x2, g, w_main, w_fl, *, tm=2048, tn=D_MODEL):
    T = x2.shape[0]
    return pl.pallas_call(
        functools.partial(_inproj_body, q_block=ZC_Q * LANES // tn),
        grid=(T // tm, Z_WIDTH // tn),
        in_specs=[
            pl.BlockSpec((tm, D_MODEL), lambda i, j: (i, 0)),
            pl.BlockSpec((1, D_MODEL), lambda i, j: (0, 0)),
            pl.BlockSpec((D_MODEL, tn), lambda i, j: (0, j)),
            pl.BlockSpec((D_MODEL, LANES), lambda i, j: (0, 0)),
        ],
        out_specs=[
            pl.BlockSpec((tm, tn), lambda i, j: (i, j)),
            pl.BlockSpec((tm, LANES), lambda i, j: (i, 0)),
        ],
        out_shape=[
            jax.ShapeDtypeStruct((T, Z_WIDTH), BF16),
            jax.ShapeDtypeStruct((T, LANES), F32),
        ],
        scratch_shapes=[pltpu.VMEM((tm, D_MODEL), BF16)],
        compiler_params=_cparams(("arbitrary", "arbitrary")),
        name="inproj",
    )(x2, g, w_main, w_fl)


C_TERMS = 3


def _log_sigmoid(z):
    return jnp.minimum(z, 0.0) - jnp.log1p(jnp.exp(-jnp.abs(z)))


def _forget_body(fl_ref, b_ref, c_ref, *, seq):
    lf = _log_sigmoid(fl_ref[...] + b_ref[...])
    row = lax.broadcasted_iota(jnp.int32, lf.shape, 0)
    lane = lax.broadcasted_iota(jnp.int32, lf.shape, 1)
    c = lf
    k = 1
    while k < seq:
        c = c + jnp.where(row >= k, pltpu.roll(c, k, axis=0), 0.0)
        k *= 2
    rem = c * LOG2E
    out = jnp.zeros(lf.shape, F32)
    for n in range(C_TERMS):
        t = rem.astype(BF16).astype(F32)
        rem = rem - t
        shifted = t if n == 0 else pltpu.roll(t, n * N_HEADS, axis=1)
        out = jnp.where((lane >= n * N_HEADS) & (lane < (n + 1) * N_HEADS), shifted, out)
    c_ref[...] = out.astype(BF16)


def _forget(fl, b_pad, *, batch, seq):
    return pl.pallas_call(
        functools.partial(_forget_body, seq=seq),
        grid=(batch,),
        in_specs=[
            pl.BlockSpec((seq, LANES), lambda b: (b, 0)),
            pl.BlockSpec((1, LANES), lambda b: (0, 0)),
        ],
        out_specs=pl.BlockSpec((seq, LANES), lambda b: (b, 0)),
        out_shape=jax.ShapeDtypeStruct((batch * seq, LANES), BF16),
        compiler_params=_cparams(("arbitrary",)),
        name="forget",
    )(fl, b_pad)


ATTN_HEADS_PER_STEP = 4


def _attn_body(q_ref, k_ref, v_ref, kc_ref, o_ref, *, tq):
    i = pl.program_id(2)
    lane = lax.broadcasted_iota(jnp.int32, (tq, LANES), 1)
    heads = range(ATTN_HEADS_PER_STEP)
    cols =[slice(g * HEAD_DIM, (g + 1) * HEAD_DIM) for g in heads]
    qs = []
    for g in heads:
        h = pl.program_id(1) * ATTN_HEADS_PER_STEP + g
        mine = (lane < C_TERMS * N_HEADS) & ((lane & (N_HEADS - 1)) == h)
        qc = jnp.where(mine, -1.0, 0.0).astype(BF16)
        qs.append(jnp.concatenate([q_ref[:, cols[g]], qc], axis=1))

    def step(j, carry, masked):
        off = pl.multiple_of(j * tq, tq)
        out = []
        for g in heads:
            m, l, acc = carry[g]
            kj = jnp.concatenate([k_ref[pl.ds(off, tq), cols[g]], kc_ref[pl.ds(off, tq), :]], axis=1)
            s = lax.dot_general(qs[g], kj, (((1,), (1,)), ((), ())), preferred_element_type=F32)
            if masked:
                r = lax.broadcasted_iota(jnp.int32, s.shape, 0)
                cidx = lax.broadcasted_iota(jnp.int32, s.shape, 1)
                s = jnp.where(cidx <= r, s, -jnp.inf)
            m_new = jnp.maximum(m, jnp.max(s, axis=-1, keepdims=True))
            p = jnp.exp2(s - m_new)
            alpha = jnp.exp2(m - m_new)
            l = alpha * l + jnp.sum(p, axis=-1, keepdims=True)
            acc = alpha * acc + jnp.dot(p.astype(BF16), v_ref[pl.ds(off, tq), cols[g]],
                                        preferred_element_type=F32)
            out.append((m_new, l, acc))
        return tuple(out)

    init = tuple((jnp.full((tq, 1), -jnp.inf, F32), jnp.zeros((tq, 1), F32), jnp.zeros((tq, HEAD_DIM), F32))
                 for _ in heads)
    carry = lax.fori_loop(0, i, lambda j, c: step(j, c, False), init)
    carry = step(i, carry, True)
    for g in heads:
        _, l, acc = carry[g]
        o_ref[:, cols[g]] = (acc / l).astype(BF16)


def _attention(z, caug, *, batch, seq, tq=512):
    nq = seq // tq
    T = batch * seq
    G = ATTN_HEADS_PER_STEP
    W = G * HEAD_DIM
    return pl.pallas_call(
        functools.partial(_attn_body, tq=tq),
        grid=(batch, N_HEADS // G, nq),
        in_specs=[
            pl.BlockSpec((tq, W), lambda b, h, i: (b * nq + i, ZC_Q // G + h)),
            pl.BlockSpec((seq, W), lambda b, h, i: (b, ZC_K // G + h)),
            pl.BlockSpec((seq, W), lambda b, h, i: (b, ZC_V // G + h)),
            pl.BlockSpec((seq, LANES), lambda b, h, i: (b, 0)),
        ],
        out_specs=pl.BlockSpec((tq, W), lambda b, h, i: (b * nq + i, h)),
        out_shape=jax.ShapeDtypeStruct((T, D_MODEL), BF16),
        compiler_params=_cparams(("arbitrary", "arbitrary", "arbitrary")),
        name="attn",
    )(z, z, z, caug)


def _lru_body(xr_ref, gr_ref, cw_ref, cb_ref, wa_ref, wx_ref, ba_ref, bx_ref, lam_ref,
              hg_ref, xbuf, hcar, *, ts):
    i = pl.program_id(1)

    @pl.when(i == 0)
    def _():
        xbuf[0:SUBLANES, :] = jnp.zeros((SUBLANES, D_MODEL), F32)
        hcar[...] = jnp.zeros_like(hcar)

    @pl.when(i > 0)
    def _():
        xbuf[0:SUBLANES, :] = xbuf[ts:ts + SUBLANES, :]

    xbuf[SUBLANES:ts + SUBLANES, :] = xr_ref[...].astype(F32)

    base = SUBLANES - (CONV_W - 1)
    xc = cw_ref[0:1, :] * xbuf[base:base + ts, :]
    for k in range(1, CONV_W):
        xc = xc + cw_ref[k:k + 1, :] * xbuf[base + k:base + k + ts, :]
    xc = xc + cb_ref[...]

    xcb = xc.astype(BF16)
    ra = jnp.concatenate(
        [jnp.dot(xcb[:, n * LRU_BLOCK_W:(n + 1) * LRU_BLOCK_W], wa_ref[n], preferred_element_type=F32)
         for n in range(LRU_BLOCKS)], axis=-1)
    rx = jnp.concatenate(
        [jnp.dot(xcb[:, n * LRU_BLOCK_W:(n + 1) * LRU_BLOCK_W], wx_ref[n], preferred_element_type=F32)
         for n in range(LRU_BLOCKS)], axis=-1)
    r = jax.nn.sigmoid(ra + ba_ref[...])
    ig = jax.nn.sigmoid(rx + bx_ref[...])
    nlam = -lam_ref[...]
    softplus = jnp.maximum(nlam, 0.0) + jnp.log1p(jnp.exp(-jnp.abs(nlam)))
    log_a = (-LRU_C * r) * softplus
    a = jnp.exp(log_a)
    th = jnp.tanh(log_a)
    mult = jnp.sqrt(-2.0 * th / (1.0 - th))
    b = mult * ig * xc

    row = lax.broadcasted_iota(jnp.int32, (SUBLANES, D_MODEL), 0)
    keeps = [(k, row >= k) for k in (1, 2, 4)]
    hprev = jnp.broadcast_to(hcar[...], (SUBLANES, D_MODEL))
    pieces = []
    for j in range(ts // SUBLANES):
        aj = a[j * SUBLANES:(j + 1) * SUBLANES, :]
        bj = b[j * SUBLANES:(j + 1) * SUBLANES, :]
        for k, keep in keeps:
            a_sh = jnp.where(keep, pltpu.roll(aj, k, axis=0), 1.0)
            b_sh = jnp.where(keep, pltpu.roll(bj, k, axis=0), 0.0)
            bj = aj * b_sh + bj
            aj = aj * a_sh
        hj = bj + aj * hprev
        hprev = jnp.broadcast_to(hj[SUBLANES - 1:SUBLANES, :], (SUBLANES, D_MODEL))
        pieces.append(hj)
    h = jnp.concatenate(pieces, axis=0)
    hcar[...] = h[ts - 1:ts, :]

    hg_ref[...] = (h * jax.nn.gelu(gr_ref[...].astype(F32))).astype(BF16)


def _lru(z, conv_w, conv_b, wa, wx, ba, bx, lam, *, batch, seq, ts=256):
    ns = seq // ts
    T = batch * seq
    full = lambda shape: pl.BlockSpec(shape, lambda b, i: (0,) * len(shape))
    return pl.pallas_call(
        functools.partial(_lru_body, ts=ts),
        grid=(batch, ns),
        in_specs=[
            pl.BlockSpec((ts, D_MODEL), lambda b, i: (b * ns + i, ZC_XR // SUBLANES)),
            pl.BlockSpec((ts, D_MODEL), lambda b, i: (b * ns + i, ZC_GR // SUBLANES)),
            full((CONV_W, D_MODEL)),
            full((1, D_MODEL)),
            full((LRU_BLOCKS, LRU_BLOCK_W, LRU_BLOCK_W)),
            full((LRU_BLOCKS, LRU_BLOCK_W, LRU_BLOCK_W)),
            full((1, D_MODEL)),
            full((1, D_MODEL)),
            full((1, D_MODEL)),
        ],
        out_specs=pl.BlockSpec((ts, D_MODEL), lambda b, i: (b * ns + i, 0)),
        out_shape=jax.ShapeDtypeStruct((T, D_MODEL), BF16),
        scratch_shapes=[pltpu.VMEM((ts + SUBLANES, D_MODEL), F32), pltpu.VMEM((1, D_MODEL), F32)],
        compiler_params=_cparams(("arbitrary", "arbitrary")),
        name="lru",
    )(z, z, conv_w, conv_b, wa, wx, ba, bx, lam)


MERGE_SUBTILES = 1
MOE_PARTS = 2


def _merge_body(hg_ref, o_ref, ml_ref, ma_ref, x_ref, wl_ref, wat_ref, wo_ref, g_ref, wr_ref, br_ref,
                x1_ref, v_ref, ri_ref, rt_ref, cnt_ref, carry_ref, *, tm, steps_per_part):
    @pl.when(pl.program_id(0) % steps_per_part == 0)
    def _():
        carry_ref[...] = jnp.zeros_like(carry_ref)

    ts = tm // MERGE_SUBTILES
    lane = lax.broadcasted_iota(jnp.int32, (ts, LANES), 1)
    rr = lax.broadcasted_iota(jnp.int32, (ts, ts), 0)
    cc = lax.broadcasted_iota(jnp.int32, (ts, ts), 1)
    tri = jnp.where(cc < rr, 1.0, 0.0).astype(BF16)
    ninf = -jnp.inf
    big = jnp.int32(1 << 20)
    carry = carry_ref[...]

    for sub in range(MERGE_SUBTILES):
        rows = slice(sub * ts, (sub + 1) * ts)
        yl = jnp.dot(hg_ref[rows, :], wl_ref[...], preferred_element_type=F32)
        ya = jnp.dot(o_ref[rows, :], wat_ref[...], preferred_element_type=F32)
        merged = (jax.nn.sigmoid(ml_ref[rows, :].astype(F32)) * yl
                  + jax.nn.sigmoid(ma_ref[rows, :].astype(F32)) * ya)
        x1 = x_ref[rows, :] + jnp.dot(merged.astype(BF16), wo_ref[...], preferred_element_type=F32)
        x1_ref[rows, :] = x1
        ms = jnp.mean(x1 * x1, axis=-1, keepdims=True)
        v = x1 * lax.rsqrt(ms + EPS) * g_ref[...]
        for s in range(ROW_CHUNKS):
            v_ref[pl.ds(sub * ts * ROW_CHUNKS + s, ts, stride=ROW_CHUNKS), :] = v[:, s * LANES:(s + 1) * LANES]

        logits = jnp.dot(v.astype(BF16), wr_ref[...], preferred_element_type=F32) + br_ref[...]

        gl = jnp.where(lane < N_GROUPS, logits, ninf)
        gmax = jnp.max(gl, axis=-1, keepdims=True)
        gsel = jnp.min(jnp.where(gl == gmax, lane, big), axis=-1, keepdims=True)
        pg = 1.0 / jnp.sum(jnp.exp(gl - gmax), axis=-1, keepdims=True)

        lo = R_EXP0 + gsel * EXPERTS_PER_GROUP
        el = jnp.where(lane >= lo, jnp.where(lane < lo + EXPERTS_PER_GROUP, logits, ninf), ninf)
        v1 = jnp.max(el, axis=-1, keepdims=True)
        i1 = jnp.min(jnp.where(el == v1, lane, big), axis=-1, keepdims=True)
        el2 = jnp.where(lane == i1, ninf, el)
        v2 = jnp.max(el2, axis=-1, keepdims=True)
        i2 = jnp.min(jnp.where(el2 == v2, lane, big), axis=-1, keepdims=True)
        e21 = jnp.exp(v2 - v1)
        p1 = 1.0 / (1.0 + e21)
        w0 = pg * p1
        w1 = pg * (e21 * p1)

        hit0 = lane == i1
        hit1 = lane == i2
        onehot = jnp.where(hit0, 1.0, jnp.where(hit1, 1.0, 0.0))
        before = jnp.dot(tri, onehot.astype(BF16), preferred_element_type=F32) + carry
        rank0 = jnp.sum(jnp.where(hit0, before, 0.0), axis=-1, keepdims=True)
        rank1 = jnp.sum(jnp.where(hit1, before, 0.0), axis=-1, keepdims=True)
        carry = carry + jnp.sum(onehot, axis=0, keepdims=True)

        e0 = (i1 - R_EXP0).astype(F32)
        e1 = (i2 - R_EXP0).astype(F32)
        ri = jnp.where(lane == 0, e0,
             jnp.where(lane == 1, e1,
             jnp.where(lane == 2, w0,
             jnp.where(lane == 3, w1,
             jnp.where(lane == 4, rank0,
             jnp.where(lane == 5, rank1, 0.0))))))
        ri_ref[rows, :] = ri
        rt_ref[:, rows] = ri.T[:SUBLANES, :]

    carry_ref[...] = carry
    cnt_ref[0] = carry


def _merge(hg, o, z, x2, wl, wat, wo, g, wr, br, *, tm=512):
    T = x2.shape[0]
    steps_per_part = T // tm // MOE_PARTS
    full = lambda shape: pl.BlockSpec(shape, lambda i: (0,) * len(shape))
    return pl.pallas_call(
        functools.partial(_merge_body, tm=tm, steps_per_part=steps_per_part),
        grid=(T // tm,),
        in_specs=[
            pl.BlockSpec((tm, D_MODEL), lambda i: (i, 0)),
            pl.BlockSpec((tm, D_MODEL), lambda i: (i, 0)),
            pl.BlockSpec((tm, D_MODEL), lambda i: (i, ZC_ML // SUBLANES)),
            pl.BlockSpec((tm, D_MODEL), lambda i: (i, ZC_MA // SUBLANES)),
            pl.BlockSpec((tm, D_MODEL), lambda i: (i, 0)),
            full((D_MODEL, D_MODEL)),
            full((D_MODEL, D_MODEL)),
            full((D_MODEL, D_MODEL)),
            full((1, D_MODEL)),
            full((D_MODEL, LANES)),
            full((1, LANES)),
        ],
        out_specs=[
            pl.BlockSpec((tm, D_MODEL), lambda i: (i, 0)),
            pl.BlockSpec((tm * ROW_CHUNKS, LANES), lambda i: (i, 0)),
            pl.BlockSpec((tm, LANES), lambda i: (i, 0)),
            pl.BlockSpec((SUBLANES, tm), lambda i: (0, i)),
            pl.BlockSpec((1, 1, LANES), lambda i: (i // steps_per_part, 0, 0)),
        ],
        out_shape=[
            jax.ShapeDtypeStruct((T, D_MODEL), F32),
            jax.ShapeDtypeStruct((T * ROW_CHUNKS, LANES), F32),
            jax.ShapeDtypeStruct((T, LANES), F32),
            jax.ShapeDtypeStruct((SUBLANES, T), F32),
            jax.ShapeDtypeStruct((MOE_PARTS, 1, LANES), F32),
        ],
        scratch_shapes=[pltpu.VMEM((1, LANES), F32)],
        compiler_params=_cparams(("arbitrary",)),
        name="merge",
    )(hg, o, z, z, x2, wl, wat, wo, g, wr, br)


def _fill_padding_body(fill_ref, xs_in_ref, xs_ref, zblock, zsem, *, n_blocks):
    del xs_in_ref

    def for_each_padding_piece(fn):
        def per_expert(e, carry):
            start = fill_ref[e]
            n = fill_ref[N_EXPERTS + e]
            size = EXPERT_ROWS // 2
            while size >= 1:
                take = (n & size) != 0

                @pl.when(take)
                def _(start=start, size=size):
                    dst = pl.ds(pl.multiple_of(start * ROW_CHUNKS, ROW_CHUNKS), size * ROW_CHUNKS)
                    fn(pltpu.make_async_copy(zblock.at[pl.ds(0, size * ROW_CHUNKS)], xs_ref.at[dst], zsem))

                start = start + jnp.where(take, size, 0)
                size //= 2
            return carry
        lax.fori_loop(0, N_EXPERTS, per_expert, 0)

    def zero_block_copy(blk):
        start = pl.multiple_of(blk * (EXPERT_ROWS * ROW_CHUNKS), EXPERT_ROWS * ROW_CHUNKS)
        return pltpu.make_async_copy(zblock, xs_ref.at[pl.ds(start, EXPERT_ROWS * ROW_CHUNKS)], zsem)

    def for_each_unused_block(fn):
        lax.fori_loop(fill_ref[2 * N_EXPERTS], n_blocks, lambda blk, c: (fn(zero_block_copy(blk)), c)[1], 0)

    zblock[...] = jnp.zeros_like(zblock)
    for_each_padding_piece(lambda cp: cp.start())
    for_each_unused_block(lambda cp: cp.start())
    for_each_padding_piece(lambda cp: cp.wait())
    for_each_unused_block(lambda cp: cp.wait())


def _fill_padding(fill, xs):
    n_blocks = xs.shape[0] // (EXPERT_ROWS * ROW_CHUNKS)
    grid_spec = pltpu.PrefetchScalarGridSpec(
        num_scalar_prefetch=1,
        grid=(1,),
        in_specs=[pl.BlockSpec(memory_space=pl.ANY)],
        out_specs=pl.BlockSpec(memory_space=pl.ANY),
        scratch_shapes=[pltpu.VMEM((EXPERT_ROWS * ROW_CHUNKS, LANES), F32),
                        pltpu.SemaphoreType.DMA(())],
    )
    return pl.pallas_call(
        functools.partial(_fill_padding_body, n_blocks=n_blocks),
        grid_spec=grid_spec,
        out_shape=jax.ShapeDtypeStruct(xs.shape, F32),
        input_output_aliases={1: 0},
        compiler_params=_cparams(("arbitrary",)),
        name="fill_padding",
    )(fill, xs)


def _expert_body(be_ref, nu_ref, x_ref, w1_ref, w3_ref, w2_ref, y_ref, w13b, w2b, *, rows):
    i = pl.program_id(0)

    @pl.when((i == 0) | (be_ref[i] != be_ref[jnp.maximum(i - 1, 0)]))
    def _():
        w13b[:, :D_EXPERT] = w1_ref[0].astype(BF16)
        w13b[:, D_EXPERT:] = w3_ref[0].astype(BF16)
        w2b[...] = w2_ref[0].astype(BF16)

    @pl.when(i < nu_ref[0])
    def _():
        xb = jnp.concatenate(
            [x_ref[pl.ds(s, rows, stride=ROW_CHUNKS), :].astype(BF16) for s in range(ROW_CHUNKS)], axis=-1)
        gu = jnp.dot(xb, w13b[...], preferred_element_type=F32)
        hb = (jax.nn.silu(gu[:, :D_EXPERT]) * gu[:, D_EXPERT:]).astype(BF16)
        y = jnp.dot(hb, w2b[...], preferred_element_type=F32)
        for s in range(ROW_CHUNKS):
            y_ref[pl.ds(s, rows, stride=ROW_CHUNKS), :] = y[:, s * LANES:(s + 1) * LANES]

    @pl.when(i >= nu_ref[0])
    def _():
        y_ref[...] = jnp.zeros_like(y_ref)


def _experts(blk_expert, n_used, xs, w1, w3, w2, *, rows=EXPERT_ROWS):
    n_blocks = xs.shape[0] // (rows * ROW_CHUNKS)
    grid_spec = pltpu.PrefetchScalarGridSpec(
        num_scalar_prefetch=2,
        grid=(n_blocks,),
        in_specs=[
            pl.BlockSpec((rows * ROW_CHUNKS, LANES), lambda i, be, nu: (jnp.minimum(i, nu[0] - 1), 0)),
            pl.BlockSpec((1, D_MODEL, D_EXPERT), lambda i, be, nu: (be[i], 0, 0)),
            pl.BlockSpec((1, D_MODEL, D_EXPERT), lambda i, be, nu: (be[i], 0, 0)),
            pl.BlockSpec((1, D_EXPERT, D_MODEL), lambda i, be, nu: (be[i], 0, 0)),
        ],
        out_specs=pl.BlockSpec((rows * ROW_CHUNKS, LANES), lambda i, be, nu: (i, 0)),
        scratch_shapes=[pltpu.VMEM((D_MODEL, 2 * D_EXPERT), BF16), pltpu.VMEM((D_EXPERT, D_MODEL), BF16)],
    )
    return pl.pallas_call(
        functools.partial(_expert_body, rows=rows),
        grid_spec=grid_spec,
        out_shape=jax.ShapeDtypeStruct(xs.shape, F32),
        compiler_params=_cparams(("arbitrary",)),
        name="experts",
    )(blk_expert, n_used, xs, w1, w3, w2)


SC_CORES = 2
SC_SUBCORES = 16
SC_WORKERS = SC_CORES * SC_SUBCORES
SC_CHUNK = 32


def _sc_worker_base(per_worker):
    return (lax.axis_index("s") * SC_CORES + lax.axis_index("c")) * per_worker


def _sc_scatter_rows(v3, dests, n_rows, tok0):
    per_worker = dests[0].shape[0] // SC_WORKERS
    mesh = plsc.VectorSubcoreMesh(core_axis_name="c", subcore_axis_name="s")
    bufs = range(2)

    @functools.partial(
        pl.kernel, mesh=mesh,
        out_type=jax.ShapeDtypeStruct((n_rows, ROW_CHUNKS, LANES), F32),
        scratch_types=[pltpu.VMEM((2 * TOP_K, SC_CHUNK), jnp.int32),
                       pltpu.VMEM((2, SC_CHUNK, ROW_CHUNKS, LANES), F32),
                       pltpu.SemaphoreType.DMA((2,)),
                       pltpu.SemaphoreType.DMA((2,))],
        name="sc_scatter")
    def scatter(v_hbm, d0_hbm, d1_hbm, xs_hbm, idx, rows, lsem, ssem):
        base = _sc_worker_base(per_worker)

        @pl.loop(0, per_worker, step=2 * SC_CHUNK)
        def _(c):
            loads = []
            for b in bufs:
                off = base + c + b * SC_CHUNK
                loads.append(pltpu.async_copy(v_hbm.at[pl.ds(tok0 + off, SC_CHUNK)], rows.at[b], lsem.at[b]))
                for k, d_hbm in enumerate((d0_hbm, d1_hbm)):
                    pltpu.sync_copy(d_hbm.at[pl.ds(off, SC_CHUNK)], idx.at[b * TOP_K + k])
            stores = []
            for b in bufs:
                loads[b].wait()
                for k in range(TOP_K):
                    stores.append(pltpu.async_copy(rows.at[b], xs_hbm.at[idx.at[b * TOP_K + k]], ssem.at[b]))
            for st in stores:
                st.wait()

    return scatter(v3, *dests)


def _sc_gather_rows(y3, dests):
    T = dests[0].shape[0]
    per_worker = T // SC_WORKERS
    mesh = plsc.VectorSubcoreMesh(core_axis_name="c", subcore_axis_name="s")
    bufs = range(2)

    @functools.partial(
        pl.kernel, mesh=mesh,
        out_type=jax.ShapeDtypeStruct((TOP_K, T, ROW_CHUNKS, LANES), F32),
        scratch_types=[pltpu.VMEM((2, SC_CHUNK), jnp.int32),
                       pltpu.VMEM((2, SC_CHUNK, ROW_CHUNKS, LANES), F32),
                       pltpu.SemaphoreType.DMA((2,)),
                       pltpu.SemaphoreType.DMA((2,))],
        name="sc_gather")
    def gather(y_hbm, d0_hbm, d1_hbm, out_hbm, idx, rows, gsem, wsem):
        base = _sc_worker_base(per_worker)
        for k, d_hbm in enumerate((d0_hbm, d1_hbm)):
            @pl.loop(0, per_worker, step=2 * SC_CHUNK)
            def _(c):
                gathers = []
                for b in bufs:
                    off = base + c + b * SC_CHUNK
                    pltpu.sync_copy(d_hbm.at[pl.ds(off, SC_CHUNK)], idx.at[b])
                    gathers.append(pltpu.async_copy(y_hbm.at[idx.at[b]], rows.at[b], gsem.at[b]))
                writes = []
                for b in bufs:
                    off = base + c + b * SC_CHUNK
                    gathers[b].wait()
                    writes.append(pltpu.async_copy(rows.at[b], out_hbm.at[k, pl.ds(off, SC_CHUNK)], wsem.at[b]))
                for w in writes:
                    w.wait()

    return gather(y3, *dests)


def _combine_body(yk_ref, x1_ref, ri_ref, g_ref, *rest, tc, final_norm):
    out_ref = rest[-1]

    def rows_of(k):
        return jnp.concatenate(
            [yk_ref[k, pl.ds(s, tc, stride=ROW_CHUNKS), :] for s in range(ROW_CHUNKS)], axis=-1)

    ri = ri_ref[...]
    w0 = ri[:, 2:3]
    w1 = ri[:, 3:4]
    x2 = x1_ref[...] + (w0 * rows_of(0) + w1 * rows_of(1))
    if final_norm:
        ms = jnp.mean(x2 * x2, axis=-1, keepdims=True)
        x2 = x2 * lax.rsqrt(ms + EPS) * g_ref[...]
    out_ref[...] = x2


def _combine(yk, x1, ri, g, out_prev, *, tok0, final_norm, tc=256):
    T = x1.shape[0]
    steps = yk.shape[1] // (tc * ROW_CHUNKS)
    blk0 = tok0 // tc
    in_specs = [
        pl.BlockSpec((TOP_K, tc * ROW_CHUNKS, LANES), lambda i: (0, i, 0)),
        pl.BlockSpec((tc, D_MODEL), lambda i: (i + blk0, 0)),
        pl.BlockSpec((tc, LANES), lambda i: (i + blk0, 0)),
        pl.BlockSpec((1, D_MODEL), lambda i: (0, 0)),
    ]
    args = [yk, x1, ri, g]
    aliases = {}
    if out_prev is not None:
        in_specs.append(pl.BlockSpec(memory_space=pl.ANY))
        args.append(out_prev)
        aliases = {len(args) - 1: 0}
    return pl.pallas_call(
        functools.partial(_combine_body, tc=tc, final_norm=final_norm),
        grid=(steps,),
        in_specs=in_specs,
        out_specs=pl.BlockSpec((tc, D_MODEL), lambda i: (i + blk0, 0)),
        out_shape=jax.ShapeDtypeStruct((T, D_MODEL), F32),
        input_output_aliases=aliases,
        compiler_params=_cparams(("arbitrary",)),
        name="combine",
    )(*args)


def _pad_lanes(a, width=LANES):
    return jnp.pad(a, ((0, 0), (0, width - a.shape[-1])))


def _layer(x2, batch, seq, g_mix, w_in, conv_w, conv_b, w_rg_a, b_rg_a, w_rg_x, b_rg_x, lam, b_forget,
           w_lru_out, w_attn_out, w_out, g_ffn, w_rgrp, b_rgrp, w_rexp, b_rexp, w1, w3, w2, g_out, final_norm):
    T = batch * seq
    row = lambda a: a.reshape(1, -1).astype(F32)

    fl0 = 5 * D_MODEL
    w_main = jnp.concatenate([w_in[:, :fl0], w_in[:, fl0 + N_HEADS:]], axis=1).astype(BF16)
    w_fl = _pad_lanes(w_in[:, fl0:fl0 + N_HEADS]).astype(BF16)
    z, fl = _inproj(x2, row(g_mix), w_main, w_fl)

    caug = _forget(fl, _pad_lanes(row(b_forget)), batch=batch, seq=seq)
    o = _attention(z, caug, batch=batch, seq=seq)

    hg = _lru(z, conv_w.astype(F32), row(conv_b), w_rg_a.astype(BF16), w_rg_x.astype(BF16),
              row(b_rg_a), row(b_rg_x), row(lam), batch=batch, seq=seq)

    wr = _pad_lanes(jnp.concatenate([w_rgrp, w_rexp], axis=1)).astype(BF16)
    br = _pad_lanes(jnp.concatenate([row(b_rgrp), row(b_rexp)], axis=1))
    x1, v, ri, rt, cnt = _merge(hg, o, z, x2, w_lru_out.astype(BF16), w_attn_out.astype(BF16),
                            w_out.astype(BF16), row(g_ffn), wr, br)

    v3 = v.reshape(T, ROW_CHUNKS, LANES)
    g_out = row(g_out)
    Tp = T // MOE_PARTS
    n_blocks = (Tp * TOP_K + N_EXPERTS * (EXPERT_ROWS - 1) + EXPERT_ROWS - 1) // EXPERT_ROWS
    blk_start = jnp.arange(n_blocks, dtype=jnp.int32) * EXPERT_ROWS
    out = None
    for part in range(MOE_PARTS):
        tok0 = part * Tp
        e = rt[0:TOP_K, tok0:tok0 + Tp].astype(jnp.int32)
        rank = rt[4:4 + TOP_K, tok0:tok0 + Tp].astype(jnp.int32)
        counts = cnt[part, 0, R_EXP0:R_EXP0 + N_EXPERTS].astype(jnp.int32)
        padded = (counts + EXPERT_ROWS - 1) // EXPERT_ROWS * EXPERT_ROWS
        pad_end = jnp.cumsum(padded)
        pad_start = pad_end - padded
        seg_start = jnp.zeros_like(e)
        for j in range(N_EXPERTS):
            seg_start = jnp.where(e == j, pad_start[j], seg_start)
        dest2 = (seg_start + rank).astype(jnp.int32)
        dests = [dest2[k] for k in range(TOP_K)]
        blk_expert = jnp.minimum(
            jnp.sum((pad_end[None, :] <= blk_start[:, None]).astype(jnp.int32), axis=1), N_EXPERTS - 1)
        n_used = (pad_end[-1:] // EXPERT_ROWS).astype(jnp.int32)
        fill = jnp.concatenate([pad_start + counts, padded - counts, n_used]).astype(jnp.int32)

        xs = _sc_scatter_rows(v3, dests, n_blocks * EXPERT_ROWS, tok0)
        xs = _fill_padding(fill, xs.reshape(-1, LANES))
        y = _experts(blk_expert, n_used, xs, w1, w3, w2)
        yk = _sc_gather_rows(y.reshape(-1, ROW_CHUNKS, LANES), dests)
        out = _combine(yk.reshape(TOP_K, Tp * ROW_CHUNKS, LANES), x1, ri, g_out, out,
                       tok0=tok0, final_norm=final_norm)
    return out


def kernel(x, g_mix, w_in, conv_w, conv_b, w_rg_a, b_rg_a, w_rg_x, b_rg_x, lru_lambda, b_forget, w_lru_out, w_attn_out, w_out, g_ffn, w_route_group, b_route_group, w_route_expert, b_route_expert, w_exp_gate, w_exp_up, w_exp_down, g_final):
    batch, seq, _ = x.shape
    depth = g_mix.shape[0]
    x2 = x.reshape(batch * seq, D_MODEL)
    for l in range(depth):
        x2 = _layer(
            x2, batch, seq, g_mix[l], w_in[l], conv_w[l], conv_b[l], w_rg_a[l], b_rg_a[l], w_rg_x[l],
            b_rg_x[l], lru_lambda[l], b_forget[l], w_lru_out[l], w_attn_out[l], w_out[l], g_ffn[l],
            w_route_group[l], b_route_group[l], w_route_expert[l], b_route_expert[l],
            w_exp_gate[l], w_exp_up[l], w_exp_down[l], g_final, l == depth - 1)
    return x2.reshape(batch, seq, D_MODEL)
```

```python
import functools

import jax
import jax.numpy as jnp
from jax import lax
from jax.experimental import pallas as pl
from jax.experimental.pallas import tpu as pltpu
from jax.experimental.pallas import tpu_sc as plsc

F32 = jnp.float32
BF16 = jnp.bfloat16

D_MODEL = 1024
LRU_BLOCK_W = 256
LRU_BLOCKS = D_MODEL // LRU_BLOCK_W
CONV_W = 4
LRU_C = 8.0
N_HEADS = 8
HEAD_DIM = D_MODEL // N_HEADS
N_GROUPS = 4
EXPERTS_PER_GROUP = 8
N_EXPERTS = N_GROUPS * EXPERTS_PER_GROUP
TOP_K = 2
D_EXPERT = D_MODEL // 2
EPS = 1e-6

LANES = 128
SUBLANES = 8
ROW_CHUNKS = D_MODEL // LANES
PACK_CHUNKS = ROW_CHUNKS // 2
U32 = jnp.uint32
VMEM_LIMIT = 48 * 1024 * 1024

ZC_XR, ZC_GR, ZC_Q, ZC_K, ZC_V, ZC_ML, ZC_MA = 0, 8, 16, 24, 32, 40, 48
Z_WIDTH = 7 * D_MODEL

R_EXP0 = N_GROUPS

EXPERT_ROWS = 256


def _cparams(sem):
    return pltpu.CompilerParams(dimension_semantics=sem, vmem_limit_bytes=VMEM_LIMIT)


LOG2E = 1.4426950408889634
Q_PRESCALE = HEAD_DIM ** -0.5 * LOG2E


def _inproj_body(x_ref, g_ref, w_ref, wfl_ref, z_ref, fl_ref, u_ref, *, q_block):
    j = pl.program_id(1)

    @pl.when(j == 0)
    def _():
        x = x_ref[...]
        ms = jnp.mean(x * x, axis=-1, keepdims=True)
        u = (x * lax.rsqrt(ms + EPS) * g_ref[...]).astype(BF16)
        u_ref[...] = u
        fl_ref[...] = jnp.dot(u, wfl_ref[...], preferred_element_type=F32)

    @pl.when(j == q_block)
    def _():
        acc = jnp.dot(u_ref[...], w_ref[...], preferred_element_type=F32)
        z_ref[...] = (acc * Q_PRESCALE).astype(BF16)

    @pl.when(j != q_block)
    def _():
        z_ref[...] = jnp.dot(u_ref[...], w_ref[...], preferred_element_type=F32).astype(BF16)


def _inproj(x2, g, w_main, w_fl, *, tm=2048, tn=D_MODEL):
    T = x2.shape[0]
    return pl.pallas_call(
        functools.partial(_inproj_body, q_block=ZC_Q * LANES // tn),
        grid=(T // tm, Z_WIDTH // tn),
        in_specs=[
            pl.BlockSpec((tm, D_MODEL), lambda i, j: (i, 0)),
            pl.BlockSpec((1, D_MODEL), lambda i, j: (0, 0)),
            pl.BlockSpec((D_MODEL, tn), lambda i, j: (0, j)),
            pl.BlockSpec((D_MODEL, LANES), lambda i, j: (0, 0)),
        ],
        out_specs=[
            pl.BlockSpec((tm, tn), lambda i, j: (i, j)),
            pl.BlockSpec((tm, LANES), lambda i, j: (i, 0)),
        ],
        out_shape=[
            jax.ShapeDtypeStruct((T, Z_WIDTH), BF16),
            jax.ShapeDtypeStruct((T, LANES), F32),
        ],
        scratch_shapes=[pltpu.VMEM((tm, D_MODEL), BF16)],
        compiler_params=_cparams(("arbitrary", "arbitrary")),
        name="inproj",
    )(x2, g, w_main, w_fl)


C_TERMS = 3


def _log_sigmoid(z):
    return jnp.minimum(z, 0.0) - jnp.log1p(jnp.exp(-jnp.abs(z)))


def _forget_body(fl_ref, b_ref, c_ref, *, seq):
    lf = _log_sigmoid(fl_ref[...] + b_ref[...])
    row = lax.broadcasted_iota(jnp.int32, lf.shape, 0)
    lane = lax.broadcasted_iota(jnp.int32, lf.shape, 1)
    c = lf
    k = 1
    while k < seq:
        c = c + jnp.where(row >= k, pltpu.roll(c, k, axis=0), 0.0)
        k *= 2
    rem = c * LOG2E
    out = jnp.zeros(lf.shape, F32)
    for n in range(C_TERMS):
        t = rem.astype(BF16).astype(F32)
        rem = rem - t
        shifted = t if n == 0 else pltpu.roll(t, n * N_HEADS, axis=1)
        out = jnp.where((lane >= n * N_HEADS) & (lane < (n + 1) * N_HEADS), shifted, out)
    c_ref[...] = out.astype(BF16)


def _forget(fl, b_pad, *, batch, seq):
    return pl.pallas_call(
        functools.partial(_forget_body, seq=seq),
        grid=(batch,),
        in_specs=[
            pl.BlockSpec((seq, LANES), lambda b: (b, 0)),
            pl.BlockSpec((1, LANES), lambda b: (0, 0)),
        ],
        out_specs=pl.BlockSpec((seq, LANES), lambda b: (b, 0)),
        out_shape=jax.ShapeDtypeStruct((batch * seq, LANES), BF16),
        compiler_params=_cparams(("arbitrary",)),
        name="forget",
    )(fl, b_pad)


ATTN_HEADS_PER_STEP = 4


def _attn_body(q_ref, k_ref, v_ref, kc_ref, o_ref, *, tq):
    i = pl.program_id(2)
    lane = lax.broadcasted_iota(jnp.int32, (tq, LANES), 1)
    heads = range(ATTN_HEADS_PER_STEP)
    cols =[slice(g * HEAD_DIM, (g + 1) * HEAD_DIM) for g in heads]
    qs = []
    for g in heads:
        h = pl.program_id(1) * ATTN_HEADS_PER_STEP + g
        mine = (lane < C_TERMS * N_HEADS) & ((lane & (N_HEADS - 1)) == h)
        qc = jnp.where(mine, -1.0, 0.0).astype(BF16)
        qs.append(jnp.concatenate([q_ref[:, cols[g]], qc], axis=1))

    def step(j, carry, masked):
        off = pl.multiple_of(j * tq, tq)
        out = []
        for g in heads:
            m, l, acc = carry[g]
            kj = jnp.concatenate([k_ref[pl.ds(off, tq), cols[g]], kc_ref[pl.ds(off, tq), :]], axis=1)
            s = lax.dot_general(qs[g], kj, (((1,), (1,)), ((), ())), preferred_element_type=F32)
            if masked:
                r = lax.broadcasted_iota(jnp.int32, s.shape, 0)
                cidx = lax.broadcasted_iota(jnp.int32, s.shape, 1)
                s = jnp.where(cidx <= r, s, -jnp.inf)
            m_new = jnp.maximum(m, jnp.max(s, axis=-1, keepdims=True))
            p = jnp.exp2(s - m_new)
            alpha = jnp.exp2(m - m_new)
            l = alpha * l + jnp.sum(p, axis=-1, keepdims=True)
            acc = alpha * acc + jnp.dot(p.astype(BF16), v_ref[pl.ds(off, tq), cols[g]],
                                        preferred_element_type=F32)
            out.append((m_new, l, acc))
        return tuple(out)

    init = tuple((jnp.full((tq, 1), -jnp.inf, F32), jnp.zeros((tq, 1), F32), jnp.zeros((tq, HEAD_DIM), F32))
                 for _ in heads)
    carry = lax.fori_loop(0, i, lambda j, c: step(j, c, False), init)
    carry = step(i, carry, True)
    for g in heads:
        _, l, acc = carry[g]
        o_ref[:, cols[g]] = (acc / l).astype(BF16)


def _attention(z, caug, *, batch, seq, tq=512):
    nq = seq // tq
    T = batch * seq
    G = ATTN_HEADS_PER_STEP
    W = G * HEAD_DIM
    return pl.pallas_call(
        functools.partial(_attn_body, tq=tq),
        grid=(batch, N_HEADS // G, nq),
        in_specs=[
            pl.BlockSpec((tq, W), lambda b, h, i: (b * nq + i, ZC_Q // G + h)),
            pl.BlockSpec((seq, W), lambda b, h, i: (b, ZC_K // G + h)),
            pl.BlockSpec((seq, W), lambda b, h, i: (b, ZC_V // G + h)),
            pl.BlockSpec((seq, LANES), lambda b, h, i: (b, 0)),
        ],
        out_specs=pl.BlockSpec((tq, W), lambda b, h, i: (b * nq + i, h)),
        out_shape=jax.ShapeDtypeStruct((T, D_MODEL), BF16),
        compiler_params=_cparams(("arbitrary", "arbitrary", "arbitrary")),
        name="attn",
    )(z, z, z, caug)


def _lru_body(xr_ref, gr_ref, cw_ref, cb_ref, wa_ref, wx_ref, ba_ref, bx_ref, lam_ref,
              hg_ref, xbuf, hcar, *, ts):
    i = pl.program_id(1)

    @pl.when(i == 0)
    def _():
        xbuf[0:SUBLANES, :] = jnp.zeros((SUBLANES, D_MODEL), F32)
        hcar[...] = jnp.zeros_like(hcar)

    @pl.when(i > 0)
    def _():
        xbuf[0:SUBLANES, :] = xbuf[ts:ts + SUBLANES, :]

    xbuf[SUBLANES:ts + SUBLANES, :] = xr_ref[...].astype(F32)

    base = SUBLANES - (CONV_W - 1)
    xc = cw_ref[0:1, :] * xbuf[base:base + ts, :]
    for k in range(1, CONV_W):
        xc = xc + cw_ref[k:k + 1, :] * xbuf[base + k:base + k + ts, :]
    xc = xc + cb_ref[...]

    xcb = xc.astype(BF16)
    ra = jnp.concatenate(
        [jnp.dot(xcb[:, n * LRU_BLOCK_W:(n + 1) * LRU_BLOCK_W], wa_ref[n], preferred_element_type=F32)
         for n in range(LRU_BLOCKS)], axis=-1)
    rx = jnp.concatenate(
        [jnp.dot(xcb[:, n * LRU_BLOCK_W:(n + 1) * LRU_BLOCK_W], wx_ref[n], preferred_element_type=F32)
         for n in range(LRU_BLOCKS)], axis=-1)
    r = jax.nn.sigmoid(ra + ba_ref[...])
    ig = jax.nn.sigmoid(rx + bx_ref[...])
    nlam = -lam_ref[...]
    softplus = jnp.maximum(nlam, 0.0) + jnp.log1p(jnp.exp(-jnp.abs(nlam)))
    log_a = (-LRU_C * r) * softplus
    a = jnp.exp(log_a)
    th = jnp.tanh(log_a)
    mult = jnp.sqrt(-2.0 * th / (1.0 - th))
    b = mult * ig * xc

    row = lax.broadcasted_iota(jnp.int32, (SUBLANES, D_MODEL), 0)
    keeps = [(k, row >= k) for k in (1, 2, 4)]
    hprev = jnp.broadcast_to(hcar[...], (SUBLANES, D_MODEL))
    pieces = []
    for j in range(ts // SUBLANES):
        aj = a[j * SUBLANES:(j + 1) * SUBLANES, :]
        bj = b[j * SUBLANES:(j + 1) * SUBLANES, :]
        for k, keep in keeps:
            a_sh = jnp.where(keep, pltpu.roll(aj, k, axis=0), 1.0)
            b_sh = jnp.where(keep, pltpu.roll(bj, k, axis=0), 0.0)
            bj = aj * b_sh + bj
            aj = aj * a_sh
        hj = bj + aj * hprev
        hprev = jnp.broadcast_to(hj[SUBLANES - 1:SUBLANES, :], (SUBLANES, D_MODEL))
        pieces.append(hj)
    h = jnp.concatenate(pieces, axis=0)
    hcar[...] = h[ts - 1:ts, :]

    hg_ref[...] = (h * jax.nn.gelu(gr_ref[...].astype(F32))).astype(BF16)


def _lru(z, conv_w, conv_b, wa, wx, ba, bx, lam, *, batch, seq, ts=256):
    ns = seq // ts
    T = batch * seq
    full = lambda shape: pl.BlockSpec(shape, lambda b, i: (0,) * len(shape))
    return pl.pallas_call(
        functools.partial(_lru_body, ts=ts),
        grid=(batch, ns),
        in_specs=[
            pl.BlockSpec((ts, D_MODEL), lambda b, i: (b * ns + i, ZC_XR // SUBLANES)),
            pl.BlockSpec((ts, D_MODEL), lambda b, i: (b * ns + i, ZC_GR // SUBLANES)),
            full((CONV_W, D_MODEL)),
            full((1, D_MODEL)),
            full((LRU_BLOCKS, LRU_BLOCK_W, LRU_BLOCK_W)),
            full((LRU_BLOCKS, LRU_BLOCK_W, LRU_BLOCK_W)),
            full((1, D_MODEL)),
            full((1, D_MODEL)),
            full((1, D_MODEL)),
        ],
        out_specs=pl.BlockSpec((ts, D_MODEL), lambda b, i: (b * ns + i, 0)),
        out_shape=jax.ShapeDtypeStruct((T, D_MODEL), BF16),
        scratch_shapes=[pltpu.VMEM((ts + SUBLANES, D_MODEL), F32), pltpu.VMEM((1, D_MODEL), F32)],
        compiler_params=_cparams(("arbitrary", "arbitrary")),
        name="lru",
    )(z, z, conv_w, conv_b, wa, wx, ba, bx, lam)


MERGE_SUBTILES = 1
MOE_PARTS = 2


def _merge_body(hg_ref, o_ref, ml_ref, ma_ref, x_ref, wl_ref, wat_ref, wo_ref, g_ref, wr_ref, br_ref,
                x1_ref, v_ref, ri_ref, rt_ref, cnt_ref, carry_ref, *, tm, steps_per_part):
    @pl.when(pl.program_id(0) % steps_per_part == 0)
    def _():
        carry_ref[...] = jnp.zeros_like(carry_ref)

    ts = tm // MERGE_SUBTILES
    lane = lax.broadcasted_iota(jnp.int32, (ts, LANES), 1)
    rr = lax.broadcasted_iota(jnp.int32, (ts, ts), 0)
    cc = lax.broadcasted_iota(jnp.int32, (ts, ts), 1)
    tri = jnp.where(cc < rr, 1.0, 0.0).astype(BF16)
    ninf = -jnp.inf
    big = jnp.int32(1 << 20)
    carry = carry_ref[...]

    for sub in range(MERGE_SUBTILES):
        rows = slice(sub * ts, (sub + 1) * ts)
        yl = jnp.dot(hg_ref[rows, :], wl_ref[...], preferred_element_type=F32)
        ya = jnp.dot(o_ref[rows, :], wat_ref[...], preferred_element_type=F32)
        merged = (jax.nn.sigmoid(ml_ref[rows, :].astype(F32)) * yl
                  + jax.nn.sigmoid(ma_ref[rows, :].astype(F32)) * ya)
        x1 = x_ref[rows, :] + jnp.dot(merged.astype(BF16), wo_ref[...], preferred_element_type=F32)
        x1_ref[rows, :] = x1
        ms = jnp.mean(x1 * x1, axis=-1, keepdims=True)
        v = x1 * lax.rsqrt(ms + EPS) * g_ref[...]
        bits = lax.bitcast_convert_type(v.astype(BF16).astype(F32), U32)
        half = PACK_CHUNKS * LANES
        for s in range(PACK_CHUNKS):
            word = (bits[:, s * LANES:(s + 1) * LANES] >> 16) | bits[:, half + s * LANES:half + (s + 1) * LANES]
            v_ref[pl.ds(sub * ts * PACK_CHUNKS + s, ts, stride=PACK_CHUNKS), :] = word

        logits = jnp.dot(v.astype(BF16), wr_ref[...], preferred_element_type=F32) + br_ref[...]

        gl = jnp.where(lane < N_GROUPS, logits, ninf)
        gmax = jnp.max(gl, axis=-1, keepdims=True)
        gsel = jnp.min(jnp.where(gl == gmax, lane, big), axis=-1, keepdims=True)
        pg = 1.0 / jnp.sum(jnp.exp(gl - gmax), axis=-1, keepdims=True)

        lo = R_EXP0 + gsel * EXPERTS_PER_GROUP
        el = jnp.where(lane >= lo, jnp.where(lane < lo + EXPERTS_PER_GROUP, logits, ninf), ninf)
        v1 = jnp.max(el, axis=-1, keepdims=True)
        i1 = jnp.min(jnp.where(el == v1, lane, big), axis=-1, keepdims=True)
        el2 = jnp.where(lane == i1, ninf, el)
        v2 = jnp.max(el2, axis=-1, keepdims=True)
        i2 = jnp.min(jnp.where(el2 == v2, lane, big), axis=-1, keepdims=True)
        e21 = jnp.exp(v2 - v1)
        p1 = 1.0 / (1.0 + e21)
        w0 = pg * p1
        w1 = pg * (e21 * p1)

        hit0 = lane == i1
        hit1 = lane == i2
        onehot = jnp.where(hit0, 1.0, jnp.where(hit1, 1.0, 0.0))
        before = jnp.dot(tri, onehot.astype(BF16), preferred_element_type=F32) + carry
        rank0 = jnp.sum(jnp.where(hit0, before, 0.0), axis=-1, keepdims=True)
        rank1 = jnp.sum(jnp.where(hit1, before, 0.0), axis=-1, keepdims=True)
        carry = carry + jnp.sum(onehot, axis=0, keepdims=True)

        e0 = (i1 - R_EXP0).astype(F32)
        e1 = (i2 - R_EXP0).astype(F32)
        ri = jnp.where(lane == 0, e0,
             jnp.where(lane == 1, e1,
             jnp.where(lane == 2, w0,
             jnp.where(lane == 3, w1,
             jnp.where(lane == 4, rank0,
             jnp.where(lane == 5, rank1, 0.0))))))
        ri_ref[rows, :] = ri
        rt_ref[:, rows] = ri.T[:SUBLANES, :]

    carry_ref[...] = carry
    cnt_ref[0] = carry


def _merge(hg, o, z, x2, wl, wat, wo, g, wr, br, *, tm=512):
    T = x2.shape[0]
    steps_per_part = T // tm // MOE_PARTS
    full = lambda shape: pl.BlockSpec(shape, lambda i: (0,) * len(shape))
    return pl.pallas_call(
        functools.partial(_merge_body, tm=tm, steps_per_part=steps_per_part),
        grid=(T // tm,),
        in_specs=[
            pl.BlockSpec((tm, D_MODEL), lambda i: (i, 0)),
            pl.BlockSpec((tm, D_MODEL), lambda i: (i, 0)),
            pl.BlockSpec((tm, D_MODEL), lambda i: (i, ZC_ML // SUBLANES)),
            pl.BlockSpec((tm, D_MODEL), lambda i: (i, ZC_MA // SUBLANES)),
            pl.BlockSpec((tm, D_MODEL), lambda i: (i, 0)),
            full((D_MODEL, D_MODEL)),
            full((D_MODEL, D_MODEL)),
            full((D_MODEL, D_MODEL)),
            full((1, D_MODEL)),
            full((D_MODEL, LANES)),
            full((1, LANES)),
        ],
        out_specs=[
            pl.BlockSpec((tm, D_MODEL), lambda i: (i, 0)),
            pl.BlockSpec((tm * PACK_CHUNKS, LANES), lambda i: (i, 0)),
            pl.BlockSpec((tm, LANES), lambda i: (i, 0)),
            pl.BlockSpec((SUBLANES, tm), lambda i: (0, i)),
            pl.BlockSpec((1, 1, LANES), lambda i: (i // steps_per_part, 0, 0)),
        ],
        out_shape=[
            jax.ShapeDtypeStruct((T, D_MODEL), F32),
            jax.ShapeDtypeStruct((T * PACK_CHUNKS, LANES), U32),
            jax.ShapeDtypeStruct((T, LANES), F32),
            jax.ShapeDtypeStruct((SUBLANES, T), F32),
            jax.ShapeDtypeStruct((MOE_PARTS, 1, LANES), F32),
        ],
        scratch_shapes=[pltpu.VMEM((1, LANES), F32)],
        compiler_params=_cparams(("arbitrary",)),
        name="merge",
    )(hg, o, z, z, x2, wl, wat, wo, g, wr, br)


def _fill_padding_body(fill_ref, xs_in_ref, xs_ref, zblock, zsem, *, n_blocks):
    del xs_in_ref

    def for_each_padding_piece(fn):
        def per_expert(e, carry):
            start = fill_ref[e]
            n = fill_ref[N_EXPERTS + e]
            size = EXPERT_ROWS // 2
            while size >= 1:
                take = (n & size) != 0

                @pl.when(take)
                def _(start=start, size=size):
                    fn(pltpu.make_async_copy(zblock.at[pl.ds(0, size)], xs_ref.at[pl.ds(start, size)], zsem))

                start = start + jnp.where(take, size, 0)
                size //= 2
            return carry
        lax.fori_loop(0, N_EXPERTS, per_expert, 0)

    def zero_block_copy(blk):
        return pltpu.make_async_copy(zblock, xs_ref.at[pl.ds(blk * EXPERT_ROWS, EXPERT_ROWS)], zsem)

    def for_each_unused_block(fn):
        lax.fori_loop(fill_ref[2 * N_EXPERTS], n_blocks, lambda blk, c: (fn(zero_block_copy(blk)), c)[1], 0)

    zblock[...] = jnp.zeros_like(zblock)
    for_each_padding_piece(lambda cp: cp.start())
    for_each_unused_block(lambda cp: cp.start())
    for_each_padding_piece(lambda cp: cp.wait())
    for_each_unused_block(lambda cp: cp.wait())


def _fill_padding(fill, xs3):
    n_blocks = xs3.shape[0] // EXPERT_ROWS
    grid_spec = pltpu.PrefetchScalarGridSpec(
        num_scalar_prefetch=1,
        grid=(1,),
        in_specs=[pl.BlockSpec(memory_space=pl.ANY)],
        out_specs=pl.BlockSpec(memory_space=pl.ANY),
        scratch_shapes=[pltpu.VMEM((EXPERT_ROWS,) + xs3.shape[1:], xs3.dtype),
                        pltpu.SemaphoreType.DMA(())],
    )
    return pl.pallas_call(
        functools.partial(_fill_padding_body, n_blocks=n_blocks),
        grid_spec=grid_spec,
        out_shape=jax.ShapeDtypeStruct(xs3.shape, xs3.dtype),
        input_output_aliases={1: 0},
        compiler_params=_cparams(("arbitrary",)),
        name="fill_padding",
    )(fill, xs3)


def _expert_body(be_ref, nu_ref, x_ref, w1_ref, w3_ref, w2_ref, y_ref, w13b, w2b, *, rows):
    i = pl.program_id(0)

    @pl.when((i == 0) | (be_ref[i] != be_ref[jnp.maximum(i - 1, 0)]))
    def _():
        w13b[:, :D_EXPERT] = w1_ref[0].astype(BF16)
        w13b[:, D_EXPERT:] = w3_ref[0].astype(BF16)
        w2b[...] = w2_ref[0].astype(BF16)

    @pl.when(i < nu_ref[0])
    def _():
        words = [x_ref[pl.ds(s, rows, stride=PACK_CHUNKS), :] for s in range(PACK_CHUNKS)]
        low = [lax.bitcast_convert_type(w << 16, F32).astype(BF16) for w in words]
        high = [lax.bitcast_convert_type(w & jnp.uint32(0xFFFF0000), F32).astype(BF16) for w in words]
        xb = jnp.concatenate(low + high, axis=-1)
        gu = jnp.dot(xb, w13b[...], preferred_element_type=F32)
        hb = (jax.nn.silu(gu[:, :D_EXPERT]) * gu[:, D_EXPERT:]).astype(BF16)
        y = jnp.dot(hb, w2b[...], preferred_element_type=F32)
        for s in range(ROW_CHUNKS):
            y_ref[pl.ds(s, rows, stride=ROW_CHUNKS), :] = y[:, s * LANES:(s + 1) * LANES]

    @pl.when(i >= nu_ref[0])
    def _():
        y_ref[...] = jnp.zeros_like(y_ref)


def _experts(blk_expert, n_used, xs, w1, w3, w2, *, rows=EXPERT_ROWS):
    n_blocks = xs.shape[0] // (rows * PACK_CHUNKS)
    grid_spec = pltpu.PrefetchScalarGridSpec(
        num_scalar_prefetch=2,
        grid=(n_blocks,),
        in_specs=[
            pl.BlockSpec((rows * PACK_CHUNKS, LANES), lambda i, be, nu: (jnp.minimum(i, nu[0] - 1), 0)),
            pl.BlockSpec((1, D_MODEL, D_EXPERT), lambda i, be, nu: (be[i], 0, 0)),
            pl.BlockSpec((1, D_MODEL, D_EXPERT), lambda i, be, nu: (be[i], 0, 0)),
            pl.BlockSpec((1, D_EXPERT, D_MODEL), lambda i, be, nu: (be[i], 0, 0)),
        ],
        out_specs=pl.BlockSpec((rows * ROW_CHUNKS, LANES), lambda i, be, nu: (i, 0)),
        scratch_shapes=[pltpu.VMEM((D_MODEL, 2 * D_EXPERT), BF16), pltpu.VMEM((D_EXPERT, D_MODEL), BF16)],
    )
    return pl.pallas_call(
        functools.partial(_expert_body, rows=rows),
        grid_spec=grid_spec,
        out_shape=jax.ShapeDtypeStruct((n_blocks * rows * ROW_CHUNKS, LANES), F32),
        compiler_params=_cparams(("arbitrary",)),
        name="experts",
    )(blk_expert, n_used, xs, w1, w3, w2)


SC_CORES = 2
SC_SUBCORES = 16
SC_WORKERS = SC_CORES * SC_SUBCORES
SC_CHUNK = 32


def _sc_worker_base(per_worker):
    return (lax.axis_index("s") * SC_CORES + lax.axis_index("c")) * per_worker


def _sc_scatter_rows(v3, dests, n_rows, tok0):
    per_worker = dests[0].shape[0] // SC_WORKERS
    mesh = plsc.VectorSubcoreMesh(core_axis_name="c", subcore_axis_name="s")
    bufs = range(2)
    slab = v3.shape[1:]

    @functools.partial(
        pl.kernel, mesh=mesh,
        out_type=jax.ShapeDtypeStruct((n_rows,) + slab, v3.dtype),
        scratch_types=[pltpu.VMEM((2 * TOP_K, SC_CHUNK), jnp.int32),
                       pltpu.VMEM((2, SC_CHUNK) + slab, v3.dtype),
                       pltpu.SemaphoreType.DMA((2,)),
                       pltpu.SemaphoreType.DMA((2,))],
        name="sc_scatter")
    def scatter(v_hbm, d0_hbm, d1_hbm, xs_hbm, idx, rows, lsem, ssem):
        base = _sc_worker_base(per_worker)

        @pl.loop(0, per_worker, step=2 * SC_CHUNK)
        def _(c):
            loads = []
            for b in bufs:
                off = base + c + b * SC_CHUNK
                loads.append(pltpu.async_copy(v_hbm.at[pl.ds(tok0 + off, SC_CHUNK)], rows.at[b], lsem.at[b]))
                for k, d_hbm in enumerate((d0_hbm, d1_hbm)):
                    pltpu.sync_copy(d_hbm.at[pl.ds(off, SC_CHUNK)], idx.at[b * TOP_K + k])
            stores = []
            for b in bufs:
                loads[b].wait()
                for k in range(TOP_K):
                    stores.append(pltpu.async_copy(rows.at[b], xs_hbm.at[idx.at[b * TOP_K + k]], ssem.at[b]))
            for st in stores:
                st.wait()

    return scatter(v3, *dests)


def _sc_gather_rows(y3, dests):
    T = dests[0].shape[0]
    per_worker = T // SC_WORKERS
    mesh = plsc.VectorSubcoreMesh(core_axis_name="c", subcore_axis_name="s")
    bufs = range(2)

    @functools.partial(
        pl.kernel, mesh=mesh,
        out_type=jax.ShapeDtypeStruct((TOP_K, T, ROW_CHUNKS, LANES), F32),
        scratch_types=[pltpu.VMEM((2, SC_CHUNK), jnp.int32),
                       pltpu.VMEM((2, SC_CHUNK, ROW_CHUNKS, LANES), F32),
                       pltpu.SemaphoreType.DMA((2,)),
                       pltpu.SemaphoreType.DMA((2,))],
        name="sc_gather")
    def gather(y_hbm, d0_hbm, d1_hbm, out_hbm, idx, rows, gsem, wsem):
        base = _sc_worker_base(per_worker)
        for k, d_hbm in enumerate((d0_hbm, d1_hbm)):
            @pl.loop(0, per_worker, step=2 * SC_CHUNK)
            def _(c):
                gathers = []
                for b in bufs:
                    off = base + c + b * SC_CHUNK
                    pltpu.sync_copy(d_hbm.at[pl.ds(off, SC_CHUNK)], idx.at[b])
                    gathers.append(pltpu.async_copy(y_hbm.at[idx.at[b]], rows.at[b], gsem.at[b]))
                writes = []
                for b in bufs:
                    off = base + c + b * SC_CHUNK
                    gathers[b].wait()
                    writes.append(pltpu.async_copy(rows.at[b], out_hbm.at[k, pl.ds(off, SC_CHUNK)], wsem.at[b]))
                for w in writes:
                    w.wait()

    return gather(y3, *dests)


def _combine_body(yk_ref, x1_ref, ri_ref, g_ref, *rest, tc, final_norm):
    out_ref = rest[-1]

    def rows_of(k):
        return jnp.concatenate(
            [yk_ref[k, pl.ds(s, tc, stride=ROW_CHUNKS), :] for s in range(ROW_CHUNKS)], axis=-1)

    ri = ri_ref[...]
    w0 = ri[:, 2:3]
    w1 = ri[:, 3:4]
    x2 = x1_ref[...] + (w0 * rows_of(0) + w1 * rows_of(1))
    if final_norm:
        ms = jnp.mean(x2 * x2, axis=-1, keepdims=True)
        x2 = x2 * lax.rsqrt(ms + EPS) * g_ref[...]
    out_ref[...] = x2


def _combine(yk, x1, ri, g, out_prev, *, tok0, final_norm, tc=256):
    T = x1.shape[0]
    steps = yk.shape[1] // (tc * ROW_CHUNKS)
    blk0 = tok0 // tc
    in_specs = [
        pl.BlockSpec((TOP_K, tc * ROW_CHUNKS, LANES), lambda i: (0, i, 0)),
        pl.BlockSpec((tc, D_MODEL), lambda i: (i + blk0, 0)),
        pl.BlockSpec((tc, LANES), lambda i: (i + blk0, 0)),
        pl.BlockSpec((1, D_MODEL), lambda i: (0, 0)),
    ]
    args = [yk, x1, ri, g]
    aliases = {}
    if out_prev is not None:
        in_specs.append(pl.BlockSpec(memory_space=pl.ANY))
        args.append(out_prev)
        aliases = {len(args) - 1: 0}
    return pl.pallas_call(
        functools.partial(_combine_body, tc=tc, final_norm=final_norm),
        grid=(steps,),
        in_specs=in_specs,
        out_specs=pl.BlockSpec((tc, D_MODEL), lambda i: (i + blk0, 0)),
        out_shape=jax.ShapeDtypeStruct((T, D_MODEL), F32),
        input_output_aliases=aliases,
        compiler_params=_cparams(("arbitrary",)),
        name="combine",
    )(*args)


def _pad_lanes(a, width=LANES):
    return jnp.pad(a, ((0, 0), (0, width - a.shape[-1])))


def _layer(x2, batch, seq, g_mix, w_in, conv_w, conv_b, w_rg_a, b_rg_a, w_rg_x, b_rg_x, lam, b_forget,
           w_lru_out, w_attn_out, w_out, g_ffn, w_rgrp, b_rgrp, w_rexp, b_rexp, w1, w3, w2, g_out, final_norm):
    T = batch * seq
    row = lambda a: a.reshape(1, -1).astype(F32)

    fl0 = 5 * D_MODEL
    w_main = jnp.concatenate([w_in[:, :fl0], w_in[:, fl0 + N_HEADS:]], axis=1).astype(BF16)
    w_fl = _pad_lanes(w_in[:, fl0:fl0 + N_HEADS]).astype(BF16)
    z, fl = _inproj(x2, row(g_mix), w_main, w_fl)

    caug = _forget(fl, _pad_lanes(row(b_forget)), batch=batch, seq=seq)
    o = _attention(z, caug, batch=batch, seq=seq)

    hg = _lru(z, conv_w.astype(F32), row(conv_b), w_rg_a.astype(BF16), w_rg_x.astype(BF16),
              row(b_rg_a), row(b_rg_x), row(lam), batch=batch, seq=seq)

    wr = _pad_lanes(jnp.concatenate([w_rgrp, w_rexp], axis=1)).astype(BF16)
    br = _pad_lanes(jnp.concatenate([row(b_rgrp), row(b_rexp)], axis=1))
    x1, v, ri, rt, cnt = _merge(hg, o, z, x2, w_lru_out.astype(BF16), w_attn_out.astype(BF16),
                            w_out.astype(BF16), row(g_ffn), wr, br)

    v3 = v.reshape(T, PACK_CHUNKS, LANES)
    g_out = row(g_out)
    Tp = T // MOE_PARTS
    n_blocks = (Tp * TOP_K + N_EXPERTS * (EXPERT_ROWS - 1) + EXPERT_ROWS - 1) // EXPERT_ROWS
    blk_start = jnp.arange(n_blocks, dtype=jnp.int32) * EXPERT_ROWS
    out = None
    for part in range(MOE_PARTS):
        tok0 = part * Tp
        e = rt[0:TOP_K, tok0:tok0 + Tp].astype(jnp.int32)
        rank = rt[4:4 + TOP_K, tok0:tok0 + Tp].astype(jnp.int32)
        counts = cnt[part, 0, R_EXP0:R_EXP0 + N_EXPERTS].astype(jnp.int32)
        padded = (counts + EXPERT_ROWS - 1) // EXPERT_ROWS * EXPERT_ROWS
        pad_end = jnp.cumsum(padded)
        pad_start = pad_end - padded
        seg_start = jnp.zeros_like(e)
        for j in range(N_EXPERTS):
            seg_start = jnp.where(e == j, pad_start[j], seg_start)
        dest2 = (seg_start + rank).astype(jnp.int32)
        dests = [dest2[k] for k in range(TOP_K)]
        blk_expert = jnp.minimum(
            jnp.sum((pad_end[None, :] <= blk_start[:, None]).astype(jnp.int32), axis=1), N_EXPERTS - 1)
        n_used = (pad_end[-1:] // EXPERT_ROWS).astype(jnp.int32)
        fill = jnp.concatenate([pad_start + counts, padded - counts, n_used]).astype(jnp.int32)

        xs = _sc_scatter_rows(v3, dests, n_blocks * EXPERT_ROWS, tok0)
        xs = _fill_padding(fill, xs)
        y = _experts(blk_expert, n_used, xs.reshape(-1, LANES), w1, w3, w2)
        yk = _sc_gather_rows(y.reshape(-1, ROW_CHUNKS, LANES), dests)
        out = _combine(yk.reshape(TOP_K, Tp * ROW_CHUNKS, LANES), x1, ri, g_out, out,
                       tok0=tok0, final_norm=final_norm)
    return out


def kernel(x, g_mix, w_in, conv_w, conv_b, w_rg_a, b_rg_a, w_rg_x, b_rg_x, lru_lambda, b_forget, w_lru_out, w_attn_out, w_out, g_ffn, w_route_group, b_route_group, w_route_expert, b_route_expert, w_exp_gate, w_exp_up, w_exp_down, g_final):
    batch, seq, _ = x.shape
    depth = g_mix.shape[0]
    x2 = x.reshape(batch * seq, D_MODEL)
    for l in range(depth):
        x2 = _layer(
            x2, batch, seq, g_mix[l], w_in[l], conv_w[l], conv_b[l], w_rg_a[l], b_rg_a[l], w_rg_x[l],
            b_rg_x[l], lru_lambda[l], b_forget[l], w_lru_out[l], w_attn_out[l], w_out[l], g_ffn[l],
            w_route_group[l], b_route_group[l], w_route_expert[l], b_route_expert[l],
            w_exp_gate[l], w_exp_up[l], w_exp_down[l], g_final, l == depth - 1)
    return x2.reshape(batch, seq, D_MODEL)
```

```python
import functools

import jax
import jax.numpy as jnp
from jax import lax
from jax.experimental import pallas as pl
from jax.experimental.pallas import tpu as pltpu
from jax.experimental.pallas import tpu_sc as plsc

F32 = jnp.float32
BF16 = jnp.bfloat16

D_MODEL = 1024
LRU_BLOCK_W = 256
LRU_BLOCKS = D_MODEL // LRU_BLOCK_W
CONV_W = 4
LRU_C = 8.0
N_HEADS = 8
HEAD_DIM = D_MODEL // N_HEADS
N_GROUPS = 4
EXPERTS_PER_GROUP = 8
N_EXPERTS = N_GROUPS * EXPERTS_PER_GROUP
TOP_K = 2
D_EXPERT = D_MODEL // 2
EPS = 1e-6

LANES = 128
SUBLANES = 8
PACK_CHUNKS = D_MODEL // LANES // 2
U32 = jnp.uint32
VMEM_LIMIT = 48 * 1024 * 1024

ZC_XR, ZC_GR, ZC_Q, ZC_K, ZC_V, ZC_ML, ZC_MA = 0, 8, 16, 24, 32, 40, 48
Z_WIDTH = 7 * D_MODEL

R_EXP0 = N_GROUPS

EXPERT_ROWS = 256


def _cparams(sem):
    return pltpu.CompilerParams(dimension_semantics=sem, vmem_limit_bytes=VMEM_LIMIT)


def _store_packed_rows(ref, first_row, x):
    n = x.shape[0]
    bits = lax.bitcast_convert_type(x.astype(BF16).astype(F32), U32)
    half = PACK_CHUNKS * LANES
    for s in range(PACK_CHUNKS):
        word = (bits[:, s * LANES:(s + 1) * LANES] >> 16) | bits[:, half + s * LANES:half + (s + 1) * LANES]
        ref[pl.ds(first_row + s, n, stride=PACK_CHUNKS), :] = word


def _load_packed_rows(ref, n, dtype):
    words = [ref[pl.ds(s, n, stride=PACK_CHUNKS), :] for s in range(PACK_CHUNKS)]
    low = [lax.bitcast_convert_type(w << 16, F32).astype(dtype) for w in words]
    high = [lax.bitcast_convert_type(w & jnp.uint32(0xFFFF0000), F32).astype(dtype) for w in words]
    return jnp.concatenate(low + high, axis=-1)


LOG2E = 1.4426950408889634
Q_PRESCALE = HEAD_DIM ** -0.5 * LOG2E


def _inproj_body(x_ref, g_ref, w_ref, wfl_ref, z_ref, fl_ref, u_ref, *, q_block):
    j = pl.program_id(1)

    @pl.when(j == 0)
    def _():
        x = x_ref[...]
        ms = jnp.mean(x * x, axis=-1, keepdims=True)
        u = (x * lax.rsqrt(ms + EPS) * g_ref[...]).astype(BF16)
        u_ref[...] = u
        fl_ref[...] = jnp.dot(u, wfl_ref[...], preferred_element_type=F32)

    @pl.when(j == q_block)
    def _():
        acc = jnp.dot(u_ref[...], w_ref[...], preferred_element_type=F32)
        z_ref[...] = (acc * Q_PRESCALE).astype(BF16)

    @pl.when(j != q_block)
    def _():
        z_ref[...] = jnp.dot(u_ref[...], w_ref[...], preferred_element_type=F32).astype(BF16)


def _inproj(x2, g, w_main, w_fl, *, tm=2048, tn=D_MODEL):
    T = x2.shape[0]
    return pl.pallas_call(
        functools.partial(_inproj_body, q_block=ZC_Q * LANES // tn),
        grid=(T // tm, Z_WIDTH // tn),
        in_specs=[
            pl.BlockSpec((tm, D_MODEL), lambda i, j: (i, 0)),
            pl.BlockSpec((1, D_MODEL), lambda i, j: (0, 0)),
            pl.BlockSpec((D_MODEL, tn), lambda i, j: (0, j)),
            pl.BlockSpec((D_MODEL, LANES), lambda i, j: (0, 0)),
        ],
        out_specs=[
            pl.BlockSpec((tm, tn), lambda i, j: (i, j)),
            pl.BlockSpec((tm, LANES), lambda i, j: (i, 0)),
        ],
        out_shape=[
            jax.ShapeDtypeStruct((T, Z_WIDTH), BF16),
            jax.ShapeDtypeStruct((T, LANES), F32),
        ],
        scratch_shapes=[pltpu.VMEM((tm, D_MODEL), BF16)],
        compiler_params=_cparams(("arbitrary", "arbitrary")),
        name="inproj",
    )(x2, g, w_main, w_fl)


C_TERMS = 3


def _log_sigmoid(z):
    return jnp.minimum(z, 0.0) - jnp.log1p(jnp.exp(-jnp.abs(z)))


def _forget_body(fl_ref, b_ref, c_ref, *, seq):
    lf = _log_sigmoid(fl_ref[...] + b_ref[...])
    row = lax.broadcasted_iota(jnp.int32, lf.shape, 0)
    lane = lax.broadcasted_iota(jnp.int32, lf.shape, 1)
    c = lf
    k = 1
    while k < seq:
        c = c + jnp.where(row >= k, pltpu.roll(c, k, axis=0), 0.0)
        k *= 2
    rem = c * LOG2E
    out = jnp.zeros(lf.shape, F32)
    for n in range(C_TERMS):
        t = rem.astype(BF16).astype(F32)
        rem = rem - t
        shifted = t if n == 0 else pltpu.roll(t, n * N_HEADS, axis=1)
        out = jnp.where((lane >= n * N_HEADS) & (lane < (n + 1) * N_HEADS), shifted, out)
    c_ref[...] = out.astype(BF16)


def _forget(fl, b_pad, *, batch, seq):
    return pl.pallas_call(
        functools.partial(_forget_body, seq=seq),
        grid=(batch,),
        in_specs=[
            pl.BlockSpec((seq, LANES), lambda b: (b, 0)),
            pl.BlockSpec((1, LANES), lambda b: (0, 0)),
        ],
        out_specs=pl.BlockSpec((seq, LANES), lambda b: (b, 0)),
        out_shape=jax.ShapeDtypeStruct((batch * seq, LANES), BF16),
        compiler_params=_cparams(("arbitrary",)),
        name="forget",
    )(fl, b_pad)


ATTN_HEADS_PER_STEP = 4


def _attn_body(q_ref, k_ref, v_ref, kc_ref, o_ref, *, tq):
    i = pl.program_id(2)
    lane = lax.broadcasted_iota(jnp.int32, (tq, LANES), 1)
    heads = range(ATTN_HEADS_PER_STEP)
    cols =[slice(g * HEAD_DIM, (g + 1) * HEAD_DIM) for g in heads]
    qs = []
    for g in heads:
        h = pl.program_id(1) * ATTN_HEADS_PER_STEP + g
        mine = (lane < C_TERMS * N_HEADS) & ((lane & (N_HEADS - 1)) == h)
        qc = jnp.where(mine, -1.0, 0.0).astype(BF16)
        qs.append(jnp.concatenate([q_ref[:, cols[g]], qc], axis=1))

    def step(j, carry, masked):
        off = pl.multiple_of(j * tq, tq)
        out = []
        for g in heads:
            m, l, acc = carry[g]
            kj = jnp.concatenate([k_ref[pl.ds(off, tq), cols[g]], kc_ref[pl.ds(off, tq), :]], axis=1)
            s = lax.dot_general(qs[g], kj, (((1,), (1,)), ((), ())), preferred_element_type=F32)
            if masked:
                r = lax.broadcasted_iota(jnp.int32, s.shape, 0)
                cidx = lax.broadcasted_iota(jnp.int32, s.shape, 1)
                s = jnp.where(cidx <= r, s, -jnp.inf)
            m_new = jnp.maximum(m, jnp.max(s, axis=-1, keepdims=True))
            p = jnp.exp2(s - m_new)
            alpha = jnp.exp2(m - m_new)
            l = alpha * l + jnp.sum(p, axis=-1, keepdims=True)
            acc = alpha * acc + jnp.dot(p.astype(BF16), v_ref[pl.ds(off, tq), cols[g]],
                                        preferred_element_type=F32)
            out.append((m_new, l, acc))
        return tuple(out)

    init = tuple((jnp.full((tq, 1), -jnp.inf, F32), jnp.zeros((tq, 1), F32), jnp.zeros((tq, HEAD_DIM), F32))
                 for _ in heads)
    carry = lax.fori_loop(0, i, lambda j, c: step(j, c, False), init)
    carry = step(i, carry, True)
    for g in heads:
        _, l, acc = carry[g]
        o_ref[:, cols[g]] = (acc / l).astype(BF16)


def _attention(z, caug, *, batch, seq, tq=512):
    nq = seq // tq
    T = batch * seq
    G = ATTN_HEADS_PER_STEP
    W = G * HEAD_DIM
    return pl.pallas_call(
        functools.partial(_attn_body, tq=tq),
        grid=(batch, N_HEADS // G, nq),
        in_specs=[
            pl.BlockSpec((tq, W), lambda b, h, i: (b * nq + i, ZC_Q // G + h)),
            pl.BlockSpec((seq, W), lambda b, h, i: (b, ZC_K // G + h)),
            pl.BlockSpec((seq, W), lambda b, h, i: (b, ZC_V // G + h)),
            pl.BlockSpec((seq, LANES), lambda b, h, i: (b, 0)),
        ],
        out_specs=pl.BlockSpec((tq, W), lambda b, h, i: (b * nq + i, h)),
        out_shape=jax.ShapeDtypeStruct((T, D_MODEL), BF16),
        compiler_params=_cparams(("arbitrary", "arbitrary", "arbitrary")),
        name="attn",
    )(z, z, z, caug)


def _lru_body(xr_ref, gr_ref, cw_ref, cb_ref, wa_ref, wx_ref, ba_ref, bx_ref, lam_ref,
              hg_ref, xbuf, hcar, *, ts):
    i = pl.program_id(1)

    @pl.when(i == 0)
    def _():
        xbuf[0:SUBLANES, :] = jnp.zeros((SUBLANES, D_MODEL), F32)
        hcar[...] = jnp.zeros_like(hcar)

    @pl.when(i > 0)
    def _():
        xbuf[0:SUBLANES, :] = xbuf[ts:ts + SUBLANES, :]

    xbuf[SUBLANES:ts + SUBLANES, :] = xr_ref[...].astype(F32)

    base = SUBLANES - (CONV_W - 1)
    xc = cw_ref[0:1, :] * xbuf[base:base + ts, :]
    for k in range(1, CONV_W):
        xc = xc + cw_ref[k:k + 1, :] * xbuf[base + k:base + k + ts, :]
    xc = xc + cb_ref[...]

    xcb = xc.astype(BF16)
    ra = jnp.concatenate(
        [jnp.dot(xcb[:, n * LRU_BLOCK_W:(n + 1) * LRU_BLOCK_W], wa_ref[n], preferred_element_type=F32)
         for n in range(LRU_BLOCKS)], axis=-1)
    rx = jnp.concatenate(
        [jnp.dot(xcb[:, n * LRU_BLOCK_W:(n + 1) * LRU_BLOCK_W], wx_ref[n], preferred_element_type=F32)
         for n in range(LRU_BLOCKS)], axis=-1)
    r = jax.nn.sigmoid(ra + ba_ref[...])
    ig = jax.nn.sigmoid(rx + bx_ref[...])
    nlam = -lam_ref[...]
    softplus = jnp.maximum(nlam, 0.0) + jnp.log1p(jnp.exp(-jnp.abs(nlam)))
    log_a = (-LRU_C * r) * softplus
    a = jnp.exp(log_a)
    th = jnp.tanh(log_a)
    mult = jnp.sqrt(-2.0 * th / (1.0 - th))
    b = mult * ig * xc

    row = lax.broadcasted_iota(jnp.int32, (SUBLANES, D_MODEL), 0)
    keeps = [(k, row >= k) for k in (1, 2, 4)]
    hprev = jnp.broadcast_to(hcar[...], (SUBLANES, D_MODEL))
    pieces = []
    for j in range(ts // SUBLANES):
        aj = a[j * SUBLANES:(j + 1) * SUBLANES, :]
        bj = b[j * SUBLANES:(j + 1) * SUBLANES, :]
        for k, keep in keeps:
            a_sh = jnp.where(keep, pltpu.roll(aj, k, axis=0), 1.0)
            b_sh = jnp.where(keep, pltpu.roll(bj, k, axis=0), 0.0)
            bj = aj * b_sh + bj
            aj = aj * a_sh
        hj = bj + aj * hprev
        hprev = jnp.broadcast_to(hj[SUBLANES - 1:SUBLANES, :], (SUBLANES, D_MODEL))
        pieces.append(hj)
    h = jnp.concatenate(pieces, axis=0)
    hcar[...] = h[ts - 1:ts, :]

    hg_ref[...] = (h * jax.nn.gelu(gr_ref[...].astype(F32))).astype(BF16)


def _lru(z, conv_w, conv_b, wa, wx, ba, bx, lam, *, batch, seq, ts=256):
    ns = seq // ts
    T = batch * seq
    full = lambda shape: pl.BlockSpec(shape, lambda b, i: (0,) * len(shape))
    return pl.pallas_call(
        functools.partial(_lru_body, ts=ts),
        grid=(batch, ns),
        in_specs=[
            pl.BlockSpec((ts, D_MODEL), lambda b, i: (b * ns + i, ZC_XR // SUBLANES)),
            pl.BlockSpec((ts, D_MODEL), lambda b, i: (b * ns + i, ZC_GR // SUBLANES)),
            full((CONV_W, D_MODEL)),
            full((1, D_MODEL)),
            full((LRU_BLOCKS, LRU_BLOCK_W, LRU_BLOCK_W)),
            full((LRU_BLOCKS, LRU_BLOCK_W, LRU_BLOCK_W)),
            full((1, D_MODEL)),
            full((1, D_MODEL)),
            full((1, D_MODEL)),
        ],
        out_specs=pl.BlockSpec((ts, D_MODEL), lambda b, i: (b * ns + i, 0)),
        out_shape=jax.ShapeDtypeStruct((T, D_MODEL), BF16),
        scratch_shapes=[pltpu.VMEM((ts + SUBLANES, D_MODEL), F32), pltpu.VMEM((1, D_MODEL), F32)],
        compiler_params=_cparams(("arbitrary", "arbitrary")),
        name="lru",
    )(z, z, conv_w, conv_b, wa, wx, ba, bx, lam)


MERGE_SUBTILES = 1
MOE_PARTS = 2


def _merge_body(hg_ref, o_ref, ml_ref, ma_ref, x_ref, wl_ref, wat_ref, wo_ref, g_ref, wr_ref, br_ref,
                x1_ref, v_ref, ri_ref, rt_ref, cnt_ref, carry_ref, *, tm, steps_per_part):
    @pl.when(pl.program_id(0) % steps_per_part == 0)
    def _():
        carry_ref[...] = jnp.zeros_like(carry_ref)

    ts = tm // MERGE_SUBTILES
    lane = lax.broadcasted_iota(jnp.int32, (ts, LANES), 1)
    rr = lax.broadcasted_iota(jnp.int32, (ts, ts), 0)
    cc = lax.broadcasted_iota(jnp.int32, (ts, ts), 1)
    tri = jnp.where(cc < rr, 1.0, 0.0).astype(BF16)
    ninf = -jnp.inf
    big = jnp.int32(1 << 20)
    carry = carry_ref[...]

    for sub in range(MERGE_SUBTILES):
        rows = slice(sub * ts, (sub + 1) * ts)
        yl = jnp.dot(hg_ref[rows, :], wl_ref[...], preferred_element_type=F32)
        ya = jnp.dot(o_ref[rows, :], wat_ref[...], preferred_element_type=F32)
        merged = (jax.nn.sigmoid(ml_ref[rows, :].astype(F32)) * yl
                  + jax.nn.sigmoid(ma_ref[rows, :].astype(F32)) * ya)
        x1 = x_ref[rows, :] + jnp.dot(merged.astype(BF16), wo_ref[...], preferred_element_type=F32)
        x1_ref[rows, :] = x1
        ms = jnp.mean(x1 * x1, axis=-1, keepdims=True)
        v = x1 * lax.rsqrt(ms + EPS) * g_ref[...]
        _store_packed_rows(v_ref, sub * ts * PACK_CHUNKS, v)

        logits = jnp.dot(v.astype(BF16), wr_ref[...], preferred_element_type=F32) + br_ref[...]

        gl = jnp.where(lane < N_GROUPS, logits, ninf)
        gmax = jnp.max(gl, axis=-1, keepdims=True)
        gsel = jnp.min(jnp.where(gl == gmax, lane, big), axis=-1, keepdims=True)
        pg = 1.0 / jnp.sum(jnp.exp(gl - gmax), axis=-1, keepdims=True)

        lo = R_EXP0 + gsel * EXPERTS_PER_GROUP
        el = jnp.where(lane >= lo, jnp.where(lane < lo + EXPERTS_PER_GROUP, logits, ninf), ninf)
        v1 = jnp.max(el, axis=-1, keepdims=True)
        i1 = jnp.min(jnp.where(el == v1, lane, big), axis=-1, keepdims=True)
        el2 = jnp.where(lane == i1, ninf, el)
        v2 = jnp.max(el2, axis=-1, keepdims=True)
        i2 = jnp.min(jnp.where(el2 == v2, lane, big), axis=-1, keepdims=True)
        e21 = jnp.exp(v2 - v1)
        p1 = 1.0 / (1.0 + e21)
        w0 = pg * p1
        w1 = pg * (e21 * p1)

        hit0 = lane == i1
        hit1 = lane == i2
        onehot = jnp.where(hit0, 1.0, jnp.where(hit1, 1.0, 0.0))
        before = jnp.dot(tri, onehot.astype(BF16), preferred_element_type=F32) + carry
        rank0 = jnp.sum(jnp.where(hit0, before, 0.0), axis=-1, keepdims=True)
        rank1 = jnp.sum(jnp.where(hit1, before, 0.0), axis=-1, keepdims=True)
        carry = carry + jnp.sum(onehot, axis=0, keepdims=True)

        e0 = (i1 - R_EXP0).astype(F32)
        e1 = (i2 - R_EXP0).astype(F32)
        ri = jnp.where(lane == 0, e0,
             jnp.where(lane == 1, e1,
             jnp.where(lane == 2, w0,
             jnp.where(lane == 3, w1,
             jnp.where(lane == 4, rank0,
             jnp.where(lane == 5, rank1, 0.0))))))
        ri_ref[rows, :] = ri
        rt_ref[:, rows] = ri.T[:SUBLANES, :]

    carry_ref[...] = carry
    cnt_ref[0] = carry


def _merge(hg, o, z, x2, wl, wat, wo, g, wr, br, *, tm=512):
    T = x2.shape[0]
    steps_per_part = T // tm // MOE_PARTS
    full = lambda shape: pl.BlockSpec(shape, lambda i: (0,) * len(shape))
    return pl.pallas_call(
        functools.partial(_merge_body, tm=tm, steps_per_part=steps_per_part),
        grid=(T // tm,),
        in_specs=[
            pl.BlockSpec((tm, D_MODEL), lambda i: (i, 0)),
            pl.BlockSpec((tm, D_MODEL), lambda i: (i, 0)),
            pl.BlockSpec((tm, D_MODEL), lambda i: (i, ZC_ML // SUBLANES)),
            pl.BlockSpec((tm, D_MODEL), lambda i: (i, ZC_MA // SUBLANES)),
            pl.BlockSpec((tm, D_MODEL), lambda i: (i, 0)),
            full((D_MODEL, D_MODEL)),
            full((D_MODEL, D_MODEL)),
            full((D_MODEL, D_MODEL)),
            full((1, D_MODEL)),
            full((D_MODEL, LANES)),
            full((1, LANES)),
        ],
        out_specs=[
            pl.BlockSpec((tm, D_MODEL), lambda i: (i, 0)),
            pl.BlockSpec((tm * PACK_CHUNKS, LANES), lambda i: (i, 0)),
            pl.BlockSpec((tm, LANES), lambda i: (i, 0)),
            pl.BlockSpec((SUBLANES, tm), lambda i: (0, i)),
            pl.BlockSpec((1, 1, LANES), lambda i: (i // steps_per_part, 0, 0)),
        ],
        out_shape=[
            jax.ShapeDtypeStruct((T, D_MODEL), F32),
            jax.ShapeDtypeStruct((T * PACK_CHUNKS, LANES), U32),
            jax.ShapeDtypeStruct((T, LANES), F32),
            jax.ShapeDtypeStruct((SUBLANES, T), F32),
            jax.ShapeDtypeStruct((MOE_PARTS, 1, LANES), F32),
        ],
        scratch_shapes=[pltpu.VMEM((1, LANES), F32)],
        compiler_params=_cparams(("arbitrary",)),
        name="merge",
    )(hg, o, z, z, x2, wl, wat, wo, g, wr, br)


def _fill_padding_body(fill_ref, xs_in_ref, xs_ref, zblock, zsem, *, n_blocks):
    del xs_in_ref

    def for_each_padding_piece(fn):
        def per_expert(e, carry):
            start = fill_ref[e]
            n = fill_ref[N_EXPERTS + e]
            size = EXPERT_ROWS // 2
            while size >= 1:
                take = (n & size) != 0

                @pl.when(take)
                def _(start=start, size=size):
                    fn(pltpu.make_async_copy(zblock.at[pl.ds(0, size)], xs_ref.at[pl.ds(start, size)], zsem))

                start = start + jnp.where(take, size, 0)
                size //= 2
            return carry
        lax.fori_loop(0, N_EXPERTS, per_expert, 0)

    def zero_block_copy(blk):
        return pltpu.make_async_copy(zblock, xs_ref.at[pl.ds(blk * EXPERT_ROWS, EXPERT_ROWS)], zsem)

    def for_each_unused_block(fn):
        lax.fori_loop(fill_ref[2 * N_EXPERTS], n_blocks, lambda blk, c: (fn(zero_block_copy(blk)), c)[1], 0)

    zblock[...] = jnp.zeros_like(zblock)
    for_each_padding_piece(lambda cp: cp.start())
    for_each_unused_block(lambda cp: cp.start())
    for_each_padding_piece(lambda cp: cp.wait())
    for_each_unused_block(lambda cp: cp.wait())


def _fill_padding(fill, xs3):
    n_blocks = xs3.shape[0] // EXPERT_ROWS
    grid_spec = pltpu.PrefetchScalarGridSpec(
        num_scalar_prefetch=1,
        grid=(1,),
        in_specs=[pl.BlockSpec(memory_space=pl.ANY)],
        out_specs=pl.BlockSpec(memory_space=pl.ANY),
        scratch_shapes=[pltpu.VMEM((EXPERT_ROWS,) + xs3.shape[1:], xs3.dtype),
                        pltpu.SemaphoreType.DMA(())],
    )
    return pl.pallas_call(
        functools.partial(_fill_padding_body, n_blocks=n_blocks),
        grid_spec=grid_spec,
        out_shape=jax.ShapeDtypeStruct(xs3.shape, xs3.dtype),
        input_output_aliases={1: 0},
        compiler_params=_cparams(("arbitrary",)),
        name="fill_padding",
    )(fill, xs3)


def _expert_body(be_ref, nu_ref, x_ref, w1_ref, w3_ref, w2_ref, y_ref, w13b, w2b, *, rows):
    i = pl.program_id(0)

    @pl.when((i == 0) | (be_ref[i] != be_ref[jnp.maximum(i - 1, 0)]))
    def _():
        w13b[:, :D_EXPERT] = w1_ref[0].astype(BF16)
        w13b[:, D_EXPERT:] = w3_ref[0].astype(BF16)
        w2b[...] = w2_ref[0].astype(BF16)

    @pl.when(i < nu_ref[0])
    def _():
        xb = _load_packed_rows(x_ref, rows, BF16)
        gu = jnp.dot(xb, w13b[...], preferred_element_type=F32)
        hb = (jax.nn.silu(gu[:, :D_EXPERT]) * gu[:, D_EXPERT:]).astype(BF16)
        y = jnp.dot(hb, w2b[...], preferred_element_type=F32)
        _store_packed_rows(y_ref, 0, y)

    @pl.when(i >= nu_ref[0])
    def _():
        y_ref[...] = jnp.zeros_like(y_ref)


def _experts(blk_expert, n_used, xs, w1, w3, w2, *, rows=EXPERT_ROWS):
    n_blocks = xs.shape[0] // (rows * PACK_CHUNKS)
    grid_spec = pltpu.PrefetchScalarGridSpec(
        num_scalar_prefetch=2,
        grid=(n_blocks,),
        in_specs=[
            pl.BlockSpec((rows * PACK_CHUNKS, LANES), lambda i, be, nu: (jnp.minimum(i, nu[0] - 1), 0)),
            pl.BlockSpec((1, D_MODEL, D_EXPERT), lambda i, be, nu: (be[i], 0, 0)),
            pl.BlockSpec((1, D_MODEL, D_EXPERT), lambda i, be, nu: (be[i], 0, 0)),
            pl.BlockSpec((1, D_EXPERT, D_MODEL), lambda i, be, nu: (be[i], 0, 0)),
        ],
        out_specs=pl.BlockSpec((rows * PACK_CHUNKS, LANES), lambda i, be, nu: (i, 0)),
        scratch_shapes=[pltpu.VMEM((D_MODEL, 2 * D_EXPERT), BF16), pltpu.VMEM((D_EXPERT, D_MODEL), BF16)],
    )
    return pl.pallas_call(
        functools.partial(_expert_body, rows=rows),
        grid_spec=grid_spec,
        out_shape=jax.ShapeDtypeStruct(xs.shape, U32),
        compiler_params=_cparams(("arbitrary",)),
        name="experts",
    )(blk_expert, n_used, xs, w1, w3, w2)


SC_CORES = 2
SC_SUBCORES = 16
SC_WORKERS = SC_CORES * SC_SUBCORES
SC_CHUNK = 32


def _sc_worker_base(per_worker):
    return (lax.axis_index("s") * SC_CORES + lax.axis_index("c")) * per_worker


def _sc_scatter_rows(v3, dests, n_rows, tok0):
    per_worker = dests[0].shape[0] // SC_WORKERS
    mesh = plsc.VectorSubcoreMesh(core_axis_name="c", subcore_axis_name="s")
    bufs = range(2)
    slab = v3.shape[1:]

    @functools.partial(
        pl.kernel, mesh=mesh,
        out_type=jax.ShapeDtypeStruct((n_rows,) + slab, v3.dtype),
        scratch_types=[pltpu.VMEM((2 * TOP_K, SC_CHUNK), jnp.int32),
                       pltpu.VMEM((2, SC_CHUNK) + slab, v3.dtype),
                       pltpu.SemaphoreType.DMA((2,)),
                       pltpu.SemaphoreType.DMA((2,))],
        name="sc_scatter")
    def scatter(v_hbm, d0_hbm, d1_hbm, xs_hbm, idx, rows, lsem, ssem):
        base = _sc_worker_base(per_worker)

        @pl.loop(0, per_worker, step=2 * SC_CHUNK)
        def _(c):
            loads = []
            for b in bufs:
                off = base + c + b * SC_CHUNK
                loads.append(pltpu.async_copy(v_hbm.at[pl.ds(tok0 + off, SC_CHUNK)], rows.at[b], lsem.at[b]))
                for k, d_hbm in enumerate((d0_hbm, d1_hbm)):
                    pltpu.sync_copy(d_hbm.at[pl.ds(off, SC_CHUNK)], idx.at[b * TOP_K + k])
            stores = []
            for b in bufs:
                loads[b].wait()
                for k in range(TOP_K):
                    stores.append(pltpu.async_copy(rows.at[b], xs_hbm.at[idx.at[b * TOP_K + k]], ssem.at[b]))
            for st in stores:
                st.wait()

    return scatter(v3, *dests)


def _sc_gather_rows(y3, dests):
    T = dests[0].shape[0]
    per_worker = T // SC_WORKERS
    mesh = plsc.VectorSubcoreMesh(core_axis_name="c", subcore_axis_name="s")
    bufs = range(2)
    slab = y3.shape[1:]

    @functools.partial(
        pl.kernel, mesh=mesh,
        out_type=jax.ShapeDtypeStruct((TOP_K, T) + slab, y3.dtype),
        scratch_types=[pltpu.VMEM((2, SC_CHUNK), jnp.int32),
                       pltpu.VMEM((2, SC_CHUNK) + slab, y3.dtype),
                       pltpu.SemaphoreType.DMA((2,)),
                       pltpu.SemaphoreType.DMA((2,))],
        name="sc_gather")
    def gather(y_hbm, d0_hbm, d1_hbm, out_hbm, idx, rows, gsem, wsem):
        base = _sc_worker_base(per_worker)
        for k, d_hbm in enumerate((d0_hbm, d1_hbm)):
            @pl.loop(0, per_worker, step=2 * SC_CHUNK)
            def _(c):
                gathers = []
                for b in bufs:
                    off = base + c + b * SC_CHUNK
                    pltpu.sync_copy(d_hbm.at[pl.ds(off, SC_CHUNK)], idx.at[b])
                    gathers.append(pltpu.async_copy(y_hbm.at[idx.at[b]], rows.at[b], gsem.at[b]))
                writes = []
                for b in bufs:
                    off = base + c + b * SC_CHUNK
                    gathers[b].wait()
                    writes.append(pltpu.async_copy(rows.at[b], out_hbm.at[k, pl.ds(off, SC_CHUNK)], wsem.at[b]))
                for w in writes:
                    w.wait()

    return gather(y3, *dests)


def _combine_body(yk_ref, x1_ref, ri_ref, g_ref, *rest, tc, final_norm):
    out_ref = rest[-1]

    ri = ri_ref[...]
    w0 = ri[:, 2:3]
    w1 = ri[:, 3:4]
    x2 = x1_ref[...] + (w0 * _load_packed_rows(yk_ref.at[0], tc, F32) + w1 * _load_packed_rows(yk_ref.at[1], tc, F32))
    if final_norm:
        ms = jnp.mean(x2 * x2, axis=-1, keepdims=True)
        x2 = x2 * lax.rsqrt(ms + EPS) * g_ref[...]
    out_ref[...] = x2


def _combine(yk, x1, ri, g, out_prev, *, tok0, final_norm, tc=256):
    T = x1.shape[0]
    steps = yk.shape[1] // (tc * PACK_CHUNKS)
    blk0 = tok0 // tc
    in_specs = [
        pl.BlockSpec((TOP_K, tc * PACK_CHUNKS, LANES), lambda i: (0, i, 0)),
        pl.BlockSpec((tc, D_MODEL), lambda i: (i + blk0, 0)),
        pl.BlockSpec((tc, LANES), lambda i: (i + blk0, 0)),
        pl.BlockSpec((1, D_MODEL), lambda i: (0, 0)),
    ]
    args = [yk, x1, ri, g]
    aliases = {}
    if out_prev is not None:
        in_specs.append(pl.BlockSpec(memory_space=pl.ANY))
        args.append(out_prev)
        aliases = {len(args) - 1: 0}
    return pl.pallas_call(
        functools.partial(_combine_body, tc=tc, final_norm=final_norm),
        grid=(steps,),
        in_specs=in_specs,
        out_specs=pl.BlockSpec((tc, D_MODEL), lambda i: (i + blk0, 0)),
        out_shape=jax.ShapeDtypeStruct((T, D_MODEL), F32),
        input_output_aliases=aliases,
        compiler_params=_cparams(("arbitrary",)),
        name="combine",
    )(*args)


def _pad_lanes(a, width=LANES):
    return jnp.pad(a, ((0, 0), (0, width - a.shape[-1])))


def _layer(x2, batch, seq, g_mix, w_in, conv_w, conv_b, w_rg_a, b_rg_a, w_rg_x, b_rg_x, lam, b_forget,
           w_lru_out, w_attn_out, w_out, g_ffn, w_rgrp, b_rgrp, w_rexp, b_rexp, w1, w3, w2, g_out, final_norm):
    T = batch * seq
    row = lambda a: a.reshape(1, -1).astype(F32)

    fl0 = 5 * D_MODEL
    w_main = jnp.concatenate([w_in[:, :fl0], w_in[:, fl0 + N_HEADS:]], axis=1).astype(BF16)
    w_fl = _pad_lanes(w_in[:, fl0:fl0 + N_HEADS]).astype(BF16)
    z, fl = _inproj(x2, row(g_mix), w_main, w_fl)

    caug = _forget(fl, _pad_lanes(row(b_forget)), batch=batch, seq=seq)
    o = _attention(z, caug, batch=batch, seq=seq)

    hg = _lru(z, conv_w.astype(F32), row(conv_b), w_rg_a.astype(BF16), w_rg_x.astype(BF16),
              row(b_rg_a), row(b_rg_x), row(lam), batch=batch, seq=seq)

    wr = _pad_lanes(jnp.concatenate([w_rgrp, w_rexp], axis=1)).astype(BF16)
    br = _pad_lanes(jnp.concatenate([row(b_rgrp), row(b_rexp)], axis=1))
    x1, v, ri, rt, cnt = _merge(hg, o, z, x2, w_lru_out.astype(BF16), w_attn_out.astype(BF16),
                            w_out.astype(BF16), row(g_ffn), wr, br)

    v3 = v.reshape(T, PACK_CHUNKS, LANES)
    g_out = row(g_out)
    Tp = T // MOE_PARTS
    n_blocks = (Tp * TOP_K + N_EXPERTS * (EXPERT_ROWS - 1) + EXPERT_ROWS - 1) // EXPERT_ROWS
    blk_start = jnp.arange(n_blocks, dtype=jnp.int32) * EXPERT_ROWS
    out = None
    for part in range(MOE_PARTS):
        tok0 = part * Tp
        e = rt[0:TOP_K, tok0:tok0 + Tp].astype(jnp.int32)
        rank = rt[4:4 + TOP_K, tok0:tok0 + Tp].astype(jnp.int32)
        counts = cnt[part, 0, R_EXP0:R_EXP0 + N_EXPERTS].astype(jnp.int32)
        padded = (counts + EXPERT_ROWS - 1) // EXPERT_ROWS * EXPERT_ROWS
        pad_end = jnp.cumsum(padded)
        pad_start = pad_end - padded
        seg_start = jnp.zeros_like(e)
        for j in range(N_EXPERTS):
            seg_start = jnp.where(e == j, pad_start[j], seg_start)
        dest2 = (seg_start + rank).astype(jnp.int32)
        dests = [dest2[k] for k in range(TOP_K)]
        blk_expert = jnp.minimum(
            jnp.sum((pad_end[None, :] <= blk_start[:, None]).astype(jnp.int32), axis=1), N_EXPERTS - 1)
        n_used = (pad_end[-1:] // EXPERT_ROWS).astype(jnp.int32)
        fill = jnp.concatenate([pad_start + counts, padded - counts, n_used]).astype(jnp.int32)

        xs = _sc_scatter_rows(v3, dests, n_blocks * EXPERT_ROWS, tok0)
        xs = _fill_padding(fill, xs)
        y = _experts(blk_expert, n_used, xs.reshape(-1, LANES), w1, w3, w2)
        yk = _sc_gather_rows(y.reshape(-1, PACK_CHUNKS, LANES), dests)
        out = _combine(yk.reshape(TOP_K, Tp * PACK_CHUNKS, LANES), x1, ri, g_out, out,
                       tok0=tok0, final_norm=final_norm)
    return out


def kernel(x, g_mix, w_in, conv_w, conv_b, w_rg_a, b_rg_a, w_rg_x, b_rg_x, lru_lambda, b_forget, w_lru_out, w_attn_out, w_out, g_ffn, w_route_group, b_route_group, w_route_expert, b_route_expert, w_exp_gate, w_exp_up, w_exp_down, g_final):
    batch, seq, _ = x.shape
    depth = g_mix.shape[0]
    x2 = x.reshape(batch * seq, D_MODEL)
    for l in range(depth):
        x2 = _layer(
            x2, batch, seq, g_mix[l], w_in[l], conv_w[l], conv_b[l], w_rg_a[l], b_rg_a[l], w_rg_x[l],
            b_rg_x[l], lru_lambda[l], b_forget[l], w_lru_out[l], w_attn_out[l], w_out[l], g_ffn[l],
            w_route_group[l], b_route_group[l], w_route_expert[l], b_route_expert[l],
            w_exp_gate[l], w_exp_up[l], w_exp_down[l], g_final, l == depth - 1)
    return x2.reshape(batch, seq, D_MODEL)
```

```python
import functools

import jax
import jax.numpy as jnp
from jax import lax
from jax.experimental import pallas as pl
from jax.experimental.pallas import tpu as pltpu
from jax.experimental.pallas import tpu_sc as plsc

F32 = jnp.float32
BF16 = jnp.bfloat16

D_MODEL = 1024
LRU_BLOCK_W = 256
LRU_BLOCKS = D_MODEL // LRU_BLOCK_W
CONV_W = 4
LRU_C = 8.0
N_HEADS = 8
HEAD_DIM = D_MODEL // N_HEADS
N_GROUPS = 4
EXPERTS_PER_GROUP = 8
N_EXPERTS = N_GROUPS * EXPERTS_PER_GROUP
TOP_K = 2
D_EXPERT = D_MODEL // 2
EPS = 1e-6

LANES = 128
SUBLANES = 8
PACK_CHUNKS = D_MODEL // LANES // 2
U32 = jnp.uint32
VMEM_LIMIT = 48 * 1024 * 1024

ZC_XR, ZC_GR, ZC_Q, ZC_K, ZC_V, ZC_ML, ZC_MA = 0, 8, 16, 24, 32, 40, 48
Z_WIDTH = 7 * D_MODEL

R_EXP0 = N_GROUPS

EXPERT_ROWS = 256


def _cparams(sem):
    return pltpu.CompilerParams(dimension_semantics=sem, vmem_limit_bytes=VMEM_LIMIT)


def _store_packed_rows(ref, first_row, x):
    n = x.shape[0]
    bits = lax.bitcast_convert_type(x.astype(BF16).astype(F32), U32)
    half = PACK_CHUNKS * LANES
    for s in range(PACK_CHUNKS):
        word = (bits[:, s * LANES:(s + 1) * LANES] >> 16) | bits[:, half + s * LANES:half + (s + 1) * LANES]
        ref[pl.ds(first_row + s, n, stride=PACK_CHUNKS), :] = word


def _load_packed_rows(ref, n, dtype):
    words = [ref[pl.ds(s, n, stride=PACK_CHUNKS), :] for s in range(PACK_CHUNKS)]
    low = [lax.bitcast_convert_type(w << 16, F32).astype(dtype) for w in words]
    high = [lax.bitcast_convert_type(w & jnp.uint32(0xFFFF0000), F32).astype(dtype) for w in words]
    return jnp.concatenate(low + high, axis=-1)


LOG2E = 1.4426950408889634
Q_PRESCALE = HEAD_DIM ** -0.5 * LOG2E


def _inproj_body(x_ref, g_ref, w_ref, wfl_ref, z_ref, fl_ref, u_ref, *, q_block):
    j = pl.program_id(1)

    @pl.when(j == 0)
    def _():
        x = x_ref[...]
        ms = jnp.mean(x * x, axis=-1, keepdims=True)
        u = (x * lax.rsqrt(ms + EPS) * g_ref[...]).astype(BF16)
        u_ref[...] = u
        fl_ref[...] = jnp.dot(u, wfl_ref[...], preferred_element_type=F32)

    @pl.when(j == q_block)
    def _():
        acc = jnp.dot(u_ref[...], w_ref[...], preferred_element_type=F32)
        z_ref[...] = (acc * Q_PRESCALE).astype(BF16)

    @pl.when(j != q_block)
    def _():
        z_ref[...] = jnp.dot(u_ref[...], w_ref[...], preferred_element_type=F32).astype(BF16)


def _inproj(x2, g, w_main, w_fl, *, tm=2048, tn=D_MODEL):
    T = x2.shape[0]
    return pl.pallas_call(
        functools.partial(_inproj_body, q_block=ZC_Q * LANES // tn),
        grid=(T // tm, Z_WIDTH // tn),
        in_specs=[
            pl.BlockSpec((tm, D_MODEL), lambda i, j: (i, 0)),
            pl.BlockSpec((1, D_MODEL), lambda i, j: (0, 0)),
            pl.BlockSpec((D_MODEL, tn), lambda i, j: (0, j)),
            pl.BlockSpec((D_MODEL, LANES), lambda i, j: (0, 0)),
        ],
        out_specs=[
            pl.BlockSpec((tm, tn), lambda i, j: (i, j)),
            pl.BlockSpec((tm, LANES), lambda i, j: (i, 0)),
        ],
        out_shape=[
            jax.ShapeDtypeStruct((T, Z_WIDTH), BF16),
            jax.ShapeDtypeStruct((T, LANES), F32),
        ],
        scratch_shapes=[pltpu.VMEM((tm, D_MODEL), BF16)],
        compiler_params=_cparams(("arbitrary", "arbitrary")),
        name="inproj",
    )(x2, g, w_main, w_fl)


C_TERMS = 3


def _log_sigmoid(z):
    return jnp.minimum(z, 0.0) - jnp.log1p(jnp.exp(-jnp.abs(z)))


def _forget_body(fl_ref, b_ref, c_ref, *, seq):
    lf = _log_sigmoid(fl_ref[...] + b_ref[...])
    row = lax.broadcasted_iota(jnp.int32, lf.shape, 0)
    lane = lax.broadcasted_iota(jnp.int32, lf.shape, 1)
    c = lf
    k = 1
    while k < seq:
        c = c + jnp.where(row >= k, pltpu.roll(c, k, axis=0), 0.0)
        k *= 2
    rem = c * LOG2E
    out = jnp.zeros(lf.shape, F32)
    for n in range(C_TERMS):
        t = rem.astype(BF16).astype(F32)
        rem = rem - t
        shifted = t if n == 0 else pltpu.roll(t, n * N_HEADS, axis=1)
        out = jnp.where((lane >= n * N_HEADS) & (lane < (n + 1) * N_HEADS), shifted, out)
    c_ref[...] = out.astype(BF16)


def _forget(fl, b_pad, *, batch, seq):
    return pl.pallas_call(
        functools.partial(_forget_body, seq=seq),
        grid=(batch,),
        in_specs=[
            pl.BlockSpec((seq, LANES), lambda b: (b, 0)),
            pl.BlockSpec((1, LANES), lambda b: (0, 0)),
        ],
        out_specs=pl.BlockSpec((seq, LANES), lambda b: (b, 0)),
        out_shape=jax.ShapeDtypeStruct((batch * seq, LANES), BF16),
        compiler_params=_cparams(("arbitrary",)),
        name="forget",
    )(fl, b_pad)


ATTN_HEADS_PER_STEP = 4


def _attn_body(q_ref, k_ref, v_ref, kc_ref, o_ref, *, tq):
    i = pl.program_id(2)
    lane = lax.broadcasted_iota(jnp.int32, (tq, LANES), 1)
    heads = range(ATTN_HEADS_PER_STEP)
    cols =[slice(g * HEAD_DIM, (g + 1) * HEAD_DIM) for g in heads]
    qs = []
    for g in heads:
        h = pl.program_id(1) * ATTN_HEADS_PER_STEP + g
        mine = (lane < C_TERMS * N_HEADS) & ((lane & (N_HEADS - 1)) == h)
        qc = jnp.where(mine, -1.0, 0.0).astype(BF16)
        qs.append(jnp.concatenate([q_ref[:, cols[g]], qc], axis=1))

    def step(j, carry, masked):
        off = pl.multiple_of(j * tq, tq)
        out = []
        for g in heads:
            m, l, acc = carry[g]
            kj = jnp.concatenate([k_ref[pl.ds(off, tq), cols[g]], kc_ref[pl.ds(off, tq), :]], axis=1)
            s = lax.dot_general(qs[g], kj, (((1,), (1,)), ((), ())), preferred_element_type=F32)
            if masked:
                r = lax.broadcasted_iota(jnp.int32, s.shape, 0)
                cidx = lax.broadcasted_iota(jnp.int32, s.shape, 1)
                s = jnp.where(cidx <= r, s, -jnp.inf)
            m_new = jnp.maximum(m, jnp.max(s, axis=-1, keepdims=True))
            p = jnp.exp2(s - m_new)
            alpha = jnp.exp2(m - m_new)
            l = alpha * l + jnp.sum(p, axis=-1, keepdims=True)
            acc = alpha * acc + jnp.dot(p.astype(BF16), v_ref[pl.ds(off, tq), cols[g]],
                                        preferred_element_type=F32)
            out.append((m_new, l, acc))
        return tuple(out)

    init = tuple((jnp.full((tq, 1), -jnp.inf, F32), jnp.zeros((tq, 1), F32), jnp.zeros((tq, HEAD_DIM), F32))
                 for _ in heads)
    carry = lax.fori_loop(0, i, lambda j, c: step(j, c, False), init)
    carry = step(i, carry, True)
    for g in heads:
        _, l, acc = carry[g]
        o_ref[:, cols[g]] = (acc / l).astype(BF16)


def _attention(z, caug, *, batch, seq, tq=512):
    nq = seq // tq
    T = batch * seq
    G = ATTN_HEADS_PER_STEP
    W = G * HEAD_DIM
    return pl.pallas_call(
        functools.partial(_attn_body, tq=tq),
        grid=(batch, N_HEADS // G, nq),
        in_specs=[
            pl.BlockSpec((tq, W), lambda b, h, i: (b * nq + i, ZC_Q // G + h)),
            pl.BlockSpec((seq, W), lambda b, h, i: (b, ZC_K // G + h)),
            pl.BlockSpec((seq, W), lambda b, h, i: (b, ZC_V // G + h)),
            pl.BlockSpec((seq, LANES), lambda b, h, i: (b, 0)),
        ],
        out_specs=pl.BlockSpec((tq, W), lambda b, h, i: (b * nq + i, h)),
        out_shape=jax.ShapeDtypeStruct((T, D_MODEL), BF16),
        compiler_params=_cparams(("arbitrary", "arbitrary", "arbitrary")),
        name="attn",
    )(z, z, z, caug)


def _lru_body(xr_ref, gr_ref, cw_ref, cb_ref, wa_ref, wx_ref, ba_ref, bx_ref, lam_ref,
              hg_ref, xbuf, hcar, *, ts):
    i = pl.program_id(1)

    @pl.when(i == 0)
    def _():
        xbuf[0:SUBLANES, :] = jnp.zeros((SUBLANES, D_MODEL), F32)
        hcar[...] = jnp.zeros_like(hcar)

    @pl.when(i > 0)
    def _():
        xbuf[0:SUBLANES, :] = xbuf[ts:ts + SUBLANES, :]

    xbuf[SUBLANES:ts + SUBLANES, :] = xr_ref[...].astype(F32)

    base = SUBLANES - (CONV_W - 1)
    xc = cw_ref[0:1, :] * xbuf[base:base + ts, :]
    for k in range(1, CONV_W):
        xc = xc + cw_ref[k:k + 1, :] * xbuf[base + k:base + k + ts, :]
    xc = xc + cb_ref[...]

    xcb = xc.astype(BF16)
    ra = jnp.concatenate(
        [jnp.dot(xcb[:, n * LRU_BLOCK_W:(n + 1) * LRU_BLOCK_W], wa_ref[n], preferred_element_type=F32)
         for n in range(LRU_BLOCKS)], axis=-1)
    rx = jnp.concatenate(
        [jnp.dot(xcb[:, n * LRU_BLOCK_W:(n + 1) * LRU_BLOCK_W], wx_ref[n], preferred_element_type=F32)
         for n in range(LRU_BLOCKS)], axis=-1)
    r = jax.nn.sigmoid(ra + ba_ref[...])
    ig = jax.nn.sigmoid(rx + bx_ref[...])
    nlam = -lam_ref[...]
    softplus = jnp.maximum(nlam, 0.0) + jnp.log1p(jnp.exp(-jnp.abs(nlam)))
    log_a = (-LRU_C * r) * softplus
    a = jnp.exp(log_a)
    th = jnp.tanh(log_a)
    mult = jnp.sqrt(-2.0 * th / (1.0 - th))
    b = mult * ig * xc

    row = lax.broadcasted_iota(jnp.int32, (SUBLANES, D_MODEL), 0)
    keeps = [(k, row >= k) for k in (1, 2, 4)]
    hprev = jnp.broadcast_to(hcar[...], (SUBLANES, D_MODEL))
    pieces = []
    for j in range(ts // SUBLANES):
        aj = a[j * SUBLANES:(j + 1) * SUBLANES, :]
        bj = b[j * SUBLANES:(j + 1) * SUBLANES, :]
        for k, keep in keeps:
            a_sh = jnp.where(keep, pltpu.roll(aj, k, axis=0), 1.0)
            b_sh = jnp.where(keep, pltpu.roll(bj, k, axis=0), 0.0)
            bj = aj * b_sh + bj
            aj = aj * a_sh
        hj = bj + aj * hprev
        hprev = jnp.broadcast_to(hj[SUBLANES - 1:SUBLANES, :], (SUBLANES, D_MODEL))
        pieces.append(hj)
    h = jnp.concatenate(pieces, axis=0)
    hcar[...] = h[ts - 1:ts, :]

    hg_ref[...] = (h * jax.nn.gelu(gr_ref[...].astype(F32))).astype(BF16)


def _lru(z, conv_w, conv_b, wa, wx, ba, bx, lam, *, batch, seq, ts=512):
    ns = seq // ts
    T = batch * seq
    full = lambda shape: pl.BlockSpec(shape, lambda b, i: (0,) * len(shape))
    return pl.pallas_call(
        functools.partial(_lru_body, ts=ts),
        grid=(batch, ns),
        in_specs=[
            pl.BlockSpec((ts, D_MODEL), lambda b, i: (b * ns + i, ZC_XR // SUBLANES)),
            pl.BlockSpec((ts, D_MODEL), lambda b, i: (b * ns + i, ZC_GR // SUBLANES)),
            full((CONV_W, D_MODEL)),
            full((1, D_MODEL)),
            full((LRU_BLOCKS, LRU_BLOCK_W, LRU_BLOCK_W)),
            full((LRU_BLOCKS, LRU_BLOCK_W, LRU_BLOCK_W)),
            full((1, D_MODEL)),
            full((1, D_MODEL)),
            full((1, D_MODEL)),
        ],
        out_specs=pl.BlockSpec((ts, D_MODEL), lambda b, i: (b * ns + i, 0)),
        out_shape=jax.ShapeDtypeStruct((T, D_MODEL), BF16),
        scratch_shapes=[pltpu.VMEM((ts + SUBLANES, D_MODEL), F32), pltpu.VMEM((1, D_MODEL), F32)],
        compiler_params=_cparams(("arbitrary", "arbitrary")),
        name="lru",
    )(z, z, conv_w, conv_b, wa, wx, ba, bx, lam)


MERGE_SUBTILES = 1
MOE_PARTS = 1


def _merge_body(hg_ref, o_ref, ml_ref, ma_ref, x_ref, wl_ref, wat_ref, wo_ref, g_ref, wr_ref, br_ref,
                x1_ref, v_ref, ri_ref, rt_ref, cnt_ref, carry_ref, *, tm, steps_per_part):
    @pl.when(pl.program_id(0) % steps_per_part == 0)
    def _():
        carry_ref[...] = jnp.zeros_like(carry_ref)

    ts = tm // MERGE_SUBTILES
    lane = lax.broadcasted_iota(jnp.int32, (ts, LANES), 1)
    rr = lax.broadcasted_iota(jnp.int32, (ts, ts), 0)
    cc = lax.broadcasted_iota(jnp.int32, (ts, ts), 1)
    tri = jnp.where(cc < rr, 1.0, 0.0).astype(BF16)
    ninf = -jnp.inf
    big = jnp.int32(1 << 20)
    carry = carry_ref[...]

    for sub in range(MERGE_SUBTILES):
        rows = slice(sub * ts, (sub + 1) * ts)
        yl = jnp.dot(hg_ref[rows, :], wl_ref[...], preferred_element_type=F32)
        ya = jnp.dot(o_ref[rows, :], wat_ref[...], preferred_element_type=F32)
        merged = (jax.nn.sigmoid(ml_ref[rows, :].astype(F32)) * yl
                  + jax.nn.sigmoid(ma_ref[rows, :].astype(F32)) * ya)
        x1 = x_ref[rows, :] + jnp.dot(merged.astype(BF16), wo_ref[...], preferred_element_type=F32)
        x1_ref[rows, :] = x1
        ms = jnp.mean(x1 * x1, axis=-1, keepdims=True)
        v = x1 * lax.rsqrt(ms + EPS) * g_ref[...]
        _store_packed_rows(v_ref, sub * ts * PACK_CHUNKS, v)

        logits = jnp.dot(v.astype(BF16), wr_ref[...], preferred_element_type=F32) + br_ref[...]

        gl = jnp.where(lane < N_GROUPS, logits, ninf)
        gmax = jnp.max(gl, axis=-1, keepdims=True)
        gsel = jnp.min(jnp.where(gl == gmax, lane, big), axis=-1, keepdims=True)
        pg = 1.0 / jnp.sum(jnp.exp(gl - gmax), axis=-1, keepdims=True)

        lo = R_EXP0 + gsel * EXPERTS_PER_GROUP
        el = jnp.where(lane >= lo, jnp.where(lane < lo + EXPERTS_PER_GROUP, logits, ninf), ninf)
        v1 = jnp.max(el, axis=-1, keepdims=True)
        i1 = jnp.min(jnp.where(el == v1, lane, big), axis=-1, keepdims=True)
        el2 = jnp.where(lane == i1, ninf, el)
        v2 = jnp.max(el2, axis=-1, keepdims=True)
        i2 = jnp.min(jnp.where(el2 == v2, lane, big), axis=-1, keepdims=True)
        e21 = jnp.exp(v2 - v1)
        p1 = 1.0 / (1.0 + e21)
        w0 = pg * p1
        w1 = pg * (e21 * p1)

        hit0 = lane == i1
        hit1 = lane == i2
        onehot = jnp.where(hit0, 1.0, jnp.where(hit1, 1.0, 0.0))
        before = jnp.dot(tri, onehot.astype(BF16), preferred_element_type=F32) + carry
        rank0 = jnp.sum(jnp.where(hit0, before, 0.0), axis=-1, keepdims=True)
        rank1 = jnp.sum(jnp.where(hit1, before, 0.0), axis=-1, keepdims=True)
        carry = carry + jnp.sum(onehot, axis=0, keepdims=True)

        e0 = (i1 - R_EXP0).astype(F32)
        e1 = (i2 - R_EXP0).astype(F32)
        ri = jnp.where(lane == 0, e0,
             jnp.where(lane == 1, e1,
             jnp.where(lane == 2, w0,
             jnp.where(lane == 3, w1,
             jnp.where(lane == 4, rank0,
             jnp.where(lane == 5, rank1, 0.0))))))
        ri_ref[rows, :] = ri
        rt_ref[:, rows] = ri.T[:SUBLANES, :]

    carry_ref[...] = carry
    cnt_ref[0] = carry


def _merge(hg, o, z, x2, wl, wat, wo, g, wr, br, *, tm=512):
    T = x2.shape[0]
    steps_per_part = T // tm // MOE_PARTS
    full = lambda shape: pl.BlockSpec(shape, lambda i: (0,) * len(shape))
    return pl.pallas_call(
        functools.partial(_merge_body, tm=tm, steps_per_part=steps_per_part),
        grid=(T // tm,),
        in_specs=[
            pl.BlockSpec((tm, D_MODEL), lambda i: (i, 0)),
            pl.BlockSpec((tm, D_MODEL), lambda i: (i, 0)),
            pl.BlockSpec((tm, D_MODEL), lambda i: (i, ZC_ML // SUBLANES)),
            pl.BlockSpec((tm, D_MODEL), lambda i: (i, ZC_MA // SUBLANES)),
            pl.BlockSpec((tm, D_MODEL), lambda i: (i, 0)),
            full((D_MODEL, D_MODEL)),
            full((D_MODEL, D_MODEL)),
            full((D_MODEL, D_MODEL)),
            full((1, D_MODEL)),
            full((D_MODEL, LANES)),
            full((1, LANES)),
        ],
        out_specs=[
            pl.BlockSpec((tm, D_MODEL), lambda i: (i, 0)),
            pl.BlockSpec((tm * PACK_CHUNKS, LANES), lambda i: (i, 0)),
            pl.BlockSpec((tm, LANES), lambda i: (i, 0)),
            pl.BlockSpec((SUBLANES, tm), lambda i: (0, i)),
            pl.BlockSpec((1, 1, LANES), lambda i: (i // steps_per_part, 0, 0)),
        ],
        out_shape=[
            jax.ShapeDtypeStruct((T, D_MODEL), F32),
            jax.ShapeDtypeStruct((T * PACK_CHUNKS, LANES), U32),
            jax.ShapeDtypeStruct((T, LANES), F32),
            jax.ShapeDtypeStruct((SUBLANES, T), F32),
            jax.ShapeDtypeStruct((MOE_PARTS, 1, LANES), F32),
        ],
        scratch_shapes=[pltpu.VMEM((1, LANES), F32)],
        compiler_params=_cparams(("arbitrary",)),
        name="merge",
    )(hg, o, z, z, x2, wl, wat, wo, g, wr, br)


def _fill_padding_body(fill_ref, xs_in_ref, xs_ref, zblock, zsem, *, n_blocks):
    del xs_in_ref

    def for_each_padding_piece(fn):
        def per_expert(e, carry):
            start = fill_ref[e]
            n = fill_ref[N_EXPERTS + e]
            size = EXPERT_ROWS // 2
            while size >= 1:
                take = (n & size) != 0

                @pl.when(take)
                def _(start=start, size=size):
                    fn(pltpu.make_async_copy(zblock.at[pl.ds(0, size)], xs_ref.at[pl.ds(start, size)], zsem))

                start = start + jnp.where(take, size, 0)
                size //= 2
            return carry
        lax.fori_loop(0, N_EXPERTS, per_expert, 0)

    def zero_block_copy(blk):
        return pltpu.make_async_copy(zblock, xs_ref.at[pl.ds(blk * EXPERT_ROWS, EXPERT_ROWS)], zsem)

    def for_each_unused_block(fn):
        lax.fori_loop(fill_ref[2 * N_EXPERTS], n_blocks, lambda blk, c: (fn(zero_block_copy(blk)), c)[1], 0)

    zblock[...] = jnp.zeros_like(zblock)
    for_each_padding_piece(lambda cp: cp.start())
    for_each_unused_block(lambda cp: cp.start())
    for_each_padding_piece(lambda cp: cp.wait())
    for_each_unused_block(lambda cp: cp.wait())


def _fill_padding(fill, xs3):
    n_blocks = xs3.shape[0] // EXPERT_ROWS
    grid_spec = pltpu.PrefetchScalarGridSpec(
        num_scalar_prefetch=1,
        grid=(1,),
        in_specs=[pl.BlockSpec(memory_space=pl.ANY)],
        out_specs=pl.BlockSpec(memory_space=pl.ANY),
        scratch_shapes=[pltpu.VMEM((EXPERT_ROWS,) + xs3.shape[1:], xs3.dtype),
                        pltpu.SemaphoreType.DMA(())],
    )
    return pl.pallas_call(
        functools.partial(_fill_padding_body, n_blocks=n_blocks),
        grid_spec=grid_spec,
        out_shape=jax.ShapeDtypeStruct(xs3.shape, xs3.dtype),
        input_output_aliases={1: 0},
        compiler_params=_cparams(("arbitrary",)),
        name="fill_padding",
    )(fill, xs3)


def _expert_body(be_ref, nu_ref, x_ref, w1_ref, w3_ref, w2_ref, y_ref, w13b, w2b, *, rows):
    i = pl.program_id(0)

    @pl.when((i == 0) | (be_ref[i] != be_ref[jnp.maximum(i - 1, 0)]))
    def _():
        w13b[:, :D_EXPERT] = w1_ref[0].astype(BF16)
        w13b[:, D_EXPERT:] = w3_ref[0].astype(BF16)
        w2b[...] = w2_ref[0].astype(BF16)

    @pl.when(i < nu_ref[0])
    def _():
        xb = _load_packed_rows(x_ref, rows, BF16)
        gu = jnp.dot(xb, w13b[...], preferred_element_type=F32)
        hb = (jax.nn.silu(gu[:, :D_EXPERT]) * gu[:, D_EXPERT:]).astype(BF16)
        y = jnp.dot(hb, w2b[...], preferred_element_type=F32)
        _store_packed_rows(y_ref, 0, y)

    @pl.when(i >= nu_ref[0])
    def _():
        y_ref[...] = jnp.zeros_like(y_ref)


def _experts(blk_expert, n_used, xs, w1, w3, w2, *, rows=EXPERT_ROWS):
    n_blocks = xs.shape[0] // (rows * PACK_CHUNKS)
    grid_spec = pltpu.PrefetchScalarGridSpec(
        num_scalar_prefetch=2,
        grid=(n_blocks,),
        in_specs=[
            pl.BlockSpec((rows * PACK_CHUNKS, LANES), lambda i, be, nu: (jnp.minimum(i, nu[0] - 1), 0)),
            pl.BlockSpec((1, D_MODEL, D_EXPERT), lambda i, be, nu: (be[i], 0, 0)),
            pl.BlockSpec((1, D_MODEL, D_EXPERT), lambda i, be, nu: (be[i], 0, 0)),
            pl.BlockSpec((1, D_EXPERT, D_MODEL), lambda i, be, nu: (be[i], 0, 0)),
        ],
        out_specs=pl.BlockSpec((rows * PACK_CHUNKS, LANES), lambda i, be, nu: (i, 0)),
        scratch_shapes=[pltpu.VMEM((D_MODEL, 2 * D_EXPERT), BF16), pltpu.VMEM((D_EXPERT, D_MODEL), BF16)],
    )
    return pl.pallas_call(
        functools.partial(_expert_body, rows=rows),
        grid_spec=grid_spec,
        out_shape=jax.ShapeDtypeStruct(xs.shape, U32),
        compiler_params=_cparams(("arbitrary",)),
        name="experts",
    )(blk_expert, n_used, xs, w1, w3, w2)


SC_CORES = 2
SC_SUBCORES = 16
SC_WORKERS = SC_CORES * SC_SUBCORES
SC_CHUNK = 64


def _sc_worker_base(per_worker):
    return (lax.axis_index("s") * SC_CORES + lax.axis_index("c")) * per_worker


def _sc_scatter_rows(v3, dests, n_rows, tok0):
    per_worker = dests[0].shape[0] // SC_WORKERS
    mesh = plsc.VectorSubcoreMesh(core_axis_name="c", subcore_axis_name="s")
    bufs = range(2)
    slab = v3.shape[1:]

    @functools.partial(
        pl.kernel, mesh=mesh,
        out_type=jax.ShapeDtypeStruct((n_rows,) + slab, v3.dtype),
        scratch_types=[pltpu.VMEM((2 * TOP_K, SC_CHUNK), jnp.int32),
                       pltpu.VMEM((2, SC_CHUNK) + slab, v3.dtype),
                       pltpu.SemaphoreType.DMA((2,)),
                       pltpu.SemaphoreType.DMA((2,))],
        name="sc_scatter")
    def scatter(v_hbm, d0_hbm, d1_hbm, xs_hbm, idx, rows, lsem, ssem):
        base = _sc_worker_base(per_worker)

        @pl.loop(0, per_worker, step=2 * SC_CHUNK)
        def _(c):
            loads = []
            for b in bufs:
                off = base + c + b * SC_CHUNK
                loads.append(pltpu.async_copy(v_hbm.at[pl.ds(tok0 + off, SC_CHUNK)], rows.at[b], lsem.at[b]))
                for k, d_hbm in enumerate((d0_hbm, d1_hbm)):
                    pltpu.sync_copy(d_hbm.at[pl.ds(off, SC_CHUNK)], idx.at[b * TOP_K + k])
            stores = []
            for b in bufs:
                loads[b].wait()
                for k in range(TOP_K):
                    stores.append(pltpu.async_copy(rows.at[b], xs_hbm.at[idx.at[b * TOP_K + k]], ssem.at[b]))
            for st in stores:
                st.wait()

    return scatter(v3, *dests)


def _sc_gather_rows(y3, dests):
    T = dests[0].shape[0]
    per_worker = T // SC_WORKERS
    mesh = plsc.VectorSubcoreMesh(core_axis_name="c", subcore_axis_name="s")
    bufs = range(2)
    slab = y3.shape[1:]

    @functools.partial(
        pl.kernel, mesh=mesh,
        out_type=jax.ShapeDtypeStruct((TOP_K, T) + slab, y3.dtype),
        scratch_types=[pltpu.VMEM((2, SC_CHUNK), jnp.int32),
                       pltpu.VMEM((2, SC_CHUNK) + slab, y3.dtype),
                       pltpu.SemaphoreType.DMA((2,)),
                       pltpu.SemaphoreType.DMA((2,))],
        name="sc_gather")
    def gather(y_hbm, d0_hbm, d1_hbm, out_hbm, idx, rows, gsem, wsem):
        base = _sc_worker_base(per_worker)
        for k, d_hbm in enumerate((d0_hbm, d1_hbm)):
            @pl.loop(0, per_worker, step=2 * SC_CHUNK)
            def _(c):
                gathers = []
                for b in bufs:
                    off = base + c + b * SC_CHUNK
                    pltpu.sync_copy(d_hbm.at[pl.ds(off, SC_CHUNK)], idx.at[b])
                    gathers.append(pltpu.async_copy(y_hbm.at[idx.at[b]], rows.at[b], gsem.at[b]))
                writes = []
                for b in bufs:
                    off = base + c + b * SC_CHUNK
                    gathers[b].wait()
                    writes.append(pltpu.async_copy(rows.at[b], out_hbm.at[k, pl.ds(off, SC_CHUNK)], wsem.at[b]))
                for w in writes:
                    w.wait()

    return gather(y3, *dests)


def _combine_body(yk_ref, x1_ref, ri_ref, g_ref, *rest, tc, final_norm):
    out_ref = rest[-1]

    ri = ri_ref[...]
    w0 = ri[:, 2:3]
    w1 = ri[:, 3:4]
    x2 = x1_ref[...] + (w0 * _load_packed_rows(yk_ref.at[0], tc, F32) + w1 * _load_packed_rows(yk_ref.at[1], tc, F32))
    if final_norm:
        ms = jnp.mean(x2 * x2, axis=-1, keepdims=True)
        x2 = x2 * lax.rsqrt(ms + EPS) * g_ref[...]
    out_ref[...] = x2


def _combine(yk, x1, ri, g, out_prev, *, tok0, final_norm, tc=256):
    T = x1.shape[0]
    steps = yk.shape[1] // (tc * PACK_CHUNKS)
    blk0 = tok0 // tc
    in_specs = [
        pl.BlockSpec((TOP_K, tc * PACK_CHUNKS, LANES), lambda i: (0, i, 0)),
        pl.BlockSpec((tc, D_MODEL), lambda i: (i + blk0, 0)),
        pl.BlockSpec((tc, LANES), lambda i: (i + blk0, 0)),
        pl.BlockSpec((1, D_MODEL), lambda i: (0, 0)),
    ]
    args = [yk, x1, ri, g]
    aliases = {}
    if out_prev is not None:
        in_specs.append(pl.BlockSpec(memory_space=pl.ANY))
        args.append(out_prev)
        aliases = {len(args) - 1: 0}
    return pl.pallas_call(
        functools.partial(_combine_body, tc=tc, final_norm=final_norm),
        grid=(steps,),
        in_specs=in_specs,
        out_specs=pl.BlockSpec((tc, D_MODEL), lambda i: (i + blk0, 0)),
        out_shape=jax.ShapeDtypeStruct((T, D_MODEL), F32),
        input_output_aliases=aliases,
        compiler_params=_cparams(("arbitrary",)),
        name="combine",
    )(*args)


def _pad_lanes(a, width=LANES):
    return jnp.pad(a, ((0, 0), (0, width - a.shape[-1])))


def _layer(x2, batch, seq, g_mix, w_in, conv_w, conv_b, w_rg_a, b_rg_a, w_rg_x, b_rg_x, lam, b_forget,
           w_lru_out, w_attn_out, w_out, g_ffn, w_rgrp, b_rgrp, w_rexp, b_rexp, w1, w3, w2, g_out, final_norm):
    T = batch * seq
    row = lambda a: a.reshape(1, -1).astype(F32)

    fl0 = 5 * D_MODEL
    w_main = jnp.concatenate([w_in[:, :fl0], w_in[:, fl0 + N_HEADS:]], axis=1).astype(BF16)
    w_fl = _pad_lanes(w_in[:, fl0:fl0 + N_HEADS]).astype(BF16)
    z, fl = _inproj(x2, row(g_mix), w_main, w_fl)

    caug = _forget(fl, _pad_lanes(row(b_forget)), batch=batch, seq=seq)
    o = _attention(z, caug, batch=batch, seq=seq)

    hg = _lru(z, conv_w.astype(F32), row(conv_b), w_rg_a.astype(BF16), w_rg_x.astype(BF16),
              row(b_rg_a), row(b_rg_x), row(lam), batch=batch, seq=seq)

    wr = _pad_lanes(jnp.concatenate([w_rgrp, w_rexp], axis=1)).astype(BF16)
    br = _pad_lanes(jnp.concatenate([row(b_rgrp), row(b_rexp)], axis=1))
    x1, v, ri, rt, cnt = _merge(hg, o, z, x2, w_lru_out.astype(BF16), w_attn_out.astype(BF16),
                            w_out.astype(BF16), row(g_ffn), wr, br)

    v3 = v.reshape(T, PACK_CHUNKS, LANES)
    g_out = row(g_out)
    Tp = T // MOE_PARTS
    n_blocks = (Tp * TOP_K + N_EXPERTS * (EXPERT_ROWS - 1) + EXPERT_ROWS - 1) // EXPERT_ROWS
    blk_start = jnp.arange(n_blocks, dtype=jnp.int32) * EXPERT_ROWS
    out = None
    for part in range(MOE_PARTS):
        tok0 = part * Tp
        e = rt[0:TOP_K, tok0:tok0 + Tp].astype(jnp.int32)
        rank = rt[4:4 + TOP_K, tok0:tok0 + Tp].astype(jnp.int32)
        counts = cnt[part, 0, R_EXP0:R_EXP0 + N_EXPERTS].astype(jnp.int32)
        padded = (counts + EXPERT_ROWS - 1) // EXPERT_ROWS * EXPERT_ROWS
        pad_end = jnp.cumsum(padded)
        pad_start = pad_end - padded
        seg_start = jnp.zeros_like(e)
        for j in range(N_EXPERTS):
            seg_start = jnp.where(e == j, pad_start[j], seg_start)
        dest2 = (seg_start + rank).astype(jnp.int32)
        dests = [dest2[k] for k in range(TOP_K)]
        blk_expert = jnp.minimum(
            jnp.sum((pad_end[None, :] <= blk_start[:, None]).astype(jnp.int32), axis=1), N_EXPERTS - 1)
        n_used = (pad_end[-1:] // EXPERT_ROWS).astype(jnp.int32)
        fill = jnp.concatenate([pad_start + counts, padded - counts, n_used]).astype(jnp.int32)

        xs = _sc_scatter_rows(v3, dests, n_blocks * EXPERT_ROWS, tok0)
        xs = _fill_padding(fill, xs)
        y = _experts(blk_expert, n_used, xs.reshape(-1, LANES), w1, w3, w2)
        yk = _sc_gather_rows(y.reshape(-1, PACK_CHUNKS, LANES), dests)
        out = _combine(yk.reshape(TOP_K, Tp * PACK_CHUNKS, LANES), x1, ri, g_out, out,
                       tok0=tok0, final_norm=final_norm)
    return out


def kernel(x, g_mix, w_in, conv_w, conv_b, w_rg_a, b_rg_a, w_rg_x, b_rg_x, lru_lambda, b_forget, w_lru_out, w_attn_out, w_out, g_ffn, w_route_group, b_route_group, w_route_expert, b_route_expert, w_exp_gate, w_exp_up, w_exp_down, g_final):
    batch, seq, _ = x.shape
    depth = g_mix.shape[0]
    x2 = x.reshape(batch * seq, D_MODEL)
    for l in range(depth):
        x2 = _layer(
            x2, batch, seq, g_mix[l], w_in[l], conv_w[l], conv_b[l], w_rg_a[l], b_rg_a[l], w_rg_x[l],
            b_rg_x[l], lru_lambda[l], b_forget[l], w_lru_out[l], w_attn_out[l], w_out[l], g_ffn[l],
            w_route_group[l], b_route_group[l], w_route_expert[l], b_route_expert[l],
            w_exp_gate[l], w_exp_up[l], w_exp_down[l], g_final, l == depth - 1)
    return x2.reshape(batch, seq, D_MODEL)
```

```python
import functools

import jax
import jax.numpy as jnp
from jax import lax
from jax.experimental import pallas as pl
from jax.experimental.pallas import tpu as pltpu
from jax.experimental.pallas import tpu_sc as plsc

F32 = jnp.float32
BF16 = jnp.bfloat16

D_MODEL = 1024
LRU_BLOCK_W = 256
LRU_BLOCKS = D_MODEL // LRU_BLOCK_W
CONV_W = 4
LRU_C = 8.0
N_HEADS = 8
HEAD_DIM = D_MODEL // N_HEADS
N_GROUPS = 4
EXPERTS_PER_GROUP = 8
N_EXPERTS = N_GROUPS * EXPERTS_PER_GROUP
TOP_K = 2
D_EXPERT = D_MODEL // 2
EPS = 1e-6

LANES = 128
SUBLANES = 8
PACK_CHUNKS = D_MODEL // LANES // 2
U32 = jnp.uint32
VMEM_LIMIT = 48 * 1024 * 1024

ZC_XR, ZC_GR, ZC_Q, ZC_K, ZC_V, ZC_ML, ZC_MA = 0, 8, 16, 24, 32, 40, 48
Z_WIDTH = 7 * D_MODEL

R_EXP0 = N_GROUPS

EXPERT_ROWS = 512


def _cparams(sem):
    return pltpu.CompilerParams(dimension_semantics=sem, vmem_limit_bytes=VMEM_LIMIT)


def _store_packed_rows(ref, first_row, x):
    n = x.shape[0]
    bits = lax.bitcast_convert_type(x.astype(BF16).astype(F32), U32)
    half = PACK_CHUNKS * LANES
    for s in range(PACK_CHUNKS):
        word = (bits[:, s * LANES:(s + 1) * LANES] >> 16) | bits[:, half + s * LANES:half + (s + 1) * LANES]
        ref[pl.ds(first_row + s, n, stride=PACK_CHUNKS), :] = word


def _load_packed_rows(ref, n, dtype):
    words = [ref[pl.ds(s, n, stride=PACK_CHUNKS), :] for s in range(PACK_CHUNKS)]
    low = [lax.bitcast_convert_type(w << 16, F32).astype(dtype) for w in words]
    high = [lax.bitcast_convert_type(w & jnp.uint32(0xFFFF0000), F32).astype(dtype) for w in words]
    return jnp.concatenate(low + high, axis=-1)


LOG2E = 1.4426950408889634
Q_PRESCALE = HEAD_DIM ** -0.5 * LOG2E


def _inproj_body(x_ref, g_ref, w_ref, wfl_ref, z_ref, fl_ref, u_ref, *, q_block):
    j = pl.program_id(1)

    @pl.when(j == 0)
    def _():
        x = x_ref[...]
        ms = jnp.mean(x * x, axis=-1, keepdims=True)
        u = (x * lax.rsqrt(ms + EPS) * g_ref[...]).astype(BF16)
        u_ref[...] = u
        fl_ref[...] = jnp.dot(u, wfl_ref[...], preferred_element_type=F32)

    @pl.when(j == q_block)
    def _():
        acc = jnp.dot(u_ref[...], w_ref[...], preferred_element_type=F32)
        z_ref[...] = (acc * Q_PRESCALE).astype(BF16)

    @pl.when(j != q_block)
    def _():
        z_ref[...] = jnp.dot(u_ref[...], w_ref[...], preferred_element_type=F32).astype(BF16)


def _inproj(x2, g, w_main, w_fl, *, tm=2048, tn=D_MODEL):
    T = x2.shape[0]
    return pl.pallas_call(
        functools.partial(_inproj_body, q_block=ZC_Q * LANES // tn),
        grid=(T // tm, Z_WIDTH // tn),
        in_specs=[
            pl.BlockSpec((tm, D_MODEL), lambda i, j: (i, 0)),
            pl.BlockSpec((1, D_MODEL), lambda i, j: (0, 0)),
            pl.BlockSpec((D_MODEL, tn), lambda i, j: (0, j)),
            pl.BlockSpec((D_MODEL, LANES), lambda i, j: (0, 0)),
        ],
        out_specs=[
            pl.BlockSpec((tm, tn), lambda i, j: (i, j)),
            pl.BlockSpec((tm, LANES), lambda i, j: (i, 0)),
        ],
        out_shape=[
            jax.ShapeDtypeStruct((T, Z_WIDTH), BF16),
            jax.ShapeDtypeStruct((T, LANES), F32),
        ],
        scratch_shapes=[pltpu.VMEM((tm, D_MODEL), BF16)],
        compiler_params=_cparams(("arbitrary", "arbitrary")),
        name="inproj",
    )(x2, g, w_main, w_fl)


C_TERMS = 3


def _log_sigmoid(z):
    return jnp.minimum(z, 0.0) - jnp.log1p(jnp.exp(-jnp.abs(z)))


def _forget_body(fl_ref, b_ref, c_ref, *, seq):
    lf = _log_sigmoid(fl_ref[...] + b_ref[...])
    row = lax.broadcasted_iota(jnp.int32, lf.shape, 0)
    lane = lax.broadcasted_iota(jnp.int32, lf.shape, 1)
    c = lf
    k = 1
    while k < seq:
        c = c + jnp.where(row >= k, pltpu.roll(c, k, axis=0), 0.0)
        k *= 2
    rem = c * LOG2E
    out = jnp.zeros(lf.shape, F32)
    for n in range(C_TERMS):
        t = rem.astype(BF16).astype(F32)
        rem = rem - t
        shifted = t if n == 0 else pltpu.roll(t, n * N_HEADS, axis=1)
        out = jnp.where((lane >= n * N_HEADS) & (lane < (n + 1) * N_HEADS), shifted, out)
    c_ref[...] = out.astype(BF16)


def _forget(fl, b_pad, *, batch, seq):
    return pl.pallas_call(
        functools.partial(_forget_body, seq=seq),
        grid=(batch,),
        in_specs=[
            pl.BlockSpec((seq, LANES), lambda b: (b, 0)),
            pl.BlockSpec((1, LANES), lambda b: (0, 0)),
        ],
        out_specs=pl.BlockSpec((seq, LANES), lambda b: (b, 0)),
        out_shape=jax.ShapeDtypeStruct((batch * seq, LANES), BF16),
        compiler_params=_cparams(("arbitrary",)),
        name="forget",
    )(fl, b_pad)


ATTN_HEADS_PER_STEP = 4


def _attn_body(q_ref, k_ref, v_ref, kc_ref, o_ref, *, tq):
    i = pl.program_id(2)
    lane = lax.broadcasted_iota(jnp.int32, (tq, LANES), 1)
    heads = range(ATTN_HEADS_PER_STEP)
    cols =[slice(g * HEAD_DIM, (g + 1) * HEAD_DIM) for g in heads]
    qs = []
    for g in heads:
        h = pl.program_id(1) * ATTN_HEADS_PER_STEP + g
        mine = (lane < C_TERMS * N_HEADS) & ((lane & (N_HEADS - 1)) == h)
        qc = jnp.where(mine, -1.0, 0.0).astype(BF16)
        qs.append(jnp.concatenate([q_ref[:, cols[g]], qc], axis=1))

    def step(j, carry, masked):
        off = pl.multiple_of(j * tq, tq)
        out = []
        for g in heads:
            m, l, acc = carry[g]
            kj = jnp.concatenate([k_ref[pl.ds(off, tq), cols[g]], kc_ref[pl.ds(off, tq), :]], axis=1)
            s = lax.dot_general(qs[g], kj, (((1,), (1,)), ((), ())), preferred_element_type=F32)
            if masked:
                r = lax.broadcasted_iota(jnp.int32, s.shape, 0)
                cidx = lax.broadcasted_iota(jnp.int32, s.shape, 1)
                s = jnp.where(cidx <= r, s, -jnp.inf)
            m_new = jnp.maximum(m, jnp.max(s, axis=-1, keepdims=True))
            p = jnp.exp2(s - m_new)
            alpha = jnp.exp2(m - m_new)
            l = alpha * l + jnp.sum(p, axis=-1, keepdims=True)
            acc = alpha * acc + jnp.dot(p.astype(BF16), v_ref[pl.ds(off, tq), cols[g]],
                                        preferred_element_type=F32)
            out.append((m_new, l, acc))
        return tuple(out)

    init = tuple((jnp.full((tq, 1), -jnp.inf, F32), jnp.zeros((tq, 1), F32), jnp.zeros((tq, HEAD_DIM), F32))
                 for _ in heads)
    carry = lax.fori_loop(0, i, lambda j, c: step(j, c, False), init)
    carry = step(i, carry, True)
    for g in heads:
        _, l, acc = carry[g]
        o_ref[:, cols[g]] = (acc / l).astype(BF16)


def _attention(z, caug, *, batch, seq, tq=512):
    nq = seq // tq
    T = batch * seq
    G = ATTN_HEADS_PER_STEP
    W = G * HEAD_DIM
    return pl.pallas_call(
        functools.partial(_attn_body, tq=tq),
        grid=(batch, N_HEADS // G, nq),
        in_specs=[
            pl.BlockSpec((tq, W), lambda b, h, i: (b * nq + i, ZC_Q // G + h)),
            pl.BlockSpec((seq, W), lambda b, h, i: (b, ZC_K // G + h)),
            pl.BlockSpec((seq, W), lambda b, h, i: (b, ZC_V // G + h)),
            pl.BlockSpec((seq, LANES), lambda b, h, i: (b, 0)),
        ],
        out_specs=pl.BlockSpec((tq, W), lambda b, h, i: (b * nq + i, h)),
        out_shape=jax.ShapeDtypeStruct((T, D_MODEL), BF16),
        compiler_params=_cparams(("arbitrary", "arbitrary", "arbitrary")),
        name="attn",
    )(z, z, z, caug)


def _lru_body(xr_ref, gr_ref, cw_ref, cb_ref, wa_ref, wx_ref, ba_ref, bx_ref, lam_ref,
              hg_ref, xbuf, hcar, *, ts):
    i = pl.program_id(1)

    @pl.when(i == 0)
    def _():
        xbuf[0:SUBLANES, :] = jnp.zeros((SUBLANES, D_MODEL), F32)
        hcar[...] = jnp.zeros_like(hcar)

    @pl.when(i > 0)
    def _():
        xbuf[0:SUBLANES, :] = xbuf[ts:ts + SUBLANES, :]

    xbuf[SUBLANES:ts + SUBLANES, :] = xr_ref[...].astype(F32)

    base = SUBLANES - (CONV_W - 1)
    xc = cw_ref[0:1, :] * xbuf[base:base + ts, :]
    for k in range(1, CONV_W):
        xc = xc + cw_ref[k:k + 1, :] * xbuf[base + k:base + k + ts, :]
    xc = xc + cb_ref[...]

    xcb = xc.astype(BF16)
    ra = jnp.concatenate(
        [jnp.dot(xcb[:, n * LRU_BLOCK_W:(n + 1) * LRU_BLOCK_W], wa_ref[n], preferred_element_type=F32)
         for n in range(LRU_BLOCKS)], axis=-1)
    rx = jnp.concatenate(
        [jnp.dot(xcb[:, n * LRU_BLOCK_W:(n + 1) * LRU_BLOCK_W], wx_ref[n], preferred_element_type=F32)
         for n in range(LRU_BLOCKS)], axis=-1)
    r = jax.nn.sigmoid(ra + ba_ref[...])
    ig = jax.nn.sigmoid(rx + bx_ref[...])
    nlam = -lam_ref[...]
    softplus = jnp.maximum(nlam, 0.0) + jnp.log1p(jnp.exp(-jnp.abs(nlam)))
    log_a = (-LRU_C * r) * softplus
    a = jnp.exp(log_a)
    th = jnp.tanh(log_a)
    mult = jnp.sqrt(-2.0 * th / (1.0 - th))
    b = mult * ig * xc

    row = lax.broadcasted_iota(jnp.int32, (SUBLANES, D_MODEL), 0)
    keeps = [(k, row >= k) for k in (1, 2, 4)]
    hprev = jnp.broadcast_to(hcar[...], (SUBLANES, D_MODEL))
    pieces = []
    for j in range(ts // SUBLANES):
        aj = a[j * SUBLANES:(j + 1) * SUBLANES, :]
        bj = b[j * SUBLANES:(j + 1) * SUBLANES, :]
        for k, keep in keeps:
            a_sh = jnp.where(keep, pltpu.roll(aj, k, axis=0), 1.0)
            b_sh = jnp.where(keep, pltpu.roll(bj, k, axis=0), 0.0)
            bj = aj * b_sh + bj
            aj = aj * a_sh
        hj = bj + aj * hprev
        hprev = jnp.broadcast_to(hj[SUBLANES - 1:SUBLANES, :], (SUBLANES, D_MODEL))
        pieces.append(hj)
    h = jnp.concatenate(pieces, axis=0)
    hcar[...] = h[ts - 1:ts, :]

    hg_ref[...] = (h * jax.nn.gelu(gr_ref[...].astype(F32))).astype(BF16)


def _lru(z, conv_w, conv_b, wa, wx, ba, bx, lam, *, batch, seq, ts=512):
    ns = seq // ts
    T = batch * seq
    full = lambda shape: pl.BlockSpec(shape, lambda b, i: (0,) * len(shape))
    return pl.pallas_call(
        functools.partial(_lru_body, ts=ts),
        grid=(batch, ns),
        in_specs=[
            pl.BlockSpec((ts, D_MODEL), lambda b, i: (b * ns + i, ZC_XR // SUBLANES)),
            pl.BlockSpec((ts, D_MODEL), lambda b, i: (b * ns + i, ZC_GR // SUBLANES)),
            full((CONV_W, D_MODEL)),
            full((1, D_MODEL)),
            full((LRU_BLOCKS, LRU_BLOCK_W, LRU_BLOCK_W)),
            full((LRU_BLOCKS, LRU_BLOCK_W, LRU_BLOCK_W)),
            full((1, D_MODEL)),
            full((1, D_MODEL)),
            full((1, D_MODEL)),
        ],
        out_specs=pl.BlockSpec((ts, D_MODEL), lambda b, i: (b * ns + i, 0)),
        out_shape=jax.ShapeDtypeStruct((T, D_MODEL), BF16),
        scratch_shapes=[pltpu.VMEM((ts + SUBLANES, D_MODEL), F32), pltpu.VMEM((1, D_MODEL), F32)],
        compiler_params=_cparams(("arbitrary", "arbitrary")),
        name="lru",
    )(z, z, conv_w, conv_b, wa, wx, ba, bx, lam)


MERGE_SUBTILES = 1
MOE_PARTS = 1


def _merge_body(hg_ref, o_ref, ml_ref, ma_ref, x_ref, wl_ref, wat_ref, wo_ref, g_ref, wr_ref, br_ref,
                x1_ref, v_ref, ri_ref, rt_ref, cnt_ref, carry_ref, *, tm, steps_per_part):
    @pl.when(pl.program_id(0) % steps_per_part == 0)
    def _():
        carry_ref[...] = jnp.zeros_like(carry_ref)

    ts = tm // MERGE_SUBTILES
    lane = lax.broadcasted_iota(jnp.int32, (ts, LANES), 1)
    rr = lax.broadcasted_iota(jnp.int32, (ts, ts), 0)
    cc = lax.broadcasted_iota(jnp.int32, (ts, ts), 1)
    tri = jnp.where(cc < rr, 1.0, 0.0).astype(BF16)
    ninf = -jnp.inf
    big = jnp.int32(1 << 20)
    carry = carry_ref[...]

    for sub in range(MERGE_SUBTILES):
        rows = slice(sub * ts, (sub + 1) * ts)
        yl = jnp.dot(hg_ref[rows, :], wl_ref[...], preferred_element_type=F32)
        ya = jnp.dot(o_ref[rows, :], wat_ref[...], preferred_element_type=F32)
        merged = (jax.nn.sigmoid(ml_ref[rows, :].astype(F32)) * yl
                  + jax.nn.sigmoid(ma_ref[rows, :].astype(F32)) * ya)
        x1 = x_ref[rows, :] + jnp.dot(merged.astype(BF16), wo_ref[...], preferred_element_type=F32)
        x1_ref[rows, :] = x1
        ms = jnp.mean(x1 * x1, axis=-1, keepdims=True)
        v = x1 * lax.rsqrt(ms + EPS) * g_ref[...]
        _store_packed_rows(v_ref, sub * ts * PACK_CHUNKS, v)

        logits = jnp.dot(v.astype(BF16), wr_ref[...], preferred_element_type=F32) + br_ref[...]

        gl = jnp.where(lane < N_GROUPS, logits, ninf)
        gmax = jnp.max(gl, axis=-1, keepdims=True)
        gsel = jnp.min(jnp.where(gl == gmax, lane, big), axis=-1, keepdims=True)
        pg = 1.0 / jnp.sum(jnp.exp(gl - gmax), axis=-1, keepdims=True)

        lo = R_EXP0 + gsel * EXPERTS_PER_GROUP
        el = jnp.where(lane >= lo, jnp.where(lane < lo + EXPERTS_PER_GROUP, logits, ninf), ninf)
        v1 = jnp.max(el, axis=-1, keepdims=True)
        i1 = jnp.min(jnp.where(el == v1, lane, big), axis=-1, keepdims=True)
        el2 = jnp.where(lane == i1, ninf, el)
        v2 = jnp.max(el2, axis=-1, keepdims=True)
        i2 = jnp.min(jnp.where(el2 == v2, lane, big), axis=-1, keepdims=True)
        e21 = jnp.exp(v2 - v1)
        p1 = 1.0 / (1.0 + e21)
        w0 = pg * p1
        w1 = pg * (e21 * p1)

        hit0 = lane == i1
        hit1 = lane == i2
        onehot = jnp.where(hit0, 1.0, jnp.where(hit1, 1.0, 0.0))
        before = jnp.dot(tri, onehot.astype(BF16), preferred_element_type=F32) + carry
        rank0 = jnp.sum(jnp.where(hit0, before, 0.0), axis=-1, keepdims=True)
        rank1 = jnp.sum(jnp.where(hit1, before, 0.0), axis=-1, keepdims=True)
        carry = carry + jnp.sum(onehot, axis=0, keepdims=True)

        e0 = (i1 - R_EXP0).astype(F32)
        e1 = (i2 - R_EXP0).astype(F32)
        ri = jnp.where(lane == 0, e0,
             jnp.where(lane == 1, e1,
             jnp.where(lane == 2, w0,
             jnp.where(lane == 3, w1,
             jnp.where(lane == 4, rank0,
             jnp.where(lane == 5, rank1, 0.0))))))
        ri_ref[rows, :] = ri
        rt_ref[:, rows] = ri.T[:SUBLANES, :]

    carry_ref[...] = carry
    cnt_ref[0] = carry


def _merge(hg, o, z, x2, wl, wat, wo, g, wr, br, *, tm=512):
    T = x2.shape[0]
    steps_per_part = T // tm // MOE_PARTS
    full = lambda shape: pl.BlockSpec(shape, lambda i: (0,) * len(shape))
    return pl.pallas_call(
        functools.partial(_merge_body, tm=tm, steps_per_part=steps_per_part),
        grid=(T // tm,),
        in_specs=[
            pl.BlockSpec((tm, D_MODEL), lambda i: (i, 0)),
            pl.BlockSpec((tm, D_MODEL), lambda i: (i, 0)),
            pl.BlockSpec((tm, D_MODEL), lambda i: (i, ZC_ML // SUBLANES)),
            pl.BlockSpec((tm, D_MODEL), lambda i: (i, ZC_MA // SUBLANES)),
            pl.BlockSpec((tm, D_MODEL), lambda i: (i, 0)),
            full((D_MODEL, D_MODEL)),
            full((D_MODEL, D_MODEL)),
            full((D_MODEL, D_MODEL)),
            full((1, D_MODEL)),
            full((D_MODEL, LANES)),
            full((1, LANES)),
        ],
        out_specs=[
            pl.BlockSpec((tm, D_MODEL), lambda i: (i, 0)),
            pl.BlockSpec((tm * PACK_CHUNKS, LANES), lambda i: (i, 0)),
            pl.BlockSpec((tm, LANES), lambda i: (i, 0)),
            pl.BlockSpec((SUBLANES, tm), lambda i: (0, i)),
            pl.BlockSpec((1, 1, LANES), lambda i: (i // steps_per_part, 0, 0)),
        ],
        out_shape=[
            jax.ShapeDtypeStruct((T, D_MODEL), F32),
            jax.ShapeDtypeStruct((T * PACK_CHUNKS, LANES), U32),
            jax.ShapeDtypeStruct((T, LANES), F32),
            jax.ShapeDtypeStruct((SUBLANES, T), F32),
            jax.ShapeDtypeStruct((MOE_PARTS, 1, LANES), F32),
        ],
        scratch_shapes=[pltpu.VMEM((1, LANES), F32)],
        compiler_params=_cparams(("arbitrary",)),
        name="merge",
    )(hg, o, z, z, x2, wl, wat, wo, g, wr, br)


def _fill_padding_body(fill_ref, xs_in_ref, xs_ref, zblock, zsem, *, n_blocks):
    del xs_in_ref

    def for_each_padding_piece(fn):
        def per_expert(e, carry):
            start = fill_ref[e]
            n = fill_ref[N_EXPERTS + e]
            size = EXPERT_ROWS // 2
            while size >= 1:
                take = (n & size) != 0

                @pl.when(take)
                def _(start=start, size=size):
                    fn(pltpu.make_async_copy(zblock.at[pl.ds(0, size)], xs_ref.at[pl.ds(start, size)], zsem))

                start = start + jnp.where(take, size, 0)
                size //= 2
            return carry
        lax.fori_loop(0, N_EXPERTS, per_expert, 0)

    def zero_block_copy(blk):
        return pltpu.make_async_copy(zblock, xs_ref.at[pl.ds(blk * EXPERT_ROWS, EXPERT_ROWS)], zsem)

    def for_each_unused_block(fn):
        lax.fori_loop(fill_ref[2 * N_EXPERTS], n_blocks, lambda blk, c: (fn(zero_block_copy(blk)), c)[1], 0)

    zblock[...] = jnp.zeros_like(zblock)
    for_each_padding_piece(lambda cp: cp.start())
    for_each_unused_block(lambda cp: cp.start())
    for_each_padding_piece(lambda cp: cp.wait())
    for_each_unused_block(lambda cp: cp.wait())


def _fill_padding(fill, xs3):
    n_blocks = xs3.shape[0] // EXPERT_ROWS
    grid_spec = pltpu.PrefetchScalarGridSpec(
        num_scalar_prefetch=1,
        grid=(1,),
        in_specs=[pl.BlockSpec(memory_space=pl.ANY)],
        out_specs=pl.BlockSpec(memory_space=pl.ANY),
        scratch_shapes=[pltpu.VMEM((EXPERT_ROWS,) + xs3.shape[1:], xs3.dtype),
                        pltpu.SemaphoreType.DMA(())],
    )
    return pl.pallas_call(
        functools.partial(_fill_padding_body, n_blocks=n_blocks),
        grid_spec=grid_spec,
        out_shape=jax.ShapeDtypeStruct(xs3.shape, xs3.dtype),
        input_output_aliases={1: 0},
        compiler_params=_cparams(("arbitrary",)),
        name="fill_padding",
    )(fill, xs3)


def _expert_body(be_ref, nxt_ref, last_ref, par_ref, nu_ref, x_ref, f1_ref, f3_ref, f2_ref,
                 n1_ref, n3_ref, n2_ref, y_ref, w13b, w2b, *, rows):
    del be_ref, nxt_ref
    i = pl.program_id(0)
    slot = par_ref[i]

    def cast(w1_ref, w3_ref, w2_ref, dst):
        w13b[dst, :, :D_EXPERT] = w1_ref[0].astype(BF16)
        w13b[dst, :, D_EXPERT:] = w3_ref[0].astype(BF16)
        w2b[dst] = w2_ref[0].astype(BF16)

    @pl.when(i == 0)
    def _():
        cast(f1_ref, f3_ref, f2_ref, 0)

    @pl.when(i < nu_ref[0])
    def _():
        xb = _load_packed_rows(x_ref, rows, BF16)
        gu = jnp.dot(xb, w13b[slot], preferred_element_type=F32)
        hb = (jax.nn.silu(gu[:, :D_EXPERT]) * gu[:, D_EXPERT:]).astype(BF16)
        y = jnp.dot(hb, w2b[slot], preferred_element_type=F32)
        _store_packed_rows(y_ref, 0, y)

    @pl.when(i >= nu_ref[0])
    def _():
        y_ref[...] = jnp.zeros_like(y_ref)

    @pl.when(last_ref[i] == 1)
    def _():
        cast(n1_ref, n3_ref, n2_ref, 1 - slot)


def _experts(blk_expert, n_used, xs, w1, w3, w2, *, rows=EXPERT_ROWS):
    n_blocks = xs.shape[0] // (rows * PACK_CHUNKS)
    differs = blk_expert[1:] != blk_expert[:-1]
    last = jnp.concatenate([differs, jnp.zeros((1,), bool)]).astype(jnp.int32)
    parity = jnp.cumsum(jnp.concatenate([jnp.zeros((1,), jnp.int32), differs.astype(jnp.int32)])) % 2
    later = jnp.where(blk_expert[None, :] > blk_expert[:, None], blk_expert[None, :], N_EXPERTS)
    nxt = jnp.min(later, axis=1)
    nxt = jnp.where(nxt == N_EXPERTS, blk_expert, nxt).astype(jnp.int32)

    first = lambda shape: pl.BlockSpec(shape, lambda i, be, nx, la, pa, nu: (be[0], 0, 0),
                                       pipeline_mode=pl.Buffered(1))
    following = lambda shape: pl.BlockSpec(shape, lambda i, be, nx, la, pa, nu: (nx[i], 0, 0))
    grid_spec = pltpu.PrefetchScalarGridSpec(
        num_scalar_prefetch=5,
        grid=(n_blocks,),
        in_specs=[
            pl.BlockSpec((rows * PACK_CHUNKS, LANES),
                         lambda i, be, nx, la, pa, nu: (jnp.minimum(i, nu[0] - 1), 0)),
            first((1, D_MODEL, D_EXPERT)), first((1, D_MODEL, D_EXPERT)), first((1, D_EXPERT, D_MODEL)),
            following((1, D_MODEL, D_EXPERT)), following((1, D_MODEL, D_EXPERT)),
            following((1, D_EXPERT, D_MODEL)),
        ],
        out_specs=pl.BlockSpec((rows * PACK_CHUNKS, LANES), lambda i, be, nx, la, pa, nu: (i, 0)),
        scratch_shapes=[pltpu.VMEM((2, D_MODEL, 2 * D_EXPERT), BF16), pltpu.VMEM((2, D_EXPERT, D_MODEL), BF16)],
    )
    return pl.pallas_call(
        functools.partial(_expert_body, rows=rows),
        grid_spec=grid_spec,
        out_shape=jax.ShapeDtypeStruct(xs.shape, U32),
        compiler_params=_cparams(("arbitrary",)),
        name="experts",
    )(blk_expert, nxt, last, parity.astype(jnp.int32), n_used, xs, w1, w3, w2, w1, w3, w2)


SC_CORES = 2
SC_SUBCORES = 16
SC_WORKERS = SC_CORES * SC_SUBCORES
SC_CHUNK = 64


def _sc_worker_base(per_worker):
    return (lax.axis_index("s") * SC_CORES + lax.axis_index("c")) * per_worker


def _sc_scatter_rows(v3, dests, n_rows, tok0):
    per_worker = dests[0].shape[0] // SC_WORKERS
    mesh = plsc.VectorSubcoreMesh(core_axis_name="c", subcore_axis_name="s")
    bufs = range(2)
    slab = v3.shape[1:]

    @functools.partial(
        pl.kernel, mesh=mesh,
        out_type=jax.ShapeDtypeStruct((n_rows,) + slab, v3.dtype),
        scratch_types=[pltpu.VMEM((2 * TOP_K, SC_CHUNK), jnp.int32),
                       pltpu.VMEM((2, SC_CHUNK) + slab, v3.dtype),
                       pltpu.SemaphoreType.DMA((2,)),
                       pltpu.SemaphoreType.DMA((2,))],
        name="sc_scatter")
    def scatter(v_hbm, d0_hbm, d1_hbm, xs_hbm, idx, rows, lsem, ssem):
        base = _sc_worker_base(per_worker)

        @pl.loop(0, per_worker, step=2 * SC_CHUNK)
        def _(c):
            loads = []
            for b in bufs:
                off = base + c + b * SC_CHUNK
                loads.append(pltpu.async_copy(v_hbm.at[pl.ds(tok0 + off, SC_CHUNK)], rows.at[b], lsem.at[b]))
                for k, d_hbm in enumerate((d0_hbm, d1_hbm)):
                    pltpu.sync_copy(d_hbm.at[pl.ds(off, SC_CHUNK)], idx.at[b * TOP_K + k])
            stores = []
            for b in bufs:
                loads[b].wait()
                for k in range(TOP_K):
                    stores.append(pltpu.async_copy(rows.at[b], xs_hbm.at[idx.at[b * TOP_K + k]], ssem.at[b]))
            for st in stores:
                st.wait()

    return scatter(v3, *dests)


def _sc_gather_rows(y3, dests):
    T = dests[0].shape[0]
    per_worker = T // SC_WORKERS
    mesh = plsc.VectorSubcoreMesh(core_axis_name="c", subcore_axis_name="s")
    bufs = range(2)
    slab = y3.shape[1:]

    @functools.partial(
        pl.kernel, mesh=mesh,
        out_type=jax.ShapeDtypeStruct((TOP_K, T) + slab, y3.dtype),
        scratch_types=[pltpu.VMEM((2, SC_CHUNK), jnp.int32),
                       pltpu.VMEM((2, SC_CHUNK) + slab, y3.dtype),
                       pltpu.SemaphoreType.DMA((2,)),
                       pltpu.SemaphoreType.DMA((2,))],
        name="sc_gather")
    def gather(y_hbm, d0_hbm, d1_hbm, out_hbm, idx, rows, gsem, wsem):
        base = _sc_worker_base(per_worker)
        for k, d_hbm in enumerate((d0_hbm, d1_hbm)):
            @pl.loop(0, per_worker, step=2 * SC_CHUNK)
            def _(c):
                gathers = []
                for b in bufs:
                    off = base + c + b * SC_CHUNK
                    pltpu.sync_copy(d_hbm.at[pl.ds(off, SC_CHUNK)], idx.at[b])
                    gathers.append(pltpu.async_copy(y_hbm.at[idx.at[b]], rows.at[b], gsem.at[b]))
                writes = []
                for b in bufs:
                    off = base + c + b * SC_CHUNK
                    gathers[b].wait()
                    writes.append(pltpu.async_copy(rows.at[b], out_hbm.at[k, pl.ds(off, SC_CHUNK)], wsem.at[b]))
                for w in writes:
                    w.wait()

    return gather(y3, *dests)


def _combine_body(yk_ref, x1_ref, ri_ref, g_ref, *rest, tc, final_norm):
    out_ref = rest[-1]

    ri = ri_ref[...]
    w0 = ri[:, 2:3]
    w1 = ri[:, 3:4]
    x2 = x1_ref[...] + (w0 * _load_packed_rows(yk_ref.at[0], tc, F32) + w1 * _load_packed_rows(yk_ref.at[1], tc, F32))
    if final_norm:
        ms = jnp.mean(x2 * x2, axis=-1, keepdims=True)
        x2 = x2 * lax.rsqrt(ms + EPS) * g_ref[...]
    out_ref[...] = x2


def _combine(yk, x1, ri, g, out_prev, *, tok0, final_norm, tc=256):
    T = x1.shape[0]
    steps = yk.shape[1] // (tc * PACK_CHUNKS)
    blk0 = tok0 // tc
    in_specs = [
        pl.BlockSpec((TOP_K, tc * PACK_CHUNKS, LANES), lambda i: (0, i, 0)),
        pl.BlockSpec((tc, D_MODEL), lambda i: (i + blk0, 0)),
        pl.BlockSpec((tc, LANES), lambda i: (i + blk0, 0)),
        pl.BlockSpec((1, D_MODEL), lambda i: (0, 0)),
    ]
    args = [yk, x1, ri, g]
    aliases = {}
    if out_prev is not None:
        in_specs.append(pl.BlockSpec(memory_space=pl.ANY))
        args.append(out_prev)
        aliases = {len(args) - 1: 0}
    return pl.pallas_call(
        functools.partial(_combine_body, tc=tc, final_norm=final_norm),
        grid=(steps,),
        in_specs=in_specs,
        out_specs=pl.BlockSpec((tc, D_MODEL), lambda i: (i + blk0, 0)),
        out_shape=jax.ShapeDtypeStruct((T, D_MODEL), F32),
        input_output_aliases=aliases,
        compiler_params=_cparams(("arbitrary",)),
        name="combine",
    )(*args)


def _pad_lanes(a, width=LANES):
    return jnp.pad(a, ((0, 0), (0, width - a.shape[-1])))


def _layer(x2, batch, seq, g_mix, w_in, conv_w, conv_b, w_rg_a, b_rg_a, w_rg_x, b_rg_x, lam, b_forget,
           w_lru_out, w_attn_out, w_out, g_ffn, w_rgrp, b_rgrp, w_rexp, b_rexp, w1, w3, w2, g_out, final_norm):
    T = batch * seq
    row = lambda a: a.reshape(1, -1).astype(F32)

    fl0 = 5 * D_MODEL
    w_main = jnp.concatenate([w_in[:, :fl0], w_in[:, fl0 + N_HEADS:]], axis=1).astype(BF16)
    w_fl = _pad_lanes(w_in[:, fl0:fl0 + N_HEADS]).astype(BF16)
    z, fl = _inproj(x2, row(g_mix), w_main, w_fl)

    caug = _forget(fl, _pad_lanes(row(b_forget)), batch=batch, seq=seq)
    o = _attention(z, caug, batch=batch, seq=seq)

    hg = _lru(z, conv_w.astype(F32), row(conv_b), w_rg_a.astype(BF16), w_rg_x.astype(BF16),
              row(b_rg_a), row(b_rg_x), row(lam), batch=batch, seq=seq)

    wr = _pad_lanes(jnp.concatenate([w_rgrp, w_rexp], axis=1)).astype(BF16)
    br = _pad_lanes(jnp.concatenate([row(b_rgrp), row(b_rexp)], axis=1))
    x1, v, ri, rt, cnt = _merge(hg, o, z, x2, w_lru_out.astype(BF16), w_attn_out.astype(BF16),
                            w_out.astype(BF16), row(g_ffn), wr, br)

    v3 = v.reshape(T, PACK_CHUNKS, LANES)
    g_out = row(g_out)
    Tp = T // MOE_PARTS
    n_blocks = (Tp * TOP_K + N_EXPERTS * (EXPERT_ROWS - 1) + EXPERT_ROWS - 1) // EXPERT_ROWS
    blk_start = jnp.arange(n_blocks, dtype=jnp.int32) * EXPERT_ROWS
    out = None
    for part in range(MOE_PARTS):
        tok0 = part * Tp
        e = rt[0:TOP_K, tok0:tok0 + Tp].astype(jnp.int32)
        rank = rt[4:4 + TOP_K, tok0:tok0 + Tp].astype(jnp.int32)
        counts = cnt[part, 0, R_EXP0:R_EXP0 + N_EXPERTS].astype(jnp.int32)
        padded = (counts + EXPERT_ROWS - 1) // EXPERT_ROWS * EXPERT_ROWS
        pad_end = jnp.cumsum(padded)
        pad_start = pad_end - padded
        seg_start = jnp.zeros_like(e)
        for j in range(N_EXPERTS):
            seg_start = jnp.where(e == j, pad_start[j], seg_start)
        dest2 = (seg_start + rank).astype(jnp.int32)
        dests = [dest2[k] for k in range(TOP_K)]
        blk_expert = jnp.minimum(
            jnp.sum((pad_end[None, :] <= blk_start[:, None]).astype(jnp.int32), axis=1), N_EXPERTS - 1)
        n_used = (pad_end[-1:] // EXPERT_ROWS).astype(jnp.int32)
        fill = jnp.concatenate([pad_start + counts, padded - counts, n_used]).astype(jnp.int32)

        xs = _sc_scatter_rows(v3, dests, n_blocks * EXPERT_ROWS, tok0)
        xs = _fill_padding(fill, xs)
        y = _experts(blk_expert, n_used, xs.reshape(-1, LANES), w1, w3, w2)
        yk = _sc_gather_rows(y.reshape(-1, PACK_CHUNKS, LANES), dests)
        out = _combine(yk.reshape(TOP_K, Tp * PACK_CHUNKS, LANES), x1, ri, g_out, out,
                       tok0=tok0, final_norm=final_norm)
    return out


def kernel(x, g_mix, w_in, conv_w, conv_b, w_rg_a, b_rg_a, w_rg_x, b_rg_x, lru_lambda, b_forget, w_lru_out, w_attn_out, w_out, g_ffn, w_route_group, b_route_group, w_route_expert, b_route_expert, w_exp_gate, w_exp_up, w_exp_down, g_final):
    batch, seq, _ = x.shape
    depth = g_mix.shape[0]
    x2 = x.reshape(batch * seq, D_MODEL)
    for l in range(depth):
        x2 = _layer(
            x2, batch, seq, g_mix[l], w_in[l], conv_w[l], conv_b[l], w_rg_a[l], b_rg_a[l], w_rg_x[l],
            b_rg_x[l], lru_lambda[l], b_forget[l], w_lru_out[l], w_attn_out[l], w_out[l], g_ffn[l],
            w_route_group[l], b_route_group[l], w_route_expert[l], b_route_expert[l],
            w_exp_gate[l], w_exp_up[l], w_exp_down[l], g_final, l == depth - 1)
    return x2.reshape(batch, seq, D_MODEL)
```

```python
import functools

import jax
import jax.numpy as jnp
from jax import lax
from jax.experimental import pallas as pl
from jax.experimental.pallas import tpu as pltpu
from jax.experimental.pallas import tpu_sc as plsc

F32 = jnp.float32
BF16 = jnp.bfloat16

D_MODEL = 1024
LRU_BLOCK_W = 256
LRU_BLOCKS = D_MODEL // LRU_BLOCK_W
CONV_W = 4
LRU_C = 8.0
N_HEADS = 8
HEAD_DIM = D_MODEL // N_HEADS
N_GROUPS = 4
EXPERTS_PER_GROUP = 8
N_EXPERTS = N_GROUPS * EXPERTS_PER_GROUP
TOP_K = 2
D_EXPERT = D_MODEL // 2
EPS = 1e-6

LANES = 128
SUBLANES = 8
PACK_CHUNKS = D_MODEL // LANES // 2
U32 = jnp.uint32
VMEM_LIMIT = 48 * 1024 * 1024

ZC_XR, ZC_GR, ZC_Q, ZC_K, ZC_V, ZC_ML, ZC_MA = 0, 8, 16, 24, 32, 40, 48
Z_WIDTH = 7 * D_MODEL

R_EXP0 = N_GROUPS

EXPERT_ROWS = 256


def _cparams(sem):
    return pltpu.CompilerParams(dimension_semantics=sem, vmem_limit_bytes=VMEM_LIMIT)


def _store_packed_rows(ref, first_row, x):
    n = x.shape[0]
    bits = lax.bitcast_convert_type(x.astype(BF16).astype(F32), U32)
    half = PACK_CHUNKS * LANES
    for s in range(PACK_CHUNKS):
        word = (bits[:, s * LANES:(s + 1) * LANES] >> 16) | bits[:, half + s * LANES:half + (s + 1) * LANES]
        ref[pl.ds(first_row + s, n, stride=PACK_CHUNKS), :] = word


def _load_packed_rows(ref, n, dtype):
    words = [ref[pl.ds(s, n, stride=PACK_CHUNKS), :] for s in range(PACK_CHUNKS)]
    low = [lax.bitcast_convert_type(w << 16, F32).astype(dtype) for w in words]
    high = [lax.bitcast_convert_type(w & jnp.uint32(0xFFFF0000), F32).astype(dtype) for w in words]
    return jnp.concatenate(low + high, axis=-1)


LOG2E = 1.4426950408889634
Q_PRESCALE = HEAD_DIM ** -0.5 * LOG2E


def _inproj_body(x_ref, g_ref, w_ref, wfl_ref, z_ref, fl_ref, u_ref, *, q_block):
    j = pl.program_id(1)

    @pl.when(j == 0)
    def _():
        x = x_ref[...]
        ms = jnp.mean(x * x, axis=-1, keepdims=True)
        u = (x * lax.rsqrt(ms + EPS) * g_ref[...]).astype(BF16)
        u_ref[...] = u
        fl_ref[...] = jnp.dot(u, wfl_ref[...], preferred_element_type=F32)

    @pl.when(j == q_block)
    def _():
        acc = jnp.dot(u_ref[...], w_ref[...], preferred_element_type=F32)
        z_ref[...] = (acc * Q_PRESCALE).astype(BF16)

    @pl.when(j != q_block)
    def _():
        z_ref[...] = jnp.dot(u_ref[...], w_ref[...], preferred_element_type=F32).astype(BF16)


def _inproj(x2, g, w_main, w_fl, *, tm=2048, tn=D_MODEL):
    T = x2.shape[0]
    return pl.pallas_call(
        functools.partial(_inproj_body, q_block=ZC_Q * LANES // tn),
        grid=(T // tm, Z_WIDTH // tn),
        in_specs=[
            pl.BlockSpec((tm, D_MODEL), lambda i, j: (i, 0)),
            pl.BlockSpec((1, D_MODEL), lambda i, j: (0, 0)),
            pl.BlockSpec((D_MODEL, tn), lambda i, j: (0, j)),
            pl.BlockSpec((D_MODEL, LANES), lambda i, j: (0, 0)),
        ],
        out_specs=[
            pl.BlockSpec((tm, tn), lambda i, j: (i, j)),
            pl.BlockSpec((tm, LANES), lambda i, j: (i, 0)),
        ],
        out_shape=[
            jax.ShapeDtypeStruct((T, Z_WIDTH), BF16),
            jax.ShapeDtypeStruct((T, LANES), F32),
        ],
        scratch_shapes=[pltpu.VMEM((tm, D_MODEL), BF16)],
        compiler_params=_cparams(("arbitrary", "arbitrary")),
        name="inproj",
    )(x2, g, w_main, w_fl)


C_TERMS = 3


def _log_sigmoid(z):
    return jnp.minimum(z, 0.0) - jnp.log1p(jnp.exp(-jnp.abs(z)))


def _forget_body(fl_ref, b_ref, c_ref, *, seq):
    lf = _log_sigmoid(fl_ref[...] + b_ref[...])
    row = lax.broadcasted_iota(jnp.int32, lf.shape, 0)
    lane = lax.broadcasted_iota(jnp.int32, lf.shape, 1)
    c = lf
    k = 1
    while k < seq:
        c = c + jnp.where(row >= k, pltpu.roll(c, k, axis=0), 0.0)
        k *= 2
    rem = c * LOG2E
    out = jnp.zeros(lf.shape, F32)
    for n in range(C_TERMS):
        t = rem.astype(BF16).astype(F32)
        rem = rem - t
        shifted = t if n == 0 else pltpu.roll(t, n * N_HEADS, axis=1)
        out = jnp.where((lane >= n * N_HEADS) & (lane < (n + 1) * N_HEADS), shifted, out)
    c_ref[...] = out.astype(BF16)


def _forget(fl, b_pad, *, batch, seq):
    return pl.pallas_call(
        functools.partial(_forget_body, seq=seq),
        grid=(batch,),
        in_specs=[
            pl.BlockSpec((seq, LANES), lambda b: (b, 0)),
            pl.BlockSpec((1, LANES), lambda b: (0, 0)),
        ],
        out_specs=pl.BlockSpec((seq, LANES), lambda b: (b, 0)),
        out_shape=jax.ShapeDtypeStruct((batch * seq, LANES), BF16),
        compiler_params=_cparams(("arbitrary",)),
        name="forget",
    )(fl, b_pad)


ATTN_HEADS_PER_STEP = 4


def _attn_body(q_ref, k_ref, v_ref, kc_ref, o_ref, *, tq):
    i = pl.program_id(2)
    lane = lax.broadcasted_iota(jnp.int32, (tq, LANES), 1)
    heads = range(ATTN_HEADS_PER_STEP)
    cols =[slice(g * HEAD_DIM, (g + 1) * HEAD_DIM) for g in heads]
    qs = []
    for g in heads:
        h = pl.program_id(1) * ATTN_HEADS_PER_STEP + g
        mine = (lane < C_TERMS * N_HEADS) & ((lane & (N_HEADS - 1)) == h)
        qc = jnp.where(mine, -1.0, 0.0).astype(BF16)
        qs.append(jnp.concatenate([q_ref[:, cols[g]], qc], axis=1))

    def step(j, carry, masked):
        off = pl.multiple_of(j * tq, tq)
        out = []
        for g in heads:
            m, l, acc = carry[g]
            kj = jnp.concatenate([k_ref[pl.ds(off, tq), cols[g]], kc_ref[pl.ds(off, tq), :]], axis=1)
            s = lax.dot_general(qs[g], kj, (((1,), (1,)), ((), ())), preferred_element_type=F32)
            if masked:
                r = lax.broadcasted_iota(jnp.int32, s.shape, 0)
                cidx = lax.broadcasted_iota(jnp.int32, s.shape, 1)
                s = jnp.where(cidx <= r, s, -jnp.inf)
            m_new = jnp.maximum(m, jnp.max(s, axis=-1, keepdims=True))
            p = jnp.exp2(s - m_new)
            alpha = jnp.exp2(m - m_new)
            l = alpha * l + jnp.sum(p, axis=-1, keepdims=True)
            acc = alpha * acc + jnp.dot(p.astype(BF16), v_ref[pl.ds(off, tq), cols[g]],
                                        preferred_element_type=F32)
            out.append((m_new, l, acc))
        return tuple(out)

    init = tuple((jnp.full((tq, 1), -jnp.inf, F32), jnp.zeros((tq, 1), F32), jnp.zeros((tq, HEAD_DIM), F32))
                 for _ in heads)
    carry = lax.fori_loop(0, i, lambda j, c: step(j, c, False), init)
    carry = step(i, carry, True)
    for g in heads:
        _, l, acc = carry[g]
        o_ref[:, cols[g]] = (acc / l).astype(BF16)


def _attention(z, caug, *, batch, seq, tq=512):
    nq = seq // tq
    T = batch * seq
    G = ATTN_HEADS_PER_STEP
    W = G * HEAD_DIM
    return pl.pallas_call(
        functools.partial(_attn_body, tq=tq),
        grid=(batch, N_HEADS // G, nq),
        in_specs=[
            pl.BlockSpec((tq, W), lambda b, h, i: (b * nq + i, ZC_Q // G + h)),
            pl.BlockSpec((seq, W), lambda b, h, i: (b, ZC_K // G + h)),
            pl.BlockSpec((seq, W), lambda b, h, i: (b, ZC_V // G + h)),
            pl.BlockSpec((seq, LANES), lambda b, h, i: (b, 0)),
        ],
        out_specs=pl.BlockSpec((tq, W), lambda b, h, i: (b * nq + i, h)),
        out_shape=jax.ShapeDtypeStruct((T, D_MODEL), BF16),
        compiler_params=_cparams(("arbitrary", "arbitrary", "arbitrary")),
        name="attn",
    )(z, z, z, caug)


def _lru_body(xr_ref, gr_ref, cw_ref, cb_ref, wa_ref, wx_ref, ba_ref, bx_ref, lam_ref,
              hg_ref, xbuf, hcar, *, ts):
    i = pl.program_id(1)

    @pl.when(i == 0)
    def _():
        xbuf[0:SUBLANES, :] = jnp.zeros((SUBLANES, D_MODEL), F32)
        hcar[...] = jnp.zeros_like(hcar)

    @pl.when(i > 0)
    def _():
        xbuf[0:SUBLANES, :] = xbuf[ts:ts + SUBLANES, :]

    xbuf[SUBLANES:ts + SUBLANES, :] = xr_ref[...].astype(F32)

    base = SUBLANES - (CONV_W - 1)
    xc = cw_ref[0:1, :] * xbuf[base:base + ts, :]
    for k in range(1, CONV_W):
        xc = xc + cw_ref[k:k + 1, :] * xbuf[base + k:base + k + ts, :]
    xc = xc + cb_ref[...]

    xcb = xc.astype(BF16)
    ra = jnp.concatenate(
        [jnp.dot(xcb[:, n * LRU_BLOCK_W:(n + 1) * LRU_BLOCK_W], wa_ref[n], preferred_element_type=F32)
         for n in range(LRU_BLOCKS)], axis=-1)
    rx = jnp.concatenate(
        [jnp.dot(xcb[:, n * LRU_BLOCK_W:(n + 1) * LRU_BLOCK_W], wx_ref[n], preferred_element_type=F32)
         for n in range(LRU_BLOCKS)], axis=-1)
    r = jax.nn.sigmoid(ra + ba_ref[...])
    ig = jax.nn.sigmoid(rx + bx_ref[...])
    nlam = -lam_ref[...]
    softplus = jnp.maximum(nlam, 0.0) + jnp.log1p(jnp.exp(-jnp.abs(nlam)))
    log_a = (-LRU_C * r) * softplus
    a = jnp.exp(log_a)
    th = jnp.tanh(log_a)
    mult = jnp.sqrt(-2.0 * th / (1.0 - th))
    b = mult * ig * xc

    row = lax.broadcasted_iota(jnp.int32, (SUBLANES, D_MODEL), 0)
    keeps = [(k, row >= k) for k in (1, 2, 4)]
    hprev = jnp.broadcast_to(hcar[...], (SUBLANES, D_MODEL))
    pieces = []
    for j in range(ts // SUBLANES):
        aj = a[j * SUBLANES:(j + 1) * SUBLANES, :]
        bj = b[j * SUBLANES:(j + 1) * SUBLANES, :]
        for k, keep in keeps:
            a_sh = jnp.where(keep, pltpu.roll(aj, k, axis=0), 1.0)
            b_sh = jnp.where(keep, pltpu.roll(bj, k, axis=0), 0.0)
            bj = aj * b_sh + bj
            aj = aj * a_sh
        hj = bj + aj * hprev
        hprev = jnp.broadcast_to(hj[SUBLANES - 1:SUBLANES, :], (SUBLANES, D_MODEL))
        pieces.append(hj)
    h = jnp.concatenate(pieces, axis=0)
    hcar[...] = h[ts - 1:ts, :]

    hg_ref[...] = (h * jax.nn.gelu(gr_ref[...].astype(F32))).astype(BF16)


def _lru(z, conv_w, conv_b, wa, wx, ba, bx, lam, *, batch, seq, ts=512):
    ns = seq // ts
    T = batch * seq
    full = lambda shape: pl.BlockSpec(shape, lambda b, i: (0,) * len(shape))
    return pl.pallas_call(
        functools.partial(_lru_body, ts=ts),
        grid=(batch, ns),
        in_specs=[
            pl.BlockSpec((ts, D_MODEL), lambda b, i: (b * ns + i, ZC_XR // SUBLANES)),
            pl.BlockSpec((ts, D_MODEL), lambda b, i: (b * ns + i, ZC_GR // SUBLANES)),
            full((CONV_W, D_MODEL)),
            full((1, D_MODEL)),
            full((LRU_BLOCKS, LRU_BLOCK_W, LRU_BLOCK_W)),
            full((LRU_BLOCKS, LRU_BLOCK_W, LRU_BLOCK_W)),
            full((1, D_MODEL)),
            full((1, D_MODEL)),
            full((1, D_MODEL)),
        ],
        out_specs=pl.BlockSpec((ts, D_MODEL), lambda b, i: (b * ns + i, 0)),
        out_shape=jax.ShapeDtypeStruct((T, D_MODEL), BF16),
        scratch_shapes=[pltpu.VMEM((ts + SUBLANES, D_MODEL), F32), pltpu.VMEM((1, D_MODEL), F32)],
        compiler_params=_cparams(("arbitrary", "arbitrary")),
        name="lru",
    )(z, z, conv_w, conv_b, wa, wx, ba, bx, lam)


MERGE_SUBTILES = 1
MOE_PARTS = 2


def _merge_body(hg_ref, o_ref, ml_ref, ma_ref, x_ref, wl_ref, wat_ref, wo_ref, g_ref, wr_ref, br_ref,
                x1_ref, v_ref, ri_ref, rt_ref, cnt_ref, carry_ref, *, tm, steps_per_part):
    @pl.when(pl.program_id(0) % steps_per_part == 0)
    def _():
        carry_ref[...] = jnp.zeros_like(carry_ref)

    ts = tm // MERGE_SUBTILES
    lane = lax.broadcasted_iota(jnp.int32, (ts, LANES), 1)
    rr = lax.broadcasted_iota(jnp.int32, (ts, ts), 0)
    cc = lax.broadcasted_iota(jnp.int32, (ts, ts), 1)
    tri = jnp.where(cc < rr, 1.0, 0.0).astype(BF16)
    ninf = -jnp.inf
    big = jnp.int32(1 << 20)
    carry = carry_ref[...]

    for sub in range(MERGE_SUBTILES):
        rows = slice(sub * ts, (sub + 1) * ts)
        yl = jnp.dot(hg_ref[rows, :], wl_ref[...], preferred_element_type=F32)
        ya = jnp.dot(o_ref[rows, :], wat_ref[...], preferred_element_type=F32)
        merged = (jax.nn.sigmoid(ml_ref[rows, :].astype(F32)) * yl
                  + jax.nn.sigmoid(ma_ref[rows, :].astype(F32)) * ya)
        x1 = x_ref[rows, :] + jnp.dot(merged.astype(BF16), wo_ref[...], preferred_element_type=F32)
        x1_ref[rows, :] = x1
        ms = jnp.mean(x1 * x1, axis=-1, keepdims=True)
        v = x1 * lax.rsqrt(ms + EPS) * g_ref[...]
        _store_packed_rows(v_ref, sub * ts * PACK_CHUNKS, v)

        logits = jnp.dot(v.astype(BF16), wr_ref[...], preferred_element_type=F32) + br_ref[...]

        gl = jnp.where(lane < N_GROUPS, logits, ninf)
        gmax = jnp.max(gl, axis=-1, keepdims=True)
        gsel = jnp.min(jnp.where(gl == gmax, lane, big), axis=-1, keepdims=True)
        pg = 1.0 / jnp.sum(jnp.exp(gl - gmax), axis=-1, keepdims=True)

        lo = R_EXP0 + gsel * EXPERTS_PER_GROUP
        el = jnp.where(lane >= lo, jnp.where(lane < lo + EXPERTS_PER_GROUP, logits, ninf), ninf)
        v1 = jnp.max(el, axis=-1, keepdims=True)
        i1 = jnp.min(jnp.where(el == v1, lane, big), axis=-1, keepdims=True)
        el2 = jnp.where(lane == i1, ninf, el)
        v2 = jnp.max(el2, axis=-1, keepdims=True)
        i2 = jnp.min(jnp.where(el2 == v2, lane, big), axis=-1, keepdims=True)
        e21 = jnp.exp(v2 - v1)
        p1 = 1.0 / (1.0 + e21)
        w0 = pg * p1
        w1 = pg * (e21 * p1)

        hit0 = lane == i1
        hit1 = lane == i2
        onehot = jnp.where(hit0, 1.0, jnp.where(hit1, 1.0, 0.0))
        before = jnp.dot(tri, onehot.astype(BF16), preferred_element_type=F32) + carry
        rank0 = jnp.sum(jnp.where(hit0, before, 0.0), axis=-1, keepdims=True)
        rank1 = jnp.sum(jnp.where(hit1, before, 0.0), axis=-1, keepdims=True)
        carry = carry + jnp.sum(onehot, axis=0, keepdims=True)

        e0 = (i1 - R_EXP0).astype(F32)
        e1 = (i2 - R_EXP0).astype(F32)
        ri = jnp.where(lane == 0, e0,
             jnp.where(lane == 1, e1,
             jnp.where(lane == 2, w0,
             jnp.where(lane == 3, w1,
             jnp.where(lane == 4, rank0,
             jnp.where(lane == 5, rank1, 0.0))))))
        ri_ref[rows, :] = ri
        rt_ref[:, rows] = ri.T[:SUBLANES, :]

    carry_ref[...] = carry
    cnt_ref[0] = carry


def _merge(hg, o, z, x2, wl, wat, wo, g, wr, br, *, tm=512):
    T = x2.shape[0]
    steps_per_part = T // tm // MOE_PARTS
    full = lambda shape: pl.BlockSpec(shape, lambda i: (0,) * len(shape))
    return pl.pallas_call(
        functools.partial(_merge_body, tm=tm, steps_per_part=steps_per_part),
        grid=(T // tm,),
        in_specs=[
            pl.BlockSpec((tm, D_MODEL), lambda i: (i, 0)),
            pl.BlockSpec((tm, D_MODEL), lambda i: (i, 0)),
            pl.BlockSpec((tm, D_MODEL), lambda i: (i, ZC_ML // SUBLANES)),
            pl.BlockSpec((tm, D_MODEL), lambda i: (i, ZC_MA // SUBLANES)),
            pl.BlockSpec((tm, D_MODEL), lambda i: (i, 0)),
            full((D_MODEL, D_MODEL)),
            full((D_MODEL, D_MODEL)),
            full((D_MODEL, D_MODEL)),
            full((1, D_MODEL)),
            full((D_MODEL, LANES)),
            full((1, LANES)),
        ],
        out_specs=[
            pl.BlockSpec((tm, D_MODEL), lambda i: (i, 0)),
            pl.BlockSpec((tm * PACK_CHUNKS, LANES), lambda i: (i, 0)),
            pl.BlockSpec((tm, LANES), lambda i: (i, 0)),
            pl.BlockSpec((SUBLANES, tm), lambda i: (0, i)),
            pl.BlockSpec((1, 1, LANES), lambda i: (i // steps_per_part, 0, 0)),
        ],
        out_shape=[
            jax.ShapeDtypeStruct((T, D_MODEL), F32),
            jax.ShapeDtypeStruct((T * PACK_CHUNKS, LANES), U32),
            jax.ShapeDtypeStruct((T, LANES), F32),
            jax.ShapeDtypeStruct((SUBLANES, T), F32),
            jax.ShapeDtypeStruct((MOE_PARTS, 1, LANES), F32),
        ],
        scratch_shapes=[pltpu.VMEM((1, LANES), F32)],
        compiler_params=_cparams(("arbitrary",)),
        name="merge",
    )(hg, o, z, z, x2, wl, wat, wo, g, wr, br)


def _fill_padding_body(fill_ref, xs_in_ref, xs_ref, zblock, zsem, *, n_blocks):
    del xs_in_ref

    def for_each_padding_piece(fn):
        def per_expert(e, carry):
            start = fill_ref[e]
            n = fill_ref[N_EXPERTS + e]
            size = EXPERT_ROWS // 2
            while size >= 1:
                take = (n & size) != 0

                @pl.when(take)
                def _(start=start, size=size):
                    fn(pltpu.make_async_copy(zblock.at[pl.ds(0, size)], xs_ref.at[pl.ds(start, size)], zsem))

                start = start + jnp.where(take, size, 0)
                size //= 2
            return carry
        lax.fori_loop(0, N_EXPERTS, per_expert, 0)

    def zero_block_copy(blk):
        return pltpu.make_async_copy(zblock, xs_ref.at[pl.ds(blk * EXPERT_ROWS, EXPERT_ROWS)], zsem)

    def for_each_unused_block(fn):
        lax.fori_loop(fill_ref[2 * N_EXPERTS], n_blocks, lambda blk, c: (fn(zero_block_copy(blk)), c)[1], 0)

    zblock[...] = jnp.zeros_like(zblock)
    for_each_padding_piece(lambda cp: cp.start())
    for_each_unused_block(lambda cp: cp.start())
    for_each_padding_piece(lambda cp: cp.wait())
    for_each_unused_block(lambda cp: cp.wait())


def _fill_padding(fill, xs3):
    n_blocks = xs3.shape[0] // EXPERT_ROWS
    grid_spec = pltpu.PrefetchScalarGridSpec(
        num_scalar_prefetch=1,
        grid=(1,),
        in_specs=[pl.BlockSpec(memory_space=pl.ANY)],
        out_specs=pl.BlockSpec(memory_space=pl.ANY),
        scratch_shapes=[pltpu.VMEM((EXPERT_ROWS,) + xs3.shape[1:], xs3.dtype),
                        pltpu.SemaphoreType.DMA(())],
    )
    return pl.pallas_call(
        functools.partial(_fill_padding_body, n_blocks=n_blocks),
        grid_spec=grid_spec,
        out_shape=jax.ShapeDtypeStruct(xs3.shape, xs3.dtype),
        input_output_aliases={1: 0},
        compiler_params=_cparams(("arbitrary",)),
        name="fill_padding",
    )(fill, xs3)


def _expert_body(be_ref, nxt_ref, last_ref, par_ref, nu_ref, x_ref, f1_ref, f3_ref, f2_ref,
                 n1_ref, n3_ref, n2_ref, y_ref, w13b, w2b, *, rows):
    del be_ref, nxt_ref
    i = pl.program_id(0)
    slot = par_ref[i]

    def cast(w1_ref, w3_ref, w2_ref, dst):
        w13b[dst, :, :D_EXPERT] = w1_ref[0].astype(BF16)
        w13b[dst, :, D_EXPERT:] = w3_ref[0].astype(BF16)
        w2b[dst] = w2_ref[0].astype(BF16)

    @pl.when(i == 0)
    def _():
        cast(f1_ref, f3_ref, f2_ref, 0)

    @pl.when(i < nu_ref[0])
    def _():
        xb = _load_packed_rows(x_ref, rows, BF16)
        gu = jnp.dot(xb, w13b[slot], preferred_element_type=F32)
        hb = (jax.nn.silu(gu[:, :D_EXPERT]) * gu[:, D_EXPERT:]).astype(BF16)
        y = jnp.dot(hb, w2b[slot], preferred_element_type=F32)
        _store_packed_rows(y_ref, 0, y)

    @pl.when(i >= nu_ref[0])
    def _():
        y_ref[...] = jnp.zeros_like(y_ref)

    @pl.when(last_ref[i] == 1)
    def _():
        cast(n1_ref, n3_ref, n2_ref, 1 - slot)


def _experts(blk_expert, n_used, xs, w1, w3, w2, *, rows=EXPERT_ROWS):
    n_blocks = xs.shape[0] // (rows * PACK_CHUNKS)
    differs = blk_expert[1:] != blk_expert[:-1]
    last = jnp.concatenate([differs, jnp.zeros((1,), bool)]).astype(jnp.int32)
    parity = jnp.cumsum(jnp.concatenate([jnp.zeros((1,), jnp.int32), differs.astype(jnp.int32)])) % 2
    later = jnp.where(blk_expert[None, :] > blk_expert[:, None], blk_expert[None, :], N_EXPERTS)
    nxt = jnp.min(later, axis=1)
    nxt = jnp.where(nxt == N_EXPERTS, blk_expert, nxt).astype(jnp.int32)

    first = lambda shape: pl.BlockSpec(shape, lambda i, be, nx, la, pa, nu: (be[0], 0, 0),
                                       pipeline_mode=pl.Buffered(1))
    following = lambda shape: pl.BlockSpec(shape, lambda i, be, nx, la, pa, nu: (nx[i], 0, 0))
    grid_spec = pltpu.PrefetchScalarGridSpec(
        num_scalar_prefetch=5,
        grid=(n_blocks,),
        in_specs=[
            pl.BlockSpec((rows * PACK_CHUNKS, LANES),
                         lambda i, be, nx, la, pa, nu: (jnp.minimum(i, nu[0] - 1), 0)),
            first((1, D_MODEL, D_EXPERT)), first((1, D_MODEL, D_EXPERT)), first((1, D_EXPERT, D_MODEL)),
            following((1, D_MODEL, D_EXPERT)), following((1, D_MODEL, D_EXPERT)),
            following((1, D_EXPERT, D_MODEL)),
        ],
        out_specs=pl.BlockSpec((rows * PACK_CHUNKS, LANES), lambda i, be, nx, la, pa, nu: (i, 0)),
        scratch_shapes=[pltpu.VMEM((2, D_MODEL, 2 * D_EXPERT), BF16), pltpu.VMEM((2, D_EXPERT, D_MODEL), BF16)],
    )
    return pl.pallas_call(
        functools.partial(_expert_body, rows=rows),
        grid_spec=grid_spec,
        out_shape=jax.ShapeDtypeStruct(xs.shape, U32),
        compiler_params=_cparams(("arbitrary",)),
        name="experts",
    )(blk_expert, nxt, last, parity.astype(jnp.int32), n_used, xs, w1, w3, w2, w1, w3, w2)


SC_CORES = 2
SC_SUBCORES = 16
SC_WORKERS = SC_CORES * SC_SUBCORES
SC_CHUNK = 64


def _sc_worker_base(per_worker):
    return (lax.axis_index("s") * SC_CORES + lax.axis_index("c")) * per_worker


def _sc_scatter_rows(v3, dests, n_rows, tok0):
    per_worker = dests[0].shape[0] // SC_WORKERS
    mesh = plsc.VectorSubcoreMesh(core_axis_name="c", subcore_axis_name="s")
    bufs = range(2)
    slab = v3.shape[1:]

    @functools.partial(
        pl.kernel, mesh=mesh,
        out_type=jax.ShapeDtypeStruct((n_rows,) + slab, v3.dtype),
        scratch_types=[pltpu.VMEM((2 * TOP_K, SC_CHUNK), jnp.int32),
                       pltpu.VMEM((2, SC_CHUNK) + slab, v3.dtype),
                       pltpu.SemaphoreType.DMA((2,)),
                       pltpu.SemaphoreType.DMA((2,))],
        name="sc_scatter")
    def scatter(v_hbm, d0_hbm, d1_hbm, xs_hbm, idx, rows, lsem, ssem):
        base = _sc_worker_base(per_worker)

        @pl.loop(0, per_worker, step=2 * SC_CHUNK)
        def _(c):
            loads = []
            for b in bufs:
                off = base + c + b * SC_CHUNK
                loads.append(pltpu.async_copy(v_hbm.at[pl.ds(tok0 + off, SC_CHUNK)], rows.at[b], lsem.at[b]))
                for k, d_hbm in enumerate((d0_hbm, d1_hbm)):
                    pltpu.sync_copy(d_hbm.at[pl.ds(off, SC_CHUNK)], idx.at[b * TOP_K + k])
            stores = []
            for b in bufs:
                loads[b].wait()
                for k in range(TOP_K):
                    stores.append(pltpu.async_copy(rows.at[b], xs_hbm.at[idx.at[b * TOP_K + k]], ssem.at[b]))
            for st in stores:
                st.wait()

    return scatter(v3, *dests)


def _sc_gather_rows(y3, dests):
    T = dests[0].shape[0]
    per_worker = T // SC_WORKERS
    mesh = plsc.VectorSubcoreMesh(core_axis_name="c", subcore_axis_name="s")
    bufs = range(2)
    slab = y3.shape[1:]

    @functools.partial(
        pl.kernel, mesh=mesh,
        out_type=jax.ShapeDtypeStruct((TOP_K, T) + slab, y3.dtype),
        scratch_types=[pltpu.VMEM((2, SC_CHUNK), jnp.int32),
                       pltpu.VMEM((2, SC_CHUNK) + slab, y3.dtype),
                       pltpu.SemaphoreType.DMA((2,)),
                       pltpu.SemaphoreType.DMA((2,))],
        name="sc_gather")
    def gather(y_hbm, d0_hbm, d1_hbm, out_hbm, idx, rows, gsem, wsem):
        base = _sc_worker_base(per_worker)
        for k, d_hbm in enumerate((d0_hbm, d1_hbm)):
            @pl.loop(0, per_worker, step=2 * SC_CHUNK)
            def _(c):
                gathers = []
                for b in bufs:
                    off = base + c + b * SC_CHUNK
                    pltpu.sync_copy(d_hbm.at[pl.ds(off, SC_CHUNK)], idx.at[b])
                    gathers.append(pltpu.async_copy(y_hbm.at[idx.at[b]], rows.at[b], gsem.at[b]))
                writes = []
                for b in bufs:
                    off = base + c + b * SC_CHUNK
                    gathers[b].wait()
                    writes.append(pltpu.async_copy(rows.at[b], out_hbm.at[k, pl.ds(off, SC_CHUNK)], wsem.at[b]))
                for w in writes:
                    w.wait()

    return gather(y3, *dests)


def _combine_body(yk_ref, x1_ref, ri_ref, g_ref, *rest, tc, final_norm):
    out_ref = rest[-1]

    ri = ri_ref[...]
    w0 = ri[:, 2:3]
    w1 = ri[:, 3:4]
    x2 = x1_ref[...] + (w0 * _load_packed_rows(yk_ref.at[0], tc, F32) + w1 * _load_packed_rows(yk_ref.at[1], tc, F32))
    if final_norm:
        ms = jnp.mean(x2 * x2, axis=-1, keepdims=True)
        x2 = x2 * lax.rsqrt(ms + EPS) * g_ref[...]
    out_ref[...] = x2


def _combine(yk, x1, ri, g, out_prev, *, tok0, final_norm, tc=256):
    T = x1.shape[0]
    steps = yk.shape[1] // (tc * PACK_CHUNKS)
    blk0 = tok0 // tc
    in_specs = [
        pl.BlockSpec((TOP_K, tc * PACK_CHUNKS, LANES), lambda i: (0, i, 0)),
        pl.BlockSpec((tc, D_MODEL), lambda i: (i + blk0, 0)),
        pl.BlockSpec((tc, LANES), lambda i: (i + blk0, 0)),
        pl.BlockSpec((1, D_MODEL), lambda i: (0, 0)),
    ]
    args = [yk, x1, ri, g]
    aliases = {}
    if out_prev is not None:
        in_specs.append(pl.BlockSpec(memory_space=pl.ANY))
        args.append(out_prev)
        aliases = {len(args) - 1: 0}
    return pl.pallas_call(
        functools.partial(_combine_body, tc=tc, final_norm=final_norm),
        grid=(steps,),
        in_specs=in_specs,
        out_specs=pl.BlockSpec((tc, D_MODEL), lambda i: (i + blk0, 0)),
        out_shape=jax.ShapeDtypeStruct((T, D_MODEL), F32),
        input_output_aliases=aliases,
        compiler_params=_cparams(("arbitrary",)),
        name="combine",
    )(*args)


def _pad_lanes(a, width=LANES):
    return jnp.pad(a, ((0, 0), (0, width - a.shape[-1])))


def _layer(x2, batch, seq, g_mix, w_in, conv_w, conv_b, w_rg_a, b_rg_a, w_rg_x, b_rg_x, lam, b_forget,
           w_lru_out, w_attn_out, w_out, g_ffn, w_rgrp, b_rgrp, w_rexp, b_rexp, w1, w3, w2, g_out, final_norm):
    T = batch * seq
    row = lambda a: a.reshape(1, -1).astype(F32)

    fl0 = 5 * D_MODEL
    w_main = jnp.concatenate([w_in[:, :fl0], w_in[:, fl0 + N_HEADS:]], axis=1).astype(BF16)
    w_fl = _pad_lanes(w_in[:, fl0:fl0 + N_HEADS]).astype(BF16)
    z, fl = _inproj(x2, row(g_mix), w_main, w_fl)

    caug = _forget(fl, _pad_lanes(row(b_forget)), batch=batch, seq=seq)
    o = _attention(z, caug, batch=batch, seq=seq)

    hg = _lru(z, conv_w.astype(F32), row(conv_b), w_rg_a.astype(BF16), w_rg_x.astype(BF16),
              row(b_rg_a), row(b_rg_x), row(lam), batch=batch, seq=seq)

    wr = _pad_lanes(jnp.concatenate([w_rgrp, w_rexp], axis=1)).astype(BF16)
    br = _pad_lanes(jnp.concatenate([row(b_rgrp), row(b_rexp)], axis=1))
    x1, v, ri, rt, cnt = _merge(hg, o, z, x2, w_lru_out.astype(BF16), w_attn_out.astype(BF16),
                            w_out.astype(BF16), row(g_ffn), wr, br)

    v3 = v.reshape(T, PACK_CHUNKS, LANES)
    g_out = row(g_out)
    Tp = T // MOE_PARTS
    n_blocks = (Tp * TOP_K + N_EXPERTS * (EXPERT_ROWS - 1) + EXPERT_ROWS - 1) // EXPERT_ROWS
    blk_start = jnp.arange(n_blocks, dtype=jnp.int32) * EXPERT_ROWS
    out = None
    for part in range(MOE_PARTS):
        tok0 = part * Tp
        e = rt[0:TOP_K, tok0:tok0 + Tp].astype(jnp.int32)
        rank = rt[4:4 + TOP_K, tok0:tok0 + Tp].astype(jnp.int32)
        counts = cnt[part, 0, R_EXP0:R_EXP0 + N_EXPERTS].astype(jnp.int32)
        padded = (counts + EXPERT_ROWS - 1) // EXPERT_ROWS * EXPERT_ROWS
        pad_end = jnp.cumsum(padded)
        pad_start = pad_end - padded
        seg_start = jnp.zeros_like(e)
        for j in range(N_EXPERTS):
            seg_start = jnp.where(e == j, pad_start[j], seg_start)
        dest2 = (seg_start + rank).astype(jnp.int32)
        dests = [dest2[k] for k in range(TOP_K)]
        blk_expert = jnp.minimum(
            jnp.sum((pad_end[None, :] <= blk_start[:, None]).astype(jnp.int32), axis=1), N_EXPERTS - 1)
        n_used = (pad_end[-1:] // EXPERT_ROWS).astype(jnp.int32)
        fill = jnp.concatenate([pad_start + counts, padded - counts, n_used]).astype(jnp.int32)

        xs = _sc_scatter_rows(v3, dests, n_blocks * EXPERT_ROWS, tok0)
        xs = _fill_padding(fill, xs)
        y = _experts(blk_expert, n_used, xs.reshape(-1, LANES), w1, w3, w2)
        yk = _sc_gather_rows(y.reshape(-1, PACK_CHUNKS, LANES), dests)
        out = _combine(yk.reshape(TOP_K, Tp * PACK_CHUNKS, LANES), x1, ri, g_out, out,
                       tok0=tok0, final_norm=final_norm)
    return out


def kernel(x, g_mix, w_in, conv_w, conv_b, w_rg_a, b_rg_a, w_rg_x, b_rg_x, lru_lambda, b_forget, w_lru_out, w_attn_out, w_out, g_ffn, w_route_group, b_route_group, w_route_expert, b_route_expert, w_exp_gate, w_exp_up, w_exp_down, g_final):
    batch, seq, _ = x.shape
    depth = g_mix.shape[0]
    x2 = x.reshape(batch * seq, D_MODEL)
    for l in range(depth):
        x2 = _layer(
            x2, batch, seq, g_mix[l], w_in[l], conv_w[l], conv_b[l], w_rg_a[l], b_rg_a[l], w_rg_x[l],
            b_rg_x[l], lru_lambda[l], b_forget[l], w_lru_out[l], w_attn_out[l], w_out[l], g_ffn[l],
            w_route_group[l], b_route_group[l], w_route_expert[l], b_route_expert[l],
            w_exp_gate[l], w_exp_up[l], w_exp_down[l], g_final, l == depth - 1)
    return x2.reshape(batch, seq, D_MODEL)
```

```python
import functools

import jax
import jax.numpy as jnp
from jax import lax
from jax.experimental import pallas as pl
from jax.experimental.pallas import tpu as pltpu
from jax.experimental.pallas import tpu_sc as plsc

F32 = jnp.float32
BF16 = jnp.bfloat16

D_MODEL = 1024
LRU_BLOCK_W = 256
LRU_BLOCKS = D_MODEL // LRU_BLOCK_W
CONV_W = 4
LRU_C = 8.0
N_HEADS = 8
HEAD_DIM = D_MODEL // N_HEADS
N_GROUPS = 4
EXPERTS_PER_GROUP = 8
N_EXPERTS = N_GROUPS * EXPERTS_PER_GROUP
TOP_K = 2
D_EXPERT = D_MODEL // 2
EPS = 1e-6

LANES = 128
SUBLANES = 8
PACK_CHUNKS = D_MODEL // LANES // 2
U32 = jnp.uint32
VMEM_LIMIT = 48 * 1024 * 1024

ZC_XR, ZC_GR, ZC_Q, ZC_K, ZC_V, ZC_ML, ZC_MA = 0, 8, 16, 24, 32, 40, 48
Z_WIDTH = 7 * D_MODEL

R_EXP0 = N_GROUPS

EXPERT_ROWS = 512


def _cparams(sem):
    return pltpu.CompilerParams(dimension_semantics=sem, vmem_limit_bytes=VMEM_LIMIT)


def _store_packed_rows(ref, first_row, x):
    n = x.shape[0]
    bits = lax.bitcast_convert_type(x.astype(BF16).astype(F32), U32)
    half = PACK_CHUNKS * LANES
    for s in range(PACK_CHUNKS):
        word = (bits[:, s * LANES:(s + 1) * LANES] >> 16) | bits[:, half + s * LANES:half + (s + 1) * LANES]
        ref[pl.ds(first_row + s, n, stride=PACK_CHUNKS), :] = word


def _load_packed_rows(ref, n, dtype):
    words = [ref[pl.ds(s, n, stride=PACK_CHUNKS), :] for s in range(PACK_CHUNKS)]
    low = [lax.bitcast_convert_type(w << 16, F32).astype(dtype) for w in words]
    high = [lax.bitcast_convert_type(w & jnp.uint32(0xFFFF0000), F32).astype(dtype) for w in words]
    return jnp.concatenate(low + high, axis=-1)


LOG2E = 1.4426950408889634
Q_PRESCALE = HEAD_DIM ** -0.5 * LOG2E


def _inproj_body(x_ref, g_ref, w_ref, wfl_ref, z_ref, fl_ref, u_ref, *, q_block):
    j = pl.program_id(1)

    @pl.when(j == 0)
    def _():
        x = x_ref[...]
        ms = jnp.mean(x * x, axis=-1, keepdims=True)
        u = (x * lax.rsqrt(ms + EPS) * g_ref[...]).astype(BF16)
        u_ref[...] = u
        fl_ref[...] = jnp.dot(u, wfl_ref[...], preferred_element_type=F32)

    @pl.when(j == q_block)
    def _():
        acc = jnp.dot(u_ref[...], w_ref[...], preferred_element_type=F32)
        z_ref[...] = (acc * Q_PRESCALE).astype(BF16)

    @pl.when(j != q_block)
    def _():
        z_ref[...] = jnp.dot(u_ref[...], w_ref[...], preferred_element_type=F32).astype(BF16)


def _inproj(x2, g, w_main, w_fl, *, tm=2048, tn=D_MODEL):
    T = x2.shape[0]
    return pl.pallas_call(
        functools.partial(_inproj_body, q_block=ZC_Q * LANES // tn),
        grid=(T // tm, Z_WIDTH // tn),
        in_specs=[
            pl.BlockSpec((tm, D_MODEL), lambda i, j: (i, 0)),
            pl.BlockSpec((1, D_MODEL), lambda i, j: (0, 0)),
            pl.BlockSpec((D_MODEL, tn), lambda i, j: (0, j)),
            pl.BlockSpec((D_MODEL, LANES), lambda i, j: (0, 0)),
        ],
        out_specs=[
            pl.BlockSpec((tm, tn), lambda i, j: (i, j)),
            pl.BlockSpec((tm, LANES), lambda i, j: (i, 0)),
        ],
        out_shape=[
            jax.ShapeDtypeStruct((T, Z_WIDTH), BF16),
            jax.ShapeDtypeStruct((T, LANES), F32),
        ],
        scratch_shapes=[pltpu.VMEM((tm, D_MODEL), BF16)],
        compiler_params=_cparams(("arbitrary", "arbitrary")),
        name="inproj",
    )(x2, g, w_main, w_fl)


C_TERMS = 3


def _log_sigmoid(z):
    return jnp.minimum(z, 0.0) - jnp.log1p(jnp.exp(-jnp.abs(z)))


def _forget_body(fl_ref, b_ref, c_ref, *, seq):
    lf = _log_sigmoid(fl_ref[...] + b_ref[...])
    row = lax.broadcasted_iota(jnp.int32, lf.shape, 0)
    lane = lax.broadcasted_iota(jnp.int32, lf.shape, 1)
    c = lf
    k = 1
    while k < seq:
        c = c + jnp.where(row >= k, pltpu.roll(c, k, axis=0), 0.0)
        k *= 2
    rem = c * LOG2E
    out = jnp.zeros(lf.shape, F32)
    for n in range(C_TERMS):
        t = rem.astype(BF16).astype(F32)
        rem = rem - t
        shifted = t if n == 0 else pltpu.roll(t, n * N_HEADS, axis=1)
        out = jnp.where((lane >= n * N_HEADS) & (lane < (n + 1) * N_HEADS), shifted, out)
    c_ref[...] = out.astype(BF16)


def _forget(fl, b_pad, *, batch, seq):
    return pl.pallas_call(
        functools.partial(_forget_body, seq=seq),
        grid=(batch,),
        in_specs=[
            pl.BlockSpec((seq, LANES), lambda b: (b, 0)),
            pl.BlockSpec((1, LANES), lambda b: (0, 0)),
        ],
        out_specs=pl.BlockSpec((seq, LANES), lambda b: (b, 0)),
        out_shape=jax.ShapeDtypeStruct((batch * seq, LANES), BF16),
        compiler_params=_cparams(("arbitrary",)),
        name="forget",
    )(fl, b_pad)


ATTN_HEADS_PER_STEP = 4


def _attn_body(q_ref, k_ref, v_ref, kc_ref, o_ref, *, tq):
    i = pl.program_id(2)
    lane = lax.broadcasted_iota(jnp.int32, (tq, LANES), 1)
    heads = range(ATTN_HEADS_PER_STEP)
    cols =[slice(g * HEAD_DIM, (g + 1) * HEAD_DIM) for g in heads]
    qs = []
    for g in heads:
        h = pl.program_id(1) * ATTN_HEADS_PER_STEP + g
        mine = (lane < C_TERMS * N_HEADS) & ((lane & (N_HEADS - 1)) == h)
        qc = jnp.where(mine, -1.0, 0.0).astype(BF16)
        qs.append(jnp.concatenate([q_ref[:, cols[g]], qc], axis=1))

    def step(j, carry, masked):
        off = pl.multiple_of(j * tq, tq)
        out = []
        for g in heads:
            m, l, acc = carry[g]
            kj = jnp.concatenate([k_ref[pl.ds(off, tq), cols[g]], kc_ref[pl.ds(off, tq), :]], axis=1)
            s = lax.dot_general(qs[g], kj, (((1,), (1,)), ((), ())), preferred_element_type=F32)
            if masked:
                r = lax.broadcasted_iota(jnp.int32, s.shape, 0)
                cidx = lax.broadcasted_iota(jnp.int32, s.shape, 1)
                s = jnp.where(cidx <= r, s, -jnp.inf)
            m_new = jnp.maximum(m, jnp.max(s, axis=-1, keepdims=True))
            p = jnp.exp2(s - m_new)
            alpha = jnp.exp2(m - m_new)
            l = alpha * l + jnp.sum(p, axis=-1, keepdims=True)
            acc = alpha * acc + jnp.dot(p.astype(BF16), v_ref[pl.ds(off, tq), cols[g]],
                                        preferred_element_type=F32)
            out.append((m_new, l, acc))
        return tuple(out)

    init = tuple((jnp.full((tq, 1), -jnp.inf, F32), jnp.zeros((tq, 1), F32), jnp.zeros((tq, HEAD_DIM), F32))
                 for _ in heads)
    carry = lax.fori_loop(0, i, lambda j, c: step(j, c, False), init)
    carry = step(i, carry, True)
    for g in heads:
        _, l, acc = carry[g]
        o_ref[:, cols[g]] = (acc / l).astype(BF16)


def _attention(z, caug, *, batch, seq, tq=512):
    nq = seq // tq
    T = batch * seq
    G = ATTN_HEADS_PER_STEP
    W = G * HEAD_DIM
    return pl.pallas_call(
        functools.partial(_attn_body, tq=tq),
        grid=(batch, N_HEADS // G, nq),
        in_specs=[
            pl.BlockSpec((tq, W), lambda b, h, i: (b * nq + i, ZC_Q // G + h)),
            pl.BlockSpec((seq, W), lambda b, h, i: (b, ZC_K // G + h)),
            pl.BlockSpec((seq, W), lambda b, h, i: (b, ZC_V // G + h)),
            pl.BlockSpec((seq, LANES), lambda b, h, i: (b, 0)),
        ],
        out_specs=pl.BlockSpec((tq, W), lambda b, h, i: (b * nq + i, h)),
        out_shape=jax.ShapeDtypeStruct((T, D_MODEL), BF16),
        compiler_params=_cparams(("arbitrary", "arbitrary", "arbitrary")),
        name="attn",
    )(z, z, z, caug)


def _lru_body(xr_ref, gr_ref, cw_ref, cb_ref, wa_ref, wx_ref, ba_ref, bx_ref, lam_ref,
              hg_ref, xbuf, hcar, *, ts):
    i = pl.program_id(1)

    @pl.when(i == 0)
    def _():
        xbuf[0:SUBLANES, :] = jnp.zeros((SUBLANES, D_MODEL), F32)
        hcar[...] = jnp.zeros_like(hcar)

    @pl.when(i > 0)
    def _():
        xbuf[0:SUBLANES, :] = xbuf[ts:ts + SUBLANES, :]

    xbuf[SUBLANES:ts + SUBLANES, :] = xr_ref[...].astype(F32)

    base = SUBLANES - (CONV_W - 1)
    xc = cw_ref[0:1, :] * xbuf[base:base + ts, :]
    for k in range(1, CONV_W):
        xc = xc + cw_ref[k:k + 1, :] * xbuf[base + k:base + k + ts, :]
    xc = xc + cb_ref[...]

    xcb = xc.astype(BF16)
    ra = jnp.concatenate(
        [jnp.dot(xcb[:, n * LRU_BLOCK_W:(n + 1) * LRU_BLOCK_W], wa_ref[n], preferred_element_type=F32)
         for n in range(LRU_BLOCKS)], axis=-1)
    rx = jnp.concatenate(
        [jnp.dot(xcb[:, n * LRU_BLOCK_W:(n + 1) * LRU_BLOCK_W], wx_ref[n], preferred_element_type=F32)
         for n in range(LRU_BLOCKS)], axis=-1)
    r = jax.nn.sigmoid(ra + ba_ref[...])
    ig = jax.nn.sigmoid(rx + bx_ref[...])
    nlam = -lam_ref[...]
    softplus = jnp.maximum(nlam, 0.0) + jnp.log1p(jnp.exp(-jnp.abs(nlam)))
    log_a = (-LRU_C * r) * softplus
    a = jnp.exp(log_a)
    th = jnp.tanh(log_a)
    mult = jnp.sqrt(-2.0 * th / (1.0 - th))
    b = mult * ig * xc

    row = lax.broadcasted_iota(jnp.int32, (SUBLANES, D_MODEL), 0)
    keeps = [(k, row >= k) for k in (1, 2, 4)]
    hprev = jnp.broadcast_to(hcar[...], (SUBLANES, D_MODEL))
    pieces = []
    for j in range(ts // SUBLANES):
        aj = a[j * SUBLANES:(j + 1) * SUBLANES, :]
        bj = b[j * SUBLANES:(j + 1) * SUBLANES, :]
        for k, keep in keeps:
            a_sh = jnp.where(keep, pltpu.roll(aj, k, axis=0), 1.0)
            b_sh = jnp.where(keep, pltpu.roll(bj, k, axis=0), 0.0)
            bj = aj * b_sh + bj
            aj = aj * a_sh
        hj = bj + aj * hprev
        hprev = jnp.broadcast_to(hj[SUBLANES - 1:SUBLANES, :], (SUBLANES, D_MODEL))
        pieces.append(hj)
    h = jnp.concatenate(pieces, axis=0)
    hcar[...] = h[ts - 1:ts, :]

    hg_ref[...] = (h * jax.nn.gelu(gr_ref[...].astype(F32))).astype(BF16)


def _lru(z, conv_w, conv_b, wa, wx, ba, bx, lam, *, batch, seq, ts=512):
    ns = seq // ts
    T = batch * seq
    full = lambda shape: pl.BlockSpec(shape, lambda b, i: (0,) * len(shape))
    return pl.pallas_call(
        functools.partial(_lru_body, ts=ts),
        grid=(batch, ns),
        in_specs=[
            pl.BlockSpec((ts, D_MODEL), lambda b, i: (b * ns + i, ZC_XR // SUBLANES)),
            pl.BlockSpec((ts, D_MODEL), lambda b, i: (b * ns + i, ZC_GR // SUBLANES)),
            full((CONV_W, D_MODEL)),
            full((1, D_MODEL)),
            full((LRU_BLOCKS, LRU_BLOCK_W, LRU_BLOCK_W)),
            full((LRU_BLOCKS, LRU_BLOCK_W, LRU_BLOCK_W)),
            full((1, D_MODEL)),
            full((1, D_MODEL)),
            full((1, D_MODEL)),
        ],
        out_specs=pl.BlockSpec((ts, D_MODEL), lambda b, i: (b * ns + i, 0)),
        out_shape=jax.ShapeDtypeStruct((T, D_MODEL), BF16),
        scratch_shapes=[pltpu.VMEM((ts + SUBLANES, D_MODEL), F32), pltpu.VMEM((1, D_MODEL), F32)],
        compiler_params=_cparams(("arbitrary", "arbitrary")),
        name="lru",
    )(z, z, conv_w, conv_b, wa, wx, ba, bx, lam)


MOE_PARTS = 1
COMBINE_PARTS = 2


def _merge_body(hg_ref, o_ref, ml_ref, ma_ref, x_ref, wl_ref, wat_ref, wo_ref, g_ref, wr_ref, br_ref,
                x1_ref, v_ref, ri_ref, rt_ref, cnt_ref, carry_ref, *, tm, steps_per_part):
    @pl.when(pl.program_id(0) % steps_per_part == 0)
    def _():
        carry_ref[...] = jnp.zeros_like(carry_ref)

    ts = tm
    lane = lax.broadcasted_iota(jnp.int32, (ts, LANES), 1)
    rr = lax.broadcasted_iota(jnp.int32, (ts, ts), 0)
    cc = lax.broadcasted_iota(jnp.int32, (ts, ts), 1)
    tri = jnp.where(cc < rr, 1.0, 0.0).astype(BF16)
    ninf = -jnp.inf
    big = jnp.int32(1 << 20)
    carry = carry_ref[...]

    def tile(sub, carry):
        rows = slice(sub * ts, (sub + 1) * ts)
        yl = jnp.dot(hg_ref[rows, :], wl_ref[...], preferred_element_type=F32)
        ya = jnp.dot(o_ref[rows, :], wat_ref[...], preferred_element_type=F32)
        merged = (jax.nn.sigmoid(ml_ref[rows, :].astype(F32)) * yl
                  + jax.nn.sigmoid(ma_ref[rows, :].astype(F32)) * ya)
        x1 = x_ref[rows, :] + jnp.dot(merged.astype(BF16), wo_ref[...], preferred_element_type=F32)
        x1_ref[rows, :] = x1
        ms = jnp.mean(x1 * x1, axis=-1, keepdims=True)
        v = x1 * lax.rsqrt(ms + EPS) * g_ref[...]
        _store_packed_rows(v_ref, sub * ts * PACK_CHUNKS, v)

        logits = jnp.dot(v.astype(BF16), wr_ref[...], preferred_element_type=F32) + br_ref[...]

        gl = jnp.where(lane < N_GROUPS, logits, ninf)
        gmax = jnp.max(gl, axis=-1, keepdims=True)
        gsel = jnp.min(jnp.where(gl == gmax, lane, big), axis=-1, keepdims=True)
        pg = 1.0 / jnp.sum(jnp.exp(gl - gmax), axis=-1, keepdims=True)

        lo = R_EXP0 + gsel * EXPERTS_PER_GROUP
        el = jnp.where(lane >= lo, jnp.where(lane < lo + EXPERTS_PER_GROUP, logits, ninf), ninf)
        v1 = jnp.max(el, axis=-1, keepdims=True)
        i1 = jnp.min(jnp.where(el == v1, lane, big), axis=-1, keepdims=True)
        el2 = jnp.where(lane == i1, ninf, el)
        v2 = jnp.max(el2, axis=-1, keepdims=True)
        i2 = jnp.min(jnp.where(el2 == v2, lane, big), axis=-1, keepdims=True)
        e21 = jnp.exp(v2 - v1)
        p1 = 1.0 / (1.0 + e21)
        w0 = pg * p1
        w1 = pg * (e21 * p1)

        hit0 = lane == i1
        hit1 = lane == i2
        onehot = jnp.where(hit0, 1.0, jnp.where(hit1, 1.0, 0.0))
        before = jnp.dot(tri, onehot.astype(BF16), preferred_element_type=F32) + carry
        rank0 = jnp.sum(jnp.where(hit0, before, 0.0), axis=-1, keepdims=True)
        rank1 = jnp.sum(jnp.where(hit1, before, 0.0), axis=-1, keepdims=True)
        carry = carry + jnp.sum(onehot, axis=0, keepdims=True)

        e0 = (i1 - R_EXP0).astype(F32)
        e1 = (i2 - R_EXP0).astype(F32)
        ri = jnp.where(lane == 0, e0,
             jnp.where(lane == 1, e1,
             jnp.where(lane == 2, w0,
             jnp.where(lane == 3, w1,
             jnp.where(lane == 4, rank0,
             jnp.where(lane == 5, rank1, 0.0))))))
        ri_ref[rows, :] = ri
        rt_ref[:, rows] = ri.T[:SUBLANES, :]
        return carry

    carry = tile(0, carry)
    carry_ref[...] = carry
    cnt_ref[0] = carry


def _merge(hg, o, z, x2, wl, wat, wo, g, wr, br, *, tm=512):
    T = x2.shape[0]
    steps_per_part = T // tm // MOE_PARTS
    full = lambda shape: pl.BlockSpec(shape, lambda i: (0,) * len(shape))
    return pl.pallas_call(
        functools.partial(_merge_body, tm=tm, steps_per_part=steps_per_part),
        grid=(T // tm,),
        in_specs=[
            pl.BlockSpec((tm, D_MODEL), lambda i: (i, 0)),
            pl.BlockSpec((tm, D_MODEL), lambda i: (i, 0)),
            pl.BlockSpec((tm, D_MODEL), lambda i: (i, ZC_ML // SUBLANES)),
            pl.BlockSpec((tm, D_MODEL), lambda i: (i, ZC_MA // SUBLANES)),
            pl.BlockSpec((tm, D_MODEL), lambda i: (i, 0)),
            full((D_MODEL, D_MODEL)),
            full((D_MODEL, D_MODEL)),
            full((D_MODEL, D_MODEL)),
            full((1, D_MODEL)),
            full((D_MODEL, LANES)),
            full((1, LANES)),
        ],
        out_specs=[
            pl.BlockSpec((tm, D_MODEL), lambda i: (i, 0)),
            pl.BlockSpec((tm * PACK_CHUNKS, LANES), lambda i: (i, 0)),
            pl.BlockSpec((tm, LANES), lambda i: (i, 0)),
            pl.BlockSpec((SUBLANES, tm), lambda i: (0, i)),
            pl.BlockSpec((1, 1, LANES), lambda i: (i // steps_per_part, 0, 0)),
        ],
        out_shape=[
            jax.ShapeDtypeStruct((T, D_MODEL), F32),
            jax.ShapeDtypeStruct((T * PACK_CHUNKS, LANES), U32),
            jax.ShapeDtypeStruct((T, LANES), F32),
            jax.ShapeDtypeStruct((SUBLANES, T), F32),
            jax.ShapeDtypeStruct((MOE_PARTS, 1, LANES), F32),
        ],
        scratch_shapes=[pltpu.VMEM((1, LANES), F32)],
        compiler_params=_cparams(("arbitrary",)),
        name="merge",
    )(hg, o, z, z, x2, wl, wat, wo, g, wr, br)


def _fill_padding_body(fill_ref, xs_in_ref, xs_ref, zblock, zsem, *, n_blocks):
    del xs_in_ref

    def for_each_padding_piece(fn):
        def per_expert(e, carry):
            start = fill_ref[e]
            n = fill_ref[N_EXPERTS + e]
            size = EXPERT_ROWS // 2
            while size >= 1:
                take = (n & size) != 0

                @pl.when(take)
                def _(start=start, size=size):
                    fn(pltpu.make_async_copy(zblock.at[pl.ds(0, size)], xs_ref.at[pl.ds(start, size)], zsem))

                start = start + jnp.where(take, size, 0)
                size //= 2
            return carry
        lax.fori_loop(0, N_EXPERTS, per_expert, 0)

    def zero_block_copy(blk):
        return pltpu.make_async_copy(zblock, xs_ref.at[pl.ds(blk * EXPERT_ROWS, EXPERT_ROWS)], zsem)

    def for_each_unused_block(fn):
        lax.fori_loop(fill_ref[2 * N_EXPERTS], n_blocks, lambda blk, c: (fn(zero_block_copy(blk)), c)[1], 0)

    zblock[...] = jnp.zeros_like(zblock)
    for_each_padding_piece(lambda cp: cp.start())
    for_each_unused_block(lambda cp: cp.start())
    for_each_padding_piece(lambda cp: cp.wait())
    for_each_unused_block(lambda cp: cp.wait())


def _fill_padding(fill, xs3):
    n_blocks = xs3.shape[0] // EXPERT_ROWS
    grid_spec = pltpu.PrefetchScalarGridSpec(
        num_scalar_prefetch=1,
        grid=(1,),
        in_specs=[pl.BlockSpec(memory_space=pl.ANY)],
        out_specs=pl.BlockSpec(memory_space=pl.ANY),
        scratch_shapes=[pltpu.VMEM((EXPERT_ROWS,) + xs3.shape[1:], xs3.dtype),
                        pltpu.SemaphoreType.DMA(())],
    )
    return pl.pallas_call(
        functools.partial(_fill_padding_body, n_blocks=n_blocks),
        grid_spec=grid_spec,
        out_shape=jax.ShapeDtypeStruct(xs3.shape, xs3.dtype),
        input_output_aliases={1: 0},
        compiler_params=_cparams(("arbitrary",)),
        name="fill_padding",
    )(fill, xs3)


def _expert_body(be_ref, nxt_ref, last_ref, par_ref, nu_ref, x_ref, f1_ref, f3_ref, f2_ref,
                 n1_ref, n3_ref, n2_ref, y_ref, w13b, w2b, *, rows):
    del be_ref, nxt_ref
    i = pl.program_id(0)
    slot = par_ref[i]

    def cast(w1_ref, w3_ref, w2_ref, dst):
        w13b[dst, :, :D_EXPERT] = w1_ref[0].astype(BF16)
        w13b[dst, :, D_EXPERT:] = w3_ref[0].astype(BF16)
        w2b[dst] = w2_ref[0].astype(BF16)

    @pl.when(i == 0)
    def _():
        cast(f1_ref, f3_ref, f2_ref, 0)

    @pl.when(i < nu_ref[0])
    def _():
        xb = _load_packed_rows(x_ref, rows, BF16)
        gu = jnp.dot(xb, w13b[slot], preferred_element_type=F32)
        hb = (jax.nn.silu(gu[:, :D_EXPERT]) * gu[:, D_EXPERT:]).astype(BF16)
        y = jnp.dot(hb, w2b[slot], preferred_element_type=F32)
        _store_packed_rows(y_ref, 0, y)

    @pl.when(i >= nu_ref[0])
    def _():
        y_ref[...] = jnp.zeros_like(y_ref)

    @pl.when(last_ref[i] == 1)
    def _():
        cast(n1_ref, n3_ref, n2_ref, 1 - slot)


def _experts(blk_expert, n_used, xs, w1, w3, w2, *, rows=EXPERT_ROWS):
    n_blocks = xs.shape[0] // (rows * PACK_CHUNKS)
    differs = blk_expert[1:] != blk_expert[:-1]
    last = jnp.concatenate([differs, jnp.zeros((1,), bool)]).astype(jnp.int32)
    parity = jnp.cumsum(jnp.concatenate([jnp.zeros((1,), jnp.int32), differs.astype(jnp.int32)])) % 2
    later = jnp.where(blk_expert[None, :] > blk_expert[:, None], blk_expert[None, :], N_EXPERTS)
    nxt = jnp.min(later, axis=1)
    nxt = jnp.where(nxt == N_EXPERTS, blk_expert, nxt).astype(jnp.int32)

    first = lambda shape: pl.BlockSpec(shape, lambda i, be, nx, la, pa, nu: (be[0], 0, 0),
                                       pipeline_mode=pl.Buffered(1))
    following = lambda shape: pl.BlockSpec(shape, lambda i, be, nx, la, pa, nu: (nx[i], 0, 0))
    grid_spec = pltpu.PrefetchScalarGridSpec(
        num_scalar_prefetch=5,
        grid=(n_blocks,),
        in_specs=[
            pl.BlockSpec((rows * PACK_CHUNKS, LANES),
                         lambda i, be, nx, la, pa, nu: (jnp.minimum(i, nu[0] - 1), 0)),
            first((1, D_MODEL, D_EXPERT)), first((1, D_MODEL, D_EXPERT)), first((1, D_EXPERT, D_MODEL)),
            following((1, D_MODEL, D_EXPERT)), following((1, D_MODEL, D_EXPERT)),
            following((1, D_EXPERT, D_MODEL)),
        ],
        out_specs=pl.BlockSpec((rows * PACK_CHUNKS, LANES), lambda i, be, nx, la, pa, nu: (i, 0)),
        scratch_shapes=[pltpu.VMEM((2, D_MODEL, 2 * D_EXPERT), BF16), pltpu.VMEM((2, D_EXPERT, D_MODEL), BF16)],
    )
    return pl.pallas_call(
        functools.partial(_expert_body, rows=rows),
        grid_spec=grid_spec,
        out_shape=jax.ShapeDtypeStruct(xs.shape, U32),
        compiler_params=_cparams(("arbitrary",)),
        name="experts",
    )(blk_expert, nxt, last, parity.astype(jnp.int32), n_used, xs, w1, w3, w2, w1, w3, w2)


SC_CORES = 2
SC_SUBCORES = 16
SC_WORKERS = SC_CORES * SC_SUBCORES
SC_CHUNK = 64


def _sc_worker_base(per_worker):
    return (lax.axis_index("s") * SC_CORES + lax.axis_index("c")) * per_worker


def _sc_scatter_rows(v3, dests, n_rows, tok0):
    per_worker = dests[0].shape[0] // SC_WORKERS
    chunks = per_worker // SC_CHUNK
    mesh = plsc.VectorSubcoreMesh(core_axis_name="c", subcore_axis_name="s")
    bufs = range(2)
    slab = v3.shape[1:]

    @functools.partial(
        pl.kernel, mesh=mesh,
        out_type=jax.ShapeDtypeStruct((n_rows,) + slab, v3.dtype),
        scratch_types=[pltpu.VMEM((TOP_K, chunks, SC_CHUNK), jnp.int32),
                       pltpu.VMEM((2, SC_CHUNK) + slab, v3.dtype),
                       pltpu.SemaphoreType.DMA((2,)),
                       pltpu.SemaphoreType.DMA((2,))],
        name="sc_scatter")
    def scatter(v_hbm, d0_hbm, d1_hbm, xs_hbm, idx, rows, lsem, ssem):
        base = _sc_worker_base(per_worker)
        for k, d_hbm in enumerate((d0_hbm, d1_hbm)):
            pltpu.sync_copy(d_hbm.at[pl.ds(_sc_worker_base(chunks), chunks)], idx.at[k])

        @pl.loop(0, chunks, step=2)
        def _(j):
            loads = []
            for b in bufs:
                off = base + (j + b) * SC_CHUNK
                loads.append(pltpu.async_copy(v_hbm.at[pl.ds(tok0 + off, SC_CHUNK)], rows.at[b], lsem.at[b]))
            stores = []
            for b in bufs:
                loads[b].wait()
                for k in range(TOP_K):
                    stores.append(pltpu.async_copy(rows.at[b], xs_hbm.at[idx.at[k, j + b]], ssem.at[b]))
            for st in stores:
                st.wait()

    return scatter(v3, *[d.reshape(-1, SC_CHUNK) for d in dests])


def _sc_gather_rows(y3, dests):
    T = dests[0].shape[0]
    per_worker = T // SC_WORKERS
    chunks = per_worker // SC_CHUNK
    mesh = plsc.VectorSubcoreMesh(core_axis_name="c", subcore_axis_name="s")
    bufs = range(2)
    slab = y3.shape[1:]

    @functools.partial(
        pl.kernel, mesh=mesh,
        out_type=jax.ShapeDtypeStruct((TOP_K, T) + slab, y3.dtype),
        scratch_types=[pltpu.VMEM((TOP_K, chunks, SC_CHUNK), jnp.int32),
                       pltpu.VMEM((2, SC_CHUNK) + slab, y3.dtype),
                       pltpu.SemaphoreType.DMA((2,)),
                       pltpu.SemaphoreType.DMA((2,))],
        name="sc_gather")
    def gather(y_hbm, d0_hbm, d1_hbm, out_hbm, idx, rows, gsem, wsem):
        base = _sc_worker_base(per_worker)
        for k, d_hbm in enumerate((d0_hbm, d1_hbm)):
            pltpu.sync_copy(d_hbm.at[pl.ds(_sc_worker_base(chunks), chunks)], idx.at[k])

        for k in range(TOP_K):
            @pl.loop(0, chunks, step=2)
            def _(j):
                gathers = [pltpu.async_copy(y_hbm.at[idx.at[k, j + b]], rows.at[b], gsem.at[b]) for b in bufs]
                writes = []
                for b in bufs:
                    off = base + (j + b) * SC_CHUNK
                    gathers[b].wait()
                    writes.append(pltpu.async_copy(rows.at[b], out_hbm.at[k, pl.ds(off, SC_CHUNK)], wsem.at[b]))
                for w in writes:
                    w.wait()

    return gather(y3, *[d.reshape(-1, SC_CHUNK) for d in dests])


def _combine_body(yk_ref, x1_ref, ri_ref, g_ref, *rest, tc, final_norm):
    out_ref = rest[-1]

    ri = ri_ref[...]
    w0 = ri[:, 2:3]
    w1 = ri[:, 3:4]
    x2 = x1_ref[...] + (w0 * _load_packed_rows(yk_ref.at[0], tc, F32) + w1 * _load_packed_rows(yk_ref.at[1], tc, F32))
    if final_norm:
        ms = jnp.mean(x2 * x2, axis=-1, keepdims=True)
        x2 = x2 * lax.rsqrt(ms + EPS) * g_ref[...]
    out_ref[...] = x2


def _combine(yk, x1, ri, g, out_prev, *, tok0, final_norm, tc=512):
    T = x1.shape[0]
    steps = yk.shape[1] // (tc * PACK_CHUNKS)
    blk0 = tok0 // tc
    in_specs = [
        pl.BlockSpec((TOP_K, tc * PACK_CHUNKS, LANES), lambda i: (0, i, 0)),
        pl.BlockSpec((tc, D_MODEL), lambda i: (i + blk0, 0)),
        pl.BlockSpec((tc, LANES), lambda i: (i + blk0, 0)),
        pl.BlockSpec((1, D_MODEL), lambda i: (0, 0)),
    ]
    args = [yk, x1, ri, g]
    aliases = {}
    if out_prev is not None:
        in_specs.append(pl.BlockSpec(memory_space=pl.ANY))
        args.append(out_prev)
        aliases = {len(args) - 1: 0}
    return pl.pallas_call(
        functools.partial(_combine_body, tc=tc, final_norm=final_norm),
        grid=(steps,),
        in_specs=in_specs,
        out_specs=pl.BlockSpec((tc, D_MODEL), lambda i: (i + blk0, 0)),
        out_shape=jax.ShapeDtypeStruct((T, D_MODEL), F32),
        input_output_aliases=aliases,
        compiler_params=_cparams(("arbitrary",)),
        name="combine",
    )(*args)


def _pad_lanes(a, width=LANES):
    return jnp.pad(a, ((0, 0), (0, width - a.shape[-1])))


def _layer(x2, batch, seq, g_mix, w_in, conv_w, conv_b, w_rg_a, b_rg_a, w_rg_x, b_rg_x, lam, b_forget,
           w_lru_out, w_attn_out, w_out, g_ffn, w_rgrp, b_rgrp, w_rexp, b_rexp, w1, w3, w2, g_out, final_norm):
    T = batch * seq
    row = lambda a: a.reshape(1, -1).astype(F32)

    fl0 = 5 * D_MODEL
    w_main = jnp.concatenate([w_in[:, :fl0], w_in[:, fl0 + N_HEADS:]], axis=1).astype(BF16)
    w_fl = _pad_lanes(w_in[:, fl0:fl0 + N_HEADS]).astype(BF16)
    z, fl = _inproj(x2, row(g_mix), w_main, w_fl)

    caug = _forget(fl, _pad_lanes(row(b_forget)), batch=batch, seq=seq)
    o = _attention(z, caug, batch=batch, seq=seq)

    hg = _lru(z, conv_w.astype(F32), row(conv_b), w_rg_a.astype(BF16), w_rg_x.astype(BF16),
              row(b_rg_a), row(b_rg_x), row(lam), batch=batch, seq=seq)

    wr = _pad_lanes(jnp.concatenate([w_rgrp, w_rexp], axis=1)).astype(BF16)
    br = _pad_lanes(jnp.concatenate([row(b_rgrp), row(b_rexp)], axis=1))
    x1, v, ri, rt, cnt = _merge(hg, o, z, x2, w_lru_out.astype(BF16), w_attn_out.astype(BF16),
                            w_out.astype(BF16), row(g_ffn), wr, br)

    v3 = v.reshape(T, PACK_CHUNKS, LANES)
    g_out = row(g_out)
    Tp = T // MOE_PARTS
    n_blocks = (Tp * TOP_K + N_EXPERTS * (EXPERT_ROWS - 1) + EXPERT_ROWS - 1) // EXPERT_ROWS
    blk_start = jnp.arange(n_blocks, dtype=jnp.int32) * EXPERT_ROWS
    out = None
    for part in range(MOE_PARTS):
        tok0 = part * Tp
        e = rt[0:TOP_K, tok0:tok0 + Tp].astype(jnp.int32)
        rank = rt[4:4 + TOP_K, tok0:tok0 + Tp].astype(jnp.int32)
        counts = cnt[part, 0, R_EXP0:R_EXP0 + N_EXPERTS].astype(jnp.int32)
        padded = (counts + EXPERT_ROWS - 1) // EXPERT_ROWS * EXPERT_ROWS
        pad_end = jnp.cumsum(padded)
        pad_start = pad_end - padded
        seg_start = jnp.zeros_like(e)
        for j in range(N_EXPERTS):
            seg_start = jnp.where(e == j, pad_start[j], seg_start)
        dest2 = (seg_start + rank).astype(jnp.int32)
        dests = [dest2[k] for k in range(TOP_K)]
        blk_expert = jnp.minimum(
            jnp.sum((pad_end[None, :] <= blk_start[:, None]).astype(jnp.int32), axis=1), N_EXPERTS - 1)
        n_used = (pad_end[-1:] // EXPERT_ROWS).astype(jnp.int32)
        fill = jnp.concatenate([pad_start + counts, padded - counts, n_used]).astype(jnp.int32)

        xs = _sc_scatter_rows(v3, dests, n_blocks * EXPERT_ROWS, tok0)
        xs = _fill_padding(fill, xs)
        y = _experts(blk_expert, n_used, xs.reshape(-1, LANES), w1, w3, w2)
        y3 = y.reshape(-1, PACK_CHUNKS, LANES)
        Tc = Tp // COMBINE_PARTS
        for sub in range(COMBINE_PARTS):
            yk = _sc_gather_rows(y3, [d[sub * Tc:(sub + 1) * Tc] for d in dests])
            out = _combine(yk.reshape(TOP_K, Tc * PACK_CHUNKS, LANES), x1, ri, g_out, out,
                           tok0=tok0 + sub * Tc, final_norm=final_norm)
    return out


def kernel(x, g_mix, w_in, conv_w, conv_b, w_rg_a, b_rg_a, w_rg_x, b_rg_x, lru_lambda, b_forget, w_lru_out, w_attn_out, w_out, g_ffn, w_route_group, b_route_group, w_route_expert, b_route_expert, w_exp_gate, w_exp_up, w_exp_down, g_final):
    batch, seq, _ = x.shape
    depth = g_mix.shape[0]
    x2 = x.reshape(batch * seq, D_MODEL)
    for l in range(depth):
        x2 = _layer(
            x2, batch, seq, g_mix[l], w_in[l], conv_w[l], conv_b[l], w_rg_a[l], b_rg_a[l], w_rg_x[l],
            b_rg_x[l], lru_lambda[l], b_forget[l], w_lru_out[l], w_attn_out[l], w_out[l], g_ffn[l],
            w_route_group[l], b_route_group[l], w_route_expert[l], b_route_expert[l],
            w_exp_gate[l], w_exp_up[l], w_exp_down[l], g_final, l == depth - 1)
    return x2.reshape(batch, seq, D_MODEL)
```

```python
import functools

import jax
import jax.numpy as jnp
from jax import lax
from jax.experimental import pallas as pl
from jax.experimental.pallas import tpu as pltpu
from jax.experimental.pallas import tpu_sc as plsc

F32 = jnp.float32
BF16 = jnp.bfloat16

D_MODEL = 1024
LRU_BLOCK_W = 256
LRU_BLOCKS = D_MODEL // LRU_BLOCK_W
CONV_W = 4
LRU_C = 8.0
N_HEADS = 8
HEAD_DIM = D_MODEL // N_HEADS
N_GROUPS = 4
EXPERTS_PER_GROUP = 8
N_EXPERTS = N_GROUPS * EXPERTS_PER_GROUP
TOP_K = 2
D_EXPERT = D_MODEL // 2
EPS = 1e-6

LANES = 128
SUBLANES = 8
PACK_CHUNKS = D_MODEL // LANES // 2
U32 = jnp.uint32
VMEM_LIMIT = 48 * 1024 * 1024

ZC_XR, ZC_GR, ZC_Q, ZC_K, ZC_V, ZC_ML, ZC_MA = 0, 8, 16, 24, 32, 40, 48
Z_WIDTH = 7 * D_MODEL

R_EXP0 = N_GROUPS

EXPERT_ROWS = 512


def _cparams(sem):
    return pltpu.CompilerParams(dimension_semantics=sem, vmem_limit_bytes=VMEM_LIMIT)


def _store_packed_rows(ref, first_row, x):
    n = x.shape[0]
    bits = lax.bitcast_convert_type(x.astype(BF16).astype(F32), U32)
    half = PACK_CHUNKS * LANES
    for s in range(PACK_CHUNKS):
        word = (bits[:, s * LANES:(s + 1) * LANES] >> 16) | bits[:, half + s * LANES:half + (s + 1) * LANES]
        ref[pl.ds(first_row + s, n, stride=PACK_CHUNKS), :] = word


def _load_packed_rows(ref, n, dtype):
    words = [ref[pl.ds(s, n, stride=PACK_CHUNKS), :] for s in range(PACK_CHUNKS)]
    low = [lax.bitcast_convert_type(w << 16, F32).astype(dtype) for w in words]
    high = [lax.bitcast_convert_type(w & jnp.uint32(0xFFFF0000), F32).astype(dtype) for w in words]
    return jnp.concatenate(low + high, axis=-1)


LOG2E = 1.4426950408889634
Q_PRESCALE = HEAD_DIM ** -0.5 * LOG2E


def _inproj_body(x_ref, g_ref, w_ref, wfl_ref, z_ref, fl_ref, u_ref, *, q_block):
    j = pl.program_id(1)

    @pl.when(j == 0)
    def _():
        x = x_ref[...]
        ms = jnp.mean(x * x, axis=-1, keepdims=True)
        u = (x * lax.rsqrt(ms + EPS) * g_ref[...]).astype(BF16)
        u_ref[...] = u
        fl_ref[...] = jnp.dot(u, wfl_ref[...], preferred_element_type=F32)

    @pl.when(j == q_block)
    def _():
        acc = jnp.dot(u_ref[...], w_ref[...], preferred_element_type=F32)
        z_ref[...] = (acc * Q_PRESCALE).astype(BF16)

    @pl.when(j != q_block)
    def _():
        z_ref[...] = jnp.dot(u_ref[...], w_ref[...], preferred_element_type=F32).astype(BF16)


def _inproj(x2, g, w_main, w_fl, *, tm=2048, tn=D_MODEL):
    T = x2.shape[0]
    return pl.pallas_call(
        functools.partial(_inproj_body, q_block=ZC_Q * LANES // tn),
        grid=(T // tm, Z_WIDTH // tn),
        in_specs=[
            pl.BlockSpec((tm, D_MODEL), lambda i, j: (i, 0)),
            pl.BlockSpec((1, D_MODEL), lambda i, j: (0, 0)),
            pl.BlockSpec((D_MODEL, tn), lambda i, j: (0, j)),
            pl.BlockSpec((D_MODEL, LANES), lambda i, j: (0, 0)),
        ],
        out_specs=[
            pl.BlockSpec((tm, tn), lambda i, j: (i, j)),
            pl.BlockSpec((tm, LANES), lambda i, j: (i, 0)),
        ],
        out_shape=[
            jax.ShapeDtypeStruct((T, Z_WIDTH), BF16),
            jax.ShapeDtypeStruct((T, LANES), F32),
        ],
        scratch_shapes=[pltpu.VMEM((tm, D_MODEL), BF16)],
        compiler_params=_cparams(("arbitrary", "arbitrary")),
        name="inproj",
    )(x2, g, w_main, w_fl)


C_TERMS = 3


def _log_sigmoid(z):
    return jnp.minimum(z, 0.0) - jnp.log1p(jnp.exp(-jnp.abs(z)))


def _forget_body(fl_ref, b_ref, c_ref, *, seq):
    lf = _log_sigmoid(fl_ref[...] + b_ref[...])
    row = lax.broadcasted_iota(jnp.int32, lf.shape, 0)
    lane = lax.broadcasted_iota(jnp.int32, lf.shape, 1)
    c = lf
    k = 1
    while k < seq:
        c = c + jnp.where(row >= k, pltpu.roll(c, k, axis=0), 0.0)
        k *= 2
    rem = c * LOG2E
    out = jnp.zeros(lf.shape, F32)
    for n in range(C_TERMS):
        t = rem.astype(BF16).astype(F32)
        rem = rem - t
        shifted = t if n == 0 else pltpu.roll(t, n * N_HEADS, axis=1)
        out = jnp.where((lane >= n * N_HEADS) & (lane < (n + 1) * N_HEADS), shifted, out)
    c_ref[...] = out.astype(BF16)


def _forget(fl, b_pad, *, batch, seq):
    return pl.pallas_call(
        functools.partial(_forget_body, seq=seq),
        grid=(batch,),
        in_specs=[
            pl.BlockSpec((seq, LANES), lambda b: (b, 0)),
            pl.BlockSpec((1, LANES), lambda b: (0, 0)),
        ],
        out_specs=pl.BlockSpec((seq, LANES), lambda b: (b, 0)),
        out_shape=jax.ShapeDtypeStruct((batch * seq, LANES), BF16),
        compiler_params=_cparams(("arbitrary",)),
        name="forget",
    )(fl, b_pad)


ATTN_HEADS_PER_STEP = 4


def _attn_body(q_ref, k_ref, v_ref, kc_ref, o_ref, *, tq):
    i = pl.program_id(2)
    lane = lax.broadcasted_iota(jnp.int32, (tq, LANES), 1)
    heads = range(ATTN_HEADS_PER_STEP)
    cols =[slice(g * HEAD_DIM, (g + 1) * HEAD_DIM) for g in heads]
    qs = []
    for g in heads:
        h = pl.program_id(1) * ATTN_HEADS_PER_STEP + g
        mine = (lane < C_TERMS * N_HEADS) & ((lane & (N_HEADS - 1)) == h)
        qc = jnp.where(mine, -1.0, 0.0).astype(BF16)
        qs.append(jnp.concatenate([q_ref[:, cols[g]], qc], axis=1))

    def step(blocks, carry, masked):
        offs = [pl.multiple_of(j * tq, tq) for j in blocks]
        out = []
        for g in heads:
            m, l, acc = carry[g]
            kj = jnp.concatenate(
                [jnp.concatenate([k_ref[pl.ds(off, tq), cols[g]], kc_ref[pl.ds(off, tq), :]], axis=1)
                 for off in offs], axis=0)
            vj = jnp.concatenate([v_ref[pl.ds(off, tq), cols[g]] for off in offs], axis=0)
            s = lax.dot_general(qs[g], kj, (((1,), (1,)), ((), ())), preferred_element_type=F32)
            if masked:
                r = lax.broadcasted_iota(jnp.int32, s.shape, 0)
                cidx = lax.broadcasted_iota(jnp.int32, s.shape, 1) - (len(blocks) - 1) * tq
                s = jnp.where(cidx <= r, s, -jnp.inf)
            m_new = jnp.maximum(m, jnp.max(s, axis=-1, keepdims=True))
            p = jnp.exp2(s - m_new)
            alpha = jnp.exp2(m - m_new)
            l = alpha * l + jnp.sum(p, axis=-1, keepdims=True)
            acc = alpha * acc + jnp.dot(p.astype(BF16), vj, preferred_element_type=F32)
            out.append((m_new, l, acc))
        return tuple(out)

    init = tuple((jnp.full((tq, 1), -jnp.inf, F32), jnp.zeros((tq, 1), F32), jnp.zeros((tq, HEAD_DIM), F32))
                 for _ in heads)
    carry = lax.fori_loop(0, i // 2, lambda j, c: step([2 * j, 2 * j + 1], c, False), init)
    carry = lax.cond(i % 2 == 1,
                     lambda c: step([i - 1, i], c, True),
                     lambda c: step([i], c, True), carry)
    for g in heads:
        _, l, acc = carry[g]
        o_ref[:, cols[g]] = (acc / l).astype(BF16)


def _attention(z, caug, *, batch, seq, tq=512):
    nq = seq // tq
    T = batch * seq
    G = ATTN_HEADS_PER_STEP
    W = G * HEAD_DIM
    return pl.pallas_call(
        functools.partial(_attn_body, tq=tq),
        grid=(batch, N_HEADS // G, nq),
        in_specs=[
            pl.BlockSpec((tq, W), lambda b, h, i: (b * nq + i, ZC_Q // G + h)),
            pl.BlockSpec((seq, W), lambda b, h, i: (b, ZC_K // G + h)),
            pl.BlockSpec((seq, W), lambda b, h, i: (b, ZC_V // G + h)),
            pl.BlockSpec((seq, LANES), lambda b, h, i: (b, 0)),
        ],
        out_specs=pl.BlockSpec((tq, W), lambda b, h, i: (b * nq + i, h)),
        out_shape=jax.ShapeDtypeStruct((T, D_MODEL), BF16),
        compiler_params=_cparams(("arbitrary", "arbitrary", "arbitrary")),
        name="attn",
    )(z, z, z, caug)


def _lru_body(xr_ref, gr_ref, cw_ref, cb_ref, wa_ref, wx_ref, ba_ref, bx_ref, lam_ref,
              hg_ref, xbuf, hcar, *, ts):
    i = pl.program_id(1)

    @pl.when(i == 0)
    def _():
        xbuf[0:SUBLANES, :] = jnp.zeros((SUBLANES, D_MODEL), F32)
        hcar[...] = jnp.zeros_like(hcar)

    @pl.when(i > 0)
    def _():
        xbuf[0:SUBLANES, :] = xbuf[ts:ts + SUBLANES, :]

    xbuf[SUBLANES:ts + SUBLANES, :] = xr_ref[...].astype(F32)

    base = SUBLANES - (CONV_W - 1)
    xc = cw_ref[0:1, :] * xbuf[base:base + ts, :]
    for k in range(1, CONV_W):
        xc = xc + cw_ref[k:k + 1, :] * xbuf[base + k:base + k + ts, :]
    xc = xc + cb_ref[...]

    xcb = xc.astype(BF16)
    ra = jnp.concatenate(
        [jnp.dot(xcb[:, n * LRU_BLOCK_W:(n + 1) * LRU_BLOCK_W], wa_ref[n], preferred_element_type=F32)
         for n in range(LRU_BLOCKS)], axis=-1)
    rx = jnp.concatenate(
        [jnp.dot(xcb[:, n * LRU_BLOCK_W:(n + 1) * LRU_BLOCK_W], wx_ref[n], preferred_element_type=F32)
         for n in range(LRU_BLOCKS)], axis=-1)
    r = jax.nn.sigmoid(ra + ba_ref[...])
    ig = jax.nn.sigmoid(rx + bx_ref[...])
    nlam = -lam_ref[...]
    softplus = jnp.maximum(nlam, 0.0) + jnp.log1p(jnp.exp(-jnp.abs(nlam)))
    log_a = (-LRU_C * r) * softplus
    a = jnp.exp(log_a)
    th = jnp.tanh(log_a)
    mult = jnp.sqrt(-2.0 * th / (1.0 - th))
    b = mult * ig * xc

    row = lax.broadcasted_iota(jnp.int32, (SUBLANES, D_MODEL), 0)
    keeps = [(k, row >= k) for k in (1, 2, 4)]
    hprev = jnp.broadcast_to(hcar[...], (SUBLANES, D_MODEL))
    pieces = []
    for j in range(ts // SUBLANES):
        aj = a[j * SUBLANES:(j + 1) * SUBLANES, :]
        bj = b[j * SUBLANES:(j + 1) * SUBLANES, :]
        for k, keep in keeps:
            a_sh = jnp.where(keep, pltpu.roll(aj, k, axis=0), 1.0)
            b_sh = jnp.where(keep, pltpu.roll(bj, k, axis=0), 0.0)
            bj = aj * b_sh + bj
            aj = aj * a_sh
        hj = bj + aj * hprev
        hprev = jnp.broadcast_to(hj[SUBLANES - 1:SUBLANES, :], (SUBLANES, D_MODEL))
        pieces.append(hj)
    h = jnp.concatenate(pieces, axis=0)
    hcar[...] = h[ts - 1:ts, :]

    hg_ref[...] = (h * jax.nn.gelu(gr_ref[...].astype(F32))).astype(BF16)


def _lru(z, conv_w, conv_b, wa, wx, ba, bx, lam, *, batch, seq, ts=512):
    ns = seq // ts
    T = batch * seq
    full = lambda shape: pl.BlockSpec(shape, lambda b, i: (0,) * len(shape))
    return pl.pallas_call(
        functools.partial(_lru_body, ts=ts),
        grid=(batch, ns),
        in_specs=[
            pl.BlockSpec((ts, D_MODEL), lambda b, i: (b * ns + i, ZC_XR // SUBLANES)),
            pl.BlockSpec((ts, D_MODEL), lambda b, i: (b * ns + i, ZC_GR // SUBLANES)),
            full((CONV_W, D_MODEL)),
            full((1, D_MODEL)),
            full((LRU_BLOCKS, LRU_BLOCK_W, LRU_BLOCK_W)),
            full((LRU_BLOCKS, LRU_BLOCK_W, LRU_BLOCK_W)),
            full((1, D_MODEL)),
            full((1, D_MODEL)),
            full((1, D_MODEL)),
        ],
        out_specs=pl.BlockSpec((ts, D_MODEL), lambda b, i: (b * ns + i, 0)),
        out_shape=jax.ShapeDtypeStruct((T, D_MODEL), BF16),
        scratch_shapes=[pltpu.VMEM((ts + SUBLANES, D_MODEL), F32), pltpu.VMEM((1, D_MODEL), F32)],
        compiler_params=_cparams(("arbitrary", "arbitrary")),
        name="lru",
    )(z, z, conv_w, conv_b, wa, wx, ba, bx, lam)


MOE_PARTS = 1
COMBINE_PARTS = 2


def _merge_body(hg_ref, o_ref, ml_ref, ma_ref, x_ref, wl_ref, wat_ref, wo_ref, g_ref, wr_ref, br_ref,
                x1_ref, v_ref, ri_ref, rt_ref, cnt_ref, carry_ref, *, tm, steps_per_part):
    @pl.when(pl.program_id(0) % steps_per_part == 0)
    def _():
        carry_ref[...] = jnp.zeros_like(carry_ref)

    ts = tm
    lane = lax.broadcasted_iota(jnp.int32, (ts, LANES), 1)
    rr = lax.broadcasted_iota(jnp.int32, (ts, ts), 0)
    cc = lax.broadcasted_iota(jnp.int32, (ts, ts), 1)
    tri = jnp.where(cc < rr, 1.0, 0.0).astype(BF16)
    ninf = -jnp.inf
    big = jnp.int32(1 << 20)
    carry = carry_ref[...]

    def tile(sub, carry):
        rows = slice(sub * ts, (sub + 1) * ts)
        yl = jnp.dot(hg_ref[rows, :], wl_ref[...], preferred_element_type=F32)
        ya = jnp.dot(o_ref[rows, :], wat_ref[...], preferred_element_type=F32)
        merged = (jax.nn.sigmoid(ml_ref[rows, :].astype(F32)) * yl
                  + jax.nn.sigmoid(ma_ref[rows, :].astype(F32)) * ya)
        x1 = x_ref[rows, :] + jnp.dot(merged.astype(BF16), wo_ref[...], preferred_element_type=F32)
        x1_ref[rows, :] = x1
        ms = jnp.mean(x1 * x1, axis=-1, keepdims=True)
        v = x1 * lax.rsqrt(ms + EPS) * g_ref[...]
        _store_packed_rows(v_ref, sub * ts * PACK_CHUNKS, v)

        logits = jnp.dot(v.astype(BF16), wr_ref[...], preferred_element_type=F32) + br_ref[...]

        gl = jnp.where(lane < N_GROUPS, logits, ninf)
        gmax = jnp.max(gl, axis=-1, keepdims=True)
        gsel = jnp.min(jnp.where(gl == gmax, lane, big), axis=-1, keepdims=True)
        pg = 1.0 / jnp.sum(jnp.exp(gl - gmax), axis=-1, keepdims=True)

        lo = R_EXP0 + gsel * EXPERTS_PER_GROUP
        el = jnp.where(lane >= lo, jnp.where(lane < lo + EXPERTS_PER_GROUP, logits, ninf), ninf)
        v1 = jnp.max(el, axis=-1, keepdims=True)
        i1 = jnp.min(jnp.where(el == v1, lane, big), axis=-1, keepdims=True)
        el2 = jnp.where(lane == i1, ninf, el)
        v2 = jnp.max(el2, axis=-1, keepdims=True)
        i2 = jnp.min(jnp.where(el2 == v2, lane, big), axis=-1, keepdims=True)
        e21 = jnp.exp(v2 - v1)
        p1 = 1.0 / (1.0 + e21)
        w0 = pg * p1
        w1 = pg * (e21 * p1)

        hit0 = lane == i1
        hit1 = lane == i2
        onehot = jnp.where(hit0, 1.0, jnp.where(hit1, 1.0, 0.0))
        before = jnp.dot(tri, onehot.astype(BF16), preferred_element_type=F32) + carry
        rank0 = jnp.sum(jnp.where(hit0, before, 0.0), axis=-1, keepdims=True)
        rank1 = jnp.sum(jnp.where(hit1, before, 0.0), axis=-1, keepdims=True)
        carry = carry + jnp.sum(onehot, axis=0, keepdims=True)

        e0 = (i1 - R_EXP0).astype(F32)
        e1 = (i2 - R_EXP0).astype(F32)
        ri = jnp.where(lane == 0, e0,
             jnp.where(lane == 1, e1,
             jnp.where(lane == 2, w0,
             jnp.where(lane == 3, w1,
             jnp.where(lane == 4, rank0,
             jnp.where(lane == 5, rank1, 0.0))))))
        ri_ref[rows, :] = ri
        rt_ref[:, rows] = ri.T[:SUBLANES, :]
        return carry

    carry = tile(0, carry)
    carry_ref[...] = carry
    cnt_ref[0] = carry


def _merge(hg, o, z, x2, wl, wat, wo, g, wr, br, *, tm=512):
    T = x2.shape[0]
    steps_per_part = T // tm // MOE_PARTS
    full = lambda shape: pl.BlockSpec(shape, lambda i: (0,) * len(shape))
    return pl.pallas_call(
        functools.partial(_merge_body, tm=tm, steps_per_part=steps_per_part),
        grid=(T // tm,),
        in_specs=[
            pl.BlockSpec((tm, D_MODEL), lambda i: (i, 0)),
            pl.BlockSpec((tm, D_MODEL), lambda i: (i, 0)),
            pl.BlockSpec((tm, D_MODEL), lambda i: (i, ZC_ML // SUBLANES)),
            pl.BlockSpec((tm, D_MODEL), lambda i: (i, ZC_MA // SUBLANES)),
            pl.BlockSpec((tm, D_MODEL), lambda i: (i, 0)),
            full((D_MODEL, D_MODEL)),
            full((D_MODEL, D_MODEL)),
            full((D_MODEL, D_MODEL)),
            full((1, D_MODEL)),
            full((D_MODEL, LANES)),
            full((1, LANES)),
        ],
        out_specs=[
            pl.BlockSpec((tm, D_MODEL), lambda i: (i, 0)),
            pl.BlockSpec((tm * PACK_CHUNKS, LANES), lambda i: (i, 0)),
            pl.BlockSpec((tm, LANES), lambda i: (i, 0)),
            pl.BlockSpec((SUBLANES, tm), lambda i: (0, i)),
            pl.BlockSpec((1, 1, LANES), lambda i: (i // steps_per_part, 0, 0)),
        ],
        out_shape=[
            jax.ShapeDtypeStruct((T, D_MODEL), F32),
            jax.ShapeDtypeStruct((T * PACK_CHUNKS, LANES), U32),
            jax.ShapeDtypeStruct((T, LANES), F32),
            jax.ShapeDtypeStruct((SUBLANES, T), F32),
            jax.ShapeDtypeStruct((MOE_PARTS, 1, LANES), F32),
        ],
        scratch_shapes=[pltpu.VMEM((1, LANES), F32)],
        compiler_params=_cparams(("arbitrary",)),
        name="merge",
    )(hg, o, z, z, x2, wl, wat, wo, g, wr, br)


def _fill_padding_body(fill_ref, xs_in_ref, xs_ref, zblock, zsem, *, n_blocks):
    del xs_in_ref

    def for_each_padding_piece(fn):
        def per_expert(e, carry):
            start = fill_ref[e]
            n = fill_ref[N_EXPERTS + e]
            size = EXPERT_ROWS // 2
            while size >= 1:
                take = (n & size) != 0

                @pl.when(take)
                def _(start=start, size=size):
                    fn(pltpu.make_async_copy(zblock.at[pl.ds(0, size)], xs_ref.at[pl.ds(start, size)], zsem))

                start = start + jnp.where(take, size, 0)
                size //= 2
            return carry
        lax.fori_loop(0, N_EXPERTS, per_expert, 0)

    def zero_block_copy(blk):
        return pltpu.make_async_copy(zblock, xs_ref.at[pl.ds(blk * EXPERT_ROWS, EXPERT_ROWS)], zsem)

    def for_each_unused_block(fn):
        lax.fori_loop(fill_ref[2 * N_EXPERTS], n_blocks, lambda blk, c: (fn(zero_block_copy(blk)), c)[1], 0)

    zblock[...] = jnp.zeros_like(zblock)
    for_each_padding_piece(lambda cp: cp.start())
    for_each_unused_block(lambda cp: cp.start())
    for_each_padding_piece(lambda cp: cp.wait())
    for_each_unused_block(lambda cp: cp.wait())


def _fill_padding(fill, xs3):
    n_blocks = xs3.shape[0] // EXPERT_ROWS
    grid_spec = pltpu.PrefetchScalarGridSpec(
        num_scalar_prefetch=1,
        grid=(1,),
        in_specs=[pl.BlockSpec(memory_space=pl.ANY)],
        out_specs=pl.BlockSpec(memory_space=pl.ANY),
        scratch_shapes=[pltpu.VMEM((EXPERT_ROWS,) + xs3.shape[1:], xs3.dtype),
                        pltpu.SemaphoreType.DMA(())],
    )
    return pl.pallas_call(
        functools.partial(_fill_padding_body, n_blocks=n_blocks),
        grid_spec=grid_spec,
        out_shape=jax.ShapeDtypeStruct(xs3.shape, xs3.dtype),
        input_output_aliases={1: 0},
        compiler_params=_cparams(("arbitrary",)),
        name="fill_padding",
    )(fill, xs3)


def _expert_body(be_ref, nxt_ref, last_ref, par_ref, nu_ref, x_ref, f1_ref, f3_ref, f2_ref,
                 n1_ref, n3_ref, n2_ref, y_ref, w13b, w2b, *, rows):
    del be_ref, nxt_ref
    i = pl.program_id(0)
    slot = par_ref[i]

    def cast(w1_ref, w3_ref, w2_ref, dst):
        w13b[dst, :, :D_EXPERT] = w1_ref[0].astype(BF16)
        w13b[dst, :, D_EXPERT:] = w3_ref[0].astype(BF16)
        w2b[dst] = w2_ref[0].astype(BF16)

    @pl.when(i == 0)
    def _():
        cast(f1_ref, f3_ref, f2_ref, 0)

    @pl.when(i < nu_ref[0])
    def _():
        xb = _load_packed_rows(x_ref, rows, BF16)
        gu = jnp.dot(xb, w13b[slot], preferred_element_type=F32)
        hb = (jax.nn.silu(gu[:, :D_EXPERT]) * gu[:, D_EXPERT:]).astype(BF16)
        y = jnp.dot(hb, w2b[slot], preferred_element_type=F32)
        _store_packed_rows(y_ref, 0, y)

    @pl.when(i >= nu_ref[0])
    def _():
        y_ref[...] = jnp.zeros_like(y_ref)

    @pl.when(last_ref[i] == 1)
    def _():
        cast(n1_ref, n3_ref, n2_ref, 1 - slot)


def _experts(blk_expert, n_used, xs, w1, w3, w2, *, rows=EXPERT_ROWS):
    n_blocks = xs.shape[0] // (rows * PACK_CHUNKS)
    differs = blk_expert[1:] != blk_expert[:-1]
    last = jnp.concatenate([differs, jnp.zeros((1,), bool)]).astype(jnp.int32)
    parity = jnp.cumsum(jnp.concatenate([jnp.zeros((1,), jnp.int32), differs.astype(jnp.int32)])) % 2
    later = jnp.where(blk_expert[None, :] > blk_expert[:, None], blk_expert[None, :], N_EXPERTS)
    nxt = jnp.min(later, axis=1)
    nxt = jnp.where(nxt == N_EXPERTS, blk_expert, nxt).astype(jnp.int32)

    first = lambda shape: pl.BlockSpec(shape, lambda i, be, nx, la, pa, nu: (be[0], 0, 0),
                                       pipeline_mode=pl.Buffered(1))
    following = lambda shape: pl.BlockSpec(shape, lambda i, be, nx, la, pa, nu: (nx[i], 0, 0))
    grid_spec = pltpu.PrefetchScalarGridSpec(
        num_scalar_prefetch=5,
        grid=(n_blocks,),
        in_specs=[
            pl.BlockSpec((rows * PACK_CHUNKS, LANES),
                         lambda i, be, nx, la, pa, nu: (jnp.minimum(i, nu[0] - 1), 0)),
            first((1, D_MODEL, D_EXPERT)), first((1, D_MODEL, D_EXPERT)), first((1, D_EXPERT, D_MODEL)),
            following((1, D_MODEL, D_EXPERT)), following((1, D_MODEL, D_EXPERT)),
            following((1, D_EXPERT, D_MODEL)),
        ],
        out_specs=pl.BlockSpec((rows * PACK_CHUNKS, LANES), lambda i, be, nx, la, pa, nu: (i, 0)),
        scratch_shapes=[pltpu.VMEM((2, D_MODEL, 2 * D_EXPERT), BF16), pltpu.VMEM((2, D_EXPERT, D_MODEL), BF16)],
    )
    return pl.pallas_call(
        functools.partial(_expert_body, rows=rows),
        grid_spec=grid_spec,
        out_shape=jax.ShapeDtypeStruct(xs.shape, U32),
        compiler_params=_cparams(("arbitrary",)),
        name="experts",
    )(blk_expert, nxt, last, parity.astype(jnp.int32), n_used, xs, w1, w3, w2, w1, w3, w2)


SC_CORES = 2
SC_SUBCORES = 16
SC_WORKERS = SC_CORES * SC_SUBCORES
SC_CHUNK = 64


def _sc_worker_base(per_worker):
    return (lax.axis_index("s") * SC_CORES + lax.axis_index("c")) * per_worker


def _sc_scatter_rows(v3, dests, n_rows, tok0):
    per_worker = dests[0].shape[0] // SC_WORKERS
    chunks = per_worker // SC_CHUNK
    mesh = plsc.VectorSubcoreMesh(core_axis_name="c", subcore_axis_name="s")
    bufs = range(2)
    slab = v3.shape[1:]

    @functools.partial(
        pl.kernel, mesh=mesh,
        out_type=jax.ShapeDtypeStruct((n_rows,) + slab, v3.dtype),
        scratch_types=[pltpu.VMEM((TOP_K, chunks, SC_CHUNK), jnp.int32),
                       pltpu.VMEM((2, SC_CHUNK) + slab, v3.dtype),
                       pltpu.SemaphoreType.DMA((2,)),
                       pltpu.SemaphoreType.DMA((2,))],
        name="sc_scatter")
    def scatter(v_hbm, d0_hbm, d1_hbm, xs_hbm, idx, rows, lsem, ssem):
        base = _sc_worker_base(per_worker)
        for k, d_hbm in enumerate((d0_hbm, d1_hbm)):
            pltpu.sync_copy(d_hbm.at[pl.ds(_sc_worker_base(chunks), chunks)], idx.at[k])

        @pl.loop(0, chunks, step=2)
        def _(j):
            loads = []
            for b in bufs:
                off = base + (j + b) * SC_CHUNK
                loads.append(pltpu.async_copy(v_hbm.at[pl.ds(tok0 + off, SC_CHUNK)], rows.at[b], lsem.at[b]))
            stores = []
            for b in bufs:
                loads[b].wait()
                for k in range(TOP_K):
                    stores.append(pltpu.async_copy(rows.at[b], xs_hbm.at[idx.at[k, j + b]], ssem.at[b]))
            for st in stores:
                st.wait()

    return scatter(v3, *[d.reshape(-1, SC_CHUNK) for d in dests])


def _sc_gather_rows(y3, dests):
    T = dests[0].shape[0]
    per_worker = T // SC_WORKERS
    chunks = per_worker // SC_CHUNK
    mesh = plsc.VectorSubcoreMesh(core_axis_name="c", subcore_axis_name="s")
    bufs = range(2)
    slab = y3.shape[1:]

    @functools.partial(
        pl.kernel, mesh=mesh,
        out_type=jax.ShapeDtypeStruct((TOP_K, T) + slab, y3.dtype),
        scratch_types=[pltpu.VMEM((TOP_K, chunks, SC_CHUNK), jnp.int32),
                       pltpu.VMEM((2, SC_CHUNK) + slab, y3.dtype),
                       pltpu.SemaphoreType.DMA((2,)),
                       pltpu.SemaphoreType.DMA((2,))],
        name="sc_gather")
    def gather(y_hbm, d0_hbm, d1_hbm, out_hbm, idx, rows, gsem, wsem):
        base = _sc_worker_base(per_worker)
        for k, d_hbm in enumerate((d0_hbm, d1_hbm)):
            pltpu.sync_copy(d_hbm.at[pl.ds(_sc_worker_base(chunks), chunks)], idx.at[k])

        for k in range(TOP_K):
            @pl.loop(0, chunks, step=2)
            def _(j):
                gathers = [pltpu.async_copy(y_hbm.at[idx.at[k, j + b]], rows.at[b], gsem.at[b]) for b in bufs]
                writes = []
                for b in bufs:
                    off = base + (j + b) * SC_CHUNK
                    gathers[b].wait()
                    writes.append(pltpu.async_copy(rows.at[b], out_hbm.at[k, pl.ds(off, SC_CHUNK)], wsem.at[b]))
                for w in writes:
                    w.wait()

    return gather(y3, *[d.reshape(-1, SC_CHUNK) for d in dests])


def _combine_body(yk_ref, x1_ref, ri_ref, g_ref, *rest, tc, final_norm):
    out_ref = rest[-1]

    ri = ri_ref[...]
    w0 = ri[:, 2:3]
    w1 = ri[:, 3:4]
    x2 = x1_ref[...] + (w0 * _load_packed_rows(yk_ref.at[0], tc, F32) + w1 * _load_packed_rows(yk_ref.at[1], tc, F32))
    if final_norm:
        ms = jnp.mean(x2 * x2, axis=-1, keepdims=True)
        x2 = x2 * lax.rsqrt(ms + EPS) * g_ref[...]
    out_ref[...] = x2


def _combine(yk, x1, ri, g, out_prev, *, tok0, final_norm, tc=512):
    T = x1.shape[0]
    steps = yk.shape[1] // (tc * PACK_CHUNKS)
    blk0 = tok0 // tc
    in_specs = [
        pl.BlockSpec((TOP_K, tc * PACK_CHUNKS, LANES), lambda i: (0, i, 0)),
        pl.BlockSpec((tc, D_MODEL), lambda i: (i + blk0, 0)),
        pl.BlockSpec((tc, LANES), lambda i: (i + blk0, 0)),
        pl.BlockSpec((1, D_MODEL), lambda i: (0, 0)),
    ]
    args = [yk, x1, ri, g]
    aliases = {}
    if out_prev is not None:
        in_specs.append(pl.BlockSpec(memory_space=pl.ANY))
        args.append(out_prev)
        aliases = {len(args) - 1: 0}
    return pl.pallas_call(
        functools.partial(_combine_body, tc=tc, final_norm=final_norm),
        grid=(steps,),
        in_specs=in_specs,
        out_specs=pl.BlockSpec((tc, D_MODEL), lambda i: (i + blk0, 0)),
        out_shape=jax.ShapeDtypeStruct((T, D_MODEL), F32),
        input_output_aliases=aliases,
        compiler_params=_cparams(("arbitrary",)),
        name="combine",
    )(*args)


def _pad_lanes(a, width=LANES):
    return jnp.pad(a, ((0, 0), (0, width - a.shape[-1])))


def _layer(x2, batch, seq, g_mix, w_in, conv_w, conv_b, w_rg_a, b_rg_a, w_rg_x, b_rg_x, lam, b_forget,
           w_lru_out, w_attn_out, w_out, g_ffn, w_rgrp, b_rgrp, w_rexp, b_rexp, w1, w3, w2, g_out, final_norm):
    T = batch * seq
    row = lambda a: a.reshape(1, -1).astype(F32)

    fl0 = 5 * D_MODEL
    w_main = jnp.concatenate([w_in[:, :fl0], w_in[:, fl0 + N_HEADS:]], axis=1).astype(BF16)
    w_fl = _pad_lanes(w_in[:, fl0:fl0 + N_HEADS]).astype(BF16)
    z, fl = _inproj(x2, row(g_mix), w_main, w_fl)

    caug = _forget(fl, _pad_lanes(row(b_forget)), batch=batch, seq=seq)
    o = _attention(z, caug, batch=batch, seq=seq)

    hg = _lru(z, conv_w.astype(F32), row(conv_b), w_rg_a.astype(BF16), w_rg_x.astype(BF16),
              row(b_rg_a), row(b_rg_x), row(lam), batch=batch, seq=seq)

    wr = _pad_lanes(jnp.concatenate([w_rgrp, w_rexp], axis=1)).astype(BF16)
    br = _pad_lanes(jnp.concatenate([row(b_rgrp), row(b_rexp)], axis=1))
    x1, v, ri, rt, cnt = _merge(hg, o, z, x2, w_lru_out.astype(BF16), w_attn_out.astype(BF16),
                            w_out.astype(BF16), row(g_ffn), wr, br)

    v3 = v.reshape(T, PACK_CHUNKS, LANES)
    g_out = row(g_out)
    Tp = T // MOE_PARTS
    n_blocks = (Tp * TOP_K + N_EXPERTS * (EXPERT_ROWS - 1) + EXPERT_ROWS - 1) // EXPERT_ROWS
    blk_start = jnp.arange(n_blocks, dtype=jnp.int32) * EXPERT_ROWS
    out = None
    for part in range(MOE_PARTS):
        tok0 = part * Tp
        e = rt[0:TOP_K, tok0:tok0 + Tp].astype(jnp.int32)
        rank = rt[4:4 + TOP_K, tok0:tok0 + Tp].astype(jnp.int32)
        counts = cnt[part, 0, R_EXP0:R_EXP0 + N_EXPERTS].astype(jnp.int32)
        padded = (counts + EXPERT_ROWS - 1) // EXPERT_ROWS * EXPERT_ROWS
        pad_end = jnp.cumsum(padded)
        pad_start = pad_end - padded
        seg_start = jnp.zeros_like(e)
        for j in range(N_EXPERTS):
            seg_start = jnp.where(e == j, pad_start[j], seg_start)
        dest2 = (seg_start + rank).astype(jnp.int32)
        dests = [dest2[k] for k in range(TOP_K)]
        blk_expert = jnp.minimum(
            jnp.sum((pad_end[None, :] <= blk_start[:, None]).astype(jnp.int32), axis=1), N_EXPERTS - 1)
        n_used = (pad_end[-1:] // EXPERT_ROWS).astype(jnp.int32)
        fill = jnp.concatenate([pad_start + counts, padded - counts, n_used]).astype(jnp.int32)

        xs = _sc_scatter_rows(v3, dests, n_blocks * EXPERT_ROWS, tok0)
        xs = _fill_padding(fill, xs)
        y = _experts(blk_expert, n_used, xs.reshape(-1, LANES), w1, w3, w2)
        y3 = y.reshape(-1, PACK_CHUNKS, LANES)
        Tc = Tp // COMBINE_PARTS
        for sub in range(COMBINE_PARTS):
            yk = _sc_gather_rows(y3, [d[sub * Tc:(sub + 1) * Tc] for d in dests])
            out = _combine(yk.reshape(TOP_K, Tc * PACK_CHUNKS, LANES), x1, ri, g_out, out,
                           tok0=tok0 + sub * Tc, final_norm=final_norm)
    return out


def kernel(x, g_mix, w_in, conv_w, conv_b, w_rg_a, b_rg_a, w_rg_x, b_rg_x, lru_lambda, b_forget, w_lru_out, w_attn_out, w_out, g_ffn, w_route_group, b_route_group, w_route_expert, b_route_expert, w_exp_gate, w_exp_up, w_exp_down, g_final):
    batch, seq, _ = x.shape
    depth = g_mix.shape[0]
    x2 = x.reshape(batch * seq, D_MODEL)
    for l in range(depth):
        x2 = _layer(
            x2, batch, seq, g_mix[l], w_in[l], conv_w[l], conv_b[l], w_rg_a[l], b_rg_a[l], w_rg_x[l],
            b_rg_x[l], lru_lambda[l], b_forget[l], w_lru_out[l], w_attn_out[l], w_out[l], g_ffn[l],
            w_route_group[l], b_route_group[l], w_route_expert[l], b_route_expert[l],
            w_exp_gate[l], w_exp_up[l], w_exp_down[l], g_final, l == depth - 1)
    return x2.reshape(batch, seq, D_MODEL)
```
